```python
import math
import jax, jax.numpy as jnp
from jax import lax
import numpy as np


D_MODEL = 2048
BATCH = 2
SEQ = 4096
DEPTH = 1
DEC_BATCH = 4
DEC_SEQ = 2048
PAST_LEN = 128

GRID_W = 64
MIX_WIDTH = D_MODEL
HG_WIDTH = MIX_WIDTH // 2
HG_HEAD_DIM = 128
HG_HEADS = HG_WIDTH // HG_HEAD_DIM
CHUNK = 64
HEAD_DIM = 128
N_Q_HEADS = (MIX_WIDTH - HG_WIDTH) // HEAD_DIM
N_KV_HEADS = 2
Q_PER_KV = N_Q_HEADS // N_KV_HEADS
Q_BLOCK = 128
ROPE_THETA = 10000.0
ROPE_AXIS_PAIRS = HEAD_DIM // 4
N_EXPERTS = 32
TOP_K = 4
D_FF = D_MODEL
SWIGLU_LIMIT = 7.0
SWIGLU_ALPHA = 1.702
MOE_BLOCK = 128
PLE_DIM = 256
NORM_EPS = 1e-6
IN_COLS = 5 * HG_WIDTH + (N_Q_HEADS + 2 * N_KV_HEADS) * HEAD_DIM

kernel_name = 'hymba_hgrn2_axial_gqa_moe_encoder'


def rmsnorm(x, w):
    xf = x.astype(jnp.float32)
    y = xf * lax.rsqrt(jnp.mean(xf * xf, axis=-1, keepdims=True) + NORM_EPS)
    return (y * w.astype(jnp.float32)).astype(x.dtype)


def grid_rope_angles(T):
    rows = T // GRID_W
    row = jnp.repeat(jnp.arange(rows, dtype=jnp.float32), GRID_W)
    col = jnp.tile(jnp.arange(GRID_W, dtype=jnp.float32), rows)
    inv_freq = ROPE_THETA ** (-jnp.arange(ROPE_AXIS_PAIRS, dtype=jnp.float32) / ROPE_AXIS_PAIRS)
    return jnp.concatenate([row[:, None] * inv_freq, col[:, None] * inv_freq], axis=-1)


def apply_rope(x, ang):
    xf = x.astype(jnp.float32).reshape(x.shape[:-1] + (HEAD_DIM // 2, 2))
    cos = jnp.cos(ang)[None, :, None, :]
    sin = jnp.sin(ang)[None, :, None, :]
    xe, xo = xf[..., 0], xf[..., 1]
    out = jnp.stack([xe * cos - xo * sin, xe * sin + xo * cos], axis=-1)
    return out.reshape(x.shape).astype(x.dtype)


def hgrn2_chunk_scan(q, k, logf, v):
    B, T, H, dk = q.shape
    dv = v.shape[-1]
    nc = T // CHUNK

    def to_chunks(a):
        return a.reshape(B, nc, CHUNK, H, a.shape[-1]).transpose(1, 0, 3, 2, 4)

    mask = jnp.tril(jnp.ones((CHUNK, CHUNK), dtype=bool))

    def step(S, inp):
        qc, kc, lfc, vc = inp
        b = jnp.cumsum(lfc, axis=2)
        o_inter = jnp.einsum('bhtd,bhde->bhte', qc * jnp.exp(b), S)
        diff = b[:, :, :, None, :] - b[:, :, None, :, :]
        decay = jnp.exp(jnp.where(mask[:, :, None], diff, -jnp.inf))
        A = jnp.einsum('bhtsd,bhsd->bhts', qc[:, :, :, None, :] * decay, kc)
        o_intra = jnp.einsum('bhts,bhse->bhte', A, vc)
        b_last = b[:, :, -1:, :]
        S_new = jnp.exp(b_last[:, :, 0, :])[..., None] * S + jnp.einsum(
            'bhsd,bhse->bhde', kc * jnp.exp(b_last - b), vc)
        return S_new, o_inter + o_intra

    S0 = jnp.zeros((B, H, dk, dv), jnp.float32)
    _, o = lax.scan(step, S0, (to_chunks(q), to_chunks(k), to_chunks(logf), to_chunks(v)))
    return o.transpose(1, 0, 3, 2, 4).reshape(B, T, H, dv)


def hgrn2_mixer(q_pre, f_fwd_pre, f_bwd_pre, i_pre, g_pre, lb, norm_w):
    B, T, _ = q_pre.shape

    def heads(a):
        return a.reshape(B, T, HG_HEADS, HG_HEAD_DIM)

    q = heads(jax.nn.silu(q_pre.astype(jnp.float32)))
    v = heads(i_pre.astype(jnp.float32))

    def gates(f_pre, lb_dir):
        f = lb_dir + (1.0 - lb_dir) * jax.nn.sigmoid(f_pre.astype(jnp.float32))
        return heads(1.0 - f), heads(jnp.log(f))

    k_f, lf_f = gates(f_fwd_pre, lb[0])
    k_b, lf_b = gates(f_bwd_pre, lb[1])
    rev = lambda a: jnp.flip(a, axis=1)
    o = hgrn2_chunk_scan(q, k_f, lf_f, v) + rev(hgrn2_chunk_scan(rev(q), rev(k_b), rev(lf_b), rev(v)))
    o = rmsnorm(o, norm_w).reshape(B, T, HG_WIDTH)
    return (o * jax.nn.silu(g_pre.astype(jnp.float32))).astype(q_pre.dtype)


def block_attention(q, k, v):
    B, T = q.shape[0], q.shape[1]
    nqb = T // Q_BLOCK
    qb = q.reshape(B, nqb, Q_BLOCK, N_KV_HEADS, Q_PER_KV, HEAD_DIM).transpose(1, 0, 3, 4, 2, 5)
    kt = k.transpose(0, 2, 1, 3)
    vt = v.transpose(0, 2, 1, 3)
    scale = HEAD_DIM ** -0.5

    def one_block(qblk):
        s = jnp.einsum('bkgqd,bksd->bkgqs', qblk, kt).astype(jnp.float32) * scale
        pr = jax.nn.softmax(s, axis=-1).astype(vt.dtype)
        return jnp.einsum('bkgqs,bksd->bkgqd', pr, vt)

    o = lax.map(one_block, qb)
    return o.transpose(1, 0, 4, 2, 3, 5).reshape(B, T, N_Q_HEADS * HEAD_DIM)


def moe(xn, router_w, router_b, w_gate_up, b_gate_up, w_down, b_down):
    B, T, D = xn.shape
    N = B * T
    xt = xn.reshape(N, D)
    logits = (xt @ router_w).astype(jnp.float32) + router_b.astype(jnp.float32)
    top_val, top_idx = lax.top_k(logits, TOP_K)
    gates = jax.nn.softmax(top_val, axis=-1).astype(xn.dtype)
    NK = N * TOP_K
    e_flat = top_idx.reshape(NK).astype(jnp.int32)
    g_flat = gates.reshape(NK)
    tok_flat = jnp.arange(NK, dtype=jnp.int32) // TOP_K
    e_sorted, order = lax.sort((e_flat, jnp.arange(NK, dtype=jnp.int32)), num_keys=1, is_stable=True)
    counts = jnp.zeros((N_EXPERTS,), jnp.int32).at[e_flat].add(1)
    starts = jnp.cumsum(counts) - counts
    padded = (counts + MOE_BLOCK - 1) // MOE_BLOCK * MOE_BLOCK
    pends = jnp.cumsum(padded)
    pstarts = pends - padded
    dest = pstarts[e_sorted] + jnp.arange(NK, dtype=jnp.int32) - starts[e_sorted]
    P = NK + N_EXPERTS * MOE_BLOCK
    NB = P // MOE_BLOCK
    slot_tok = jnp.zeros((P,), jnp.int32).at[dest].set(tok_flat[order])
    slot_gate = jnp.zeros((P,), g_flat.dtype).at[dest].set(g_flat[order])
    block_e = jnp.minimum(
        jnp.searchsorted(pends, jnp.arange(NB, dtype=jnp.int32) * MOE_BLOCK, side='right'),
        N_EXPERTS - 1).astype(jnp.int32)

    def expert_block(args):
        e, idx = args
        xb = xt[idx]
        gu = xb @ w_gate_up[e] + b_gate_up[e]
        gl, lin = gu[:, :D_FF], gu[:, D_FF:]
        gl = jnp.minimum(gl, SWIGLU_LIMIT)
        lin = jnp.clip(lin, -SWIGLU_LIMIT, SWIGLU_LIMIT)
        act = (lin + 1.0) * (gl * jax.nn.sigmoid(SWIGLU_ALPHA * gl))
        return act @ w_down[e] + b_down[e]

    out = lax.map(expert_block, (block_e, slot_tok.reshape(NB, MOE_BLOCK)))
    y = jnp.zeros_like(xt).at[slot_tok].add(out.reshape(P, D) * slot_gate[:, None])
    return y.reshape(B, T, D)


def encoder_layer(h, p, lb, mix_norm_w, w_in, hgrn_norm_w, q_norm_w, k_norm_w, w_out,
                  ffn_norm_w, router_w, router_b, w_gate_up, b_gate_up, w_down, b_down,
                  ple_proj, ple_norm_w, ple_gate):
    B, T, _ = h.shape
    xn = rmsnorm(h, mix_norm_w)
    proj = xn @ w_in
    sizes = (HG_WIDTH,) * 5 + (N_Q_HEADS * HEAD_DIM, N_KV_HEADS * HEAD_DIM, N_KV_HEADS * HEAD_DIM)
    bounds = [int(s) for s in np.cumsum(sizes)[:-1]]
    a_q, a_ff, a_fb, a_i, a_g, b_q, b_k, b_v = jnp.split(proj, bounds, axis=-1)
    o_a = hgrn2_mixer(a_q, a_ff, a_fb, a_i, a_g, lb, hgrn_norm_w)
    ang = grid_rope_angles(T)
    q = apply_rope(rmsnorm(b_q.reshape(B, T, N_Q_HEADS, HEAD_DIM), q_norm_w), ang)
    k = apply_rope(rmsnorm(b_k.reshape(B, T, N_KV_HEADS, HEAD_DIM), k_norm_w), ang)
    v = b_v.reshape(B, T, N_KV_HEADS, HEAD_DIM)
    o_b = block_attention(q, k, v)
    h = h + jnp.concatenate([o_a, o_b], axis=-1) @ w_out
    h = h + moe(rmsnorm(h, ffn_norm_w), router_w, router_b, w_gate_up, b_gate_up, w_down, b_down)
    gate = jax.nn.sigmoid((h @ ple_gate).astype(jnp.float32))
    pe = rmsnorm(p @ ple_proj, ple_norm_w).astype(jnp.float32)
    return h + (gate * pe).astype(h.dtype)


def trunk(x, p, mix_norm_w, w_in, hgrn_lb, hgrn_norm_w, q_norm_w, k_norm_w, w_out,
          ffn_norm_w, router_w, router_b, w_gate_up, b_gate_up, w_down, b_down,
          ple_proj, ple_norm_w, ple_gate):
    lb_all = jnp.cumsum(jax.nn.softmax(hgrn_lb.astype(jnp.float32), axis=0), axis=0)
    h = x
    for i in range(DEPTH):
        h = encoder_layer(h, p[i], lb_all[i], mix_norm_w[i], w_in[i], hgrn_norm_w[i], q_norm_w[i],
                          k_norm_w[i], w_out[i], ffn_norm_w[i], router_w[i], router_b[i],
                          w_gate_up[i], b_gate_up[i], w_down[i], b_down[i],
                          ple_proj[i], ple_norm_w[i], ple_gate[i])
    return h


def setup_inputs(seed: int = 0) -> dict:
    key = jax.random.key(seed)
    ks = jax.random.split(key, 24)
    f32 = jnp.float32
    nrm = lambda k, shape, s: jax.random.normal(k, shape, f32) * s
    gain = lambda k, shape: 1.0 + 0.01 * jax.random.normal(k, shape, f32)
    return {
        'x_prompt': nrm(ks[0], (BATCH, SEQ, D_MODEL), 1.0),
        'x_sample': nrm(ks[1], (DEC_BATCH, DEC_SEQ, D_MODEL), 1.0),
        'p_prompt': nrm(ks[2], (DEPTH, BATCH, SEQ, PLE_DIM), 1.0),
        'p_sample': nrm(ks[3], (DEPTH, DEC_BATCH, DEC_SEQ, PLE_DIM), 1.0),
        'mix_norm_w': gain(ks[4], (DEPTH, D_MODEL)),
        'w_in': nrm(ks[5], (DEPTH, D_MODEL, IN_COLS), D_MODEL ** -0.5),
        'hgrn_lb': nrm(ks[6], (DEPTH + 1, 2, HG_WIDTH), 0.5),
        'hgrn_norm_w': gain(ks[7], (DEPTH, HG_HEAD_DIM)),
        'q_norm_w': gain(ks[8], (DEPTH, HEAD_DIM)),
        'k_norm_w': gain(ks[9], (DEPTH, HEAD_DIM)),
        'w_out': nrm(ks[10], (DEPTH, MIX_WIDTH, D_MODEL), MIX_WIDTH ** -0.5),
        'ffn_norm_w': gain(ks[11], (DEPTH, D_MODEL)),
        'router_w': nrm(ks[12], (DEPTH, D_MODEL, N_EXPERTS), D_MODEL ** -0.5),
        'router_b': nrm(ks[13], (DEPTH, N_EXPERTS), 0.01),
        'w_gate_up': nrm(ks[14], (DEPTH, N_EXPERTS, D_MODEL, 2 * D_FF), D_MODEL ** -0.5),
        'b_gate_up': nrm(ks[15], (DEPTH, N_EXPERTS, 2 * D_FF), 0.01),
        'w_down': nrm(ks[16], (DEPTH, N_EXPERTS, D_FF, D_MODEL), D_FF ** -0.5),
        'b_down': nrm(ks[17], (DEPTH, N_EXPERTS, D_MODEL), 0.01),
        'ple_proj': nrm(ks[18], (DEPTH, PLE_DIM, D_MODEL), PLE_DIM ** -0.5),
        'ple_norm_w': gain(ks[19], (DEPTH, D_MODEL)),
        'ple_gate': nrm(ks[20], (DEPTH, D_MODEL, D_MODEL), D_MODEL ** -0.5),
    }


def reference(x_prompt, x_sample, p_prompt, p_sample, mix_norm_w, w_in, hgrn_lb, hgrn_norm_w,
              q_norm_w, k_norm_w, w_out, ffn_norm_w, router_w, router_b, w_gate_up, b_gate_up,
              w_down, b_down, ple_proj, ple_norm_w, ple_gate):
    y_prompt = trunk(x_prompt, p_prompt, mix_norm_w, w_in, hgrn_lb, hgrn_norm_w, q_norm_w, k_norm_w,
                     w_out, ffn_norm_w, router_w, router_b, w_gate_up, b_gate_up, w_down, b_down,
                     ple_proj, ple_norm_w, ple_gate)
    y_sample = trunk(x_sample, p_sample, mix_norm_w, w_in, hgrn_lb, hgrn_norm_w, q_norm_w, k_norm_w,
                     w_out, ffn_norm_w, router_w, router_b, w_gate_up, b_gate_up, w_down, b_down,
                     ple_proj, ple_norm_w, ple_gate)
    return (y_prompt, y_sample)
```

```python
import functools

import numpy as np
import jax
import jax.numpy as jnp
from jax import lax
from jax.experimental import pallas as pl
from jax.experimental.pallas import tpu as pltpu

F32 = jnp.float32
BF16 = jnp.bfloat16

D_MODEL = 2048
GRID_W = 64
HG_WIDTH = 1024
HG_HEAD_DIM = 128
HG_HEADS = HG_WIDTH // HG_HEAD_DIM
CHUNK = 64
HEAD_DIM = 128
N_Q_HEADS = 8
N_KV_HEADS = 2
Q_PER_KV = N_Q_HEADS // N_KV_HEADS
ROPE_THETA = 10000.0
ROPE_AXIS_PAIRS = HEAD_DIM // 4
N_EXPERTS = 32
TOP_K = 4
D_FF = D_MODEL
SWIGLU_LIMIT = 7.0
SWIGLU_ALPHA = 1.702
PLE_DIM = 256
NORM_EPS = 1e-6
Q_COLS = N_Q_HEADS * HEAD_DIM
KV_COLS = N_KV_HEADS * HEAD_DIM
IN_COLS = 5 * HG_WIDTH + Q_COLS + 2 * KV_COLS

LANES = 128
V7X_VMEM_BUDGET_BYTES = 56 * 1024 * 1024

IN_TM, IN_TN = 1024, 512
QK_TM = 512
ATT_TQ = 256
HG_TB = 256
OUT_TM = 256
MOE_R = 512
MOE_TF = 512
FIN_TM = 256
ROW_CHUNK = 128
N_LEVELS = 6


def _params(sem, vmem_bytes):
    return pltpu.CompilerParams(dimension_semantics=sem,
                                vmem_limit_bytes=min(int(vmem_bytes), V7X_VMEM_BUDGET_BYTES))


def _rms_scale(x):
    return lax.rsqrt(jnp.mean(x * x, axis=-1, keepdims=True) + NORM_EPS)


def _dot_nt(a, b):
    return lax.dot_general(a, b, (((1,), (1,)), ((), ())), preferred_element_type=F32)


def _in_proj_kernel(x_ref, nw_ref, w_ref, o_ref, xn_ref):
    @pl.when(pl.program_id(1) == 0)
    def _():
        def body(c, carry):
            r = pl.multiple_of(c * ROW_CHUNK, ROW_CHUNK)
            x = x_ref[pl.ds(r, ROW_CHUNK), :]
            xn_ref[pl.ds(r, ROW_CHUNK), :] = (x * _rms_scale(x) * nw_ref[...]).astype(BF16)
            return carry
        lax.fori_loop(0, x_ref.shape[0] // ROW_CHUNK, body, 0)

    o_ref[...] = jnp.dot(xn_ref[...], w_ref[...], preferred_element_type=F32)


def _in_proj(x, norm_w, w_bf16):
    n, d = x.shape
    cols = w_bf16.shape[1]
    tm, tn = min(IN_TM, n), IN_TN
    vmem = 2 * tm * d * 4 + tm * d * 2 + 2 * d * tn * 2 + 2 * tm * tn * 4 + (8 << 20)
    return pl.pallas_call(
        _in_proj_kernel,
        grid=(n // tm, cols // tn),
        in_specs=[pl.BlockSpec((tm, d), lambda i, j: (i, 0)),
                  pl.BlockSpec((1, d), lambda i, j: (0, 0)),
                  pl.BlockSpec((d, tn), lambda i, j: (0, j))],
        out_specs=pl.BlockSpec((tm, tn), lambda i, j: (i, j)),
        out_shape=jax.ShapeDtypeStruct((n, cols), F32),
        scratch_shapes=[pltpu.VMEM((tm, d), BF16)],
        compiler_params=_params(("parallel", "arbitrary"), vmem),
        name="in_proj",
    )(x, norm_w.reshape(1, d), w_bf16)


def _rope_tables(t_max):
    rows = t_max // GRID_W
    row = jnp.repeat(jnp.arange(rows, dtype=F32), GRID_W)
    col = jnp.tile(jnp.arange(GRID_W, dtype=F32), rows)
    inv_freq = ROPE_THETA ** (-jnp.arange(ROPE_AXIS_PAIRS, dtype=F32) / ROPE_AXIS_PAIRS)
    ang = jnp.concatenate([row[:, None] * inv_freq, col[:, None] * inv_freq], axis=-1)
    cos = jnp.repeat(jnp.cos(ang), 2, axis=-1)
    sin = jnp.repeat(jnp.sin(ang), 2, axis=-1)
    even = (jnp.arange(HEAD_DIM) % 2 == 0)[None, :]
    s_next = jnp.where(even, -sin, 0.0)
    s_prev = jnp.where(even, 0.0, sin)
    return cos, s_next, s_prev


def _qk_rope_kernel(q_ref, k_ref, v_ref, c_ref, sn_ref, sp_ref, qw_ref, kw_ref,
                    qo_ref, ko_ref, vo_ref):
    c, sn, sp = c_ref[...], sn_ref[...], sp_ref[...]

    def norm_rope(x, w):
        y = x * _rms_scale(x) * w
        return y * c + pltpu.roll(y, HEAD_DIM - 1, 1) * sn + pltpu.roll(y, 1, 1) * sp

    scale = HEAD_DIM ** -0.5
    for h in range(N_Q_HEADS):
        sl = slice(h * HEAD_DIM, (h + 1) * HEAD_DIM)
        qo_ref[:, sl] = (norm_rope(q_ref[:, sl], qw_ref[...]) * scale).astype(BF16)
    for h in range(N_KV_HEADS):
        sl = slice(h * HEAD_DIM, (h + 1) * HEAD_DIM)
        ko_ref[:, sl] = norm_rope(k_ref[:, sl], kw_ref[...]).astype(BF16)
    vo_ref[...] = v_ref[...].astype(BF16)


def _qk_rope(proj, q_norm_w, k_norm_w, groups):
    n = proj.shape[0]
    tm = QK_TM
    t_max = max(t for _, t in groups)
    cos, s_next, s_prev = _rope_tables(t_max)

    def pos_block(i):
        blk = jnp.int32(0)
        start = 0
        for n_tok, t in groups:
            first = start // tm
            blk = jnp.where(i >= first, (i - first) % (t // tm), blk)
            start += n_tok
        return blk

    tab_spec = pl.BlockSpec((tm, HEAD_DIM), lambda i: (pos_block(i), 0))
    q_blk, k_blk, v_blk = (5 * HG_WIDTH) // Q_COLS, (5 * HG_WIDTH + Q_COLS) // KV_COLS, \
        (5 * HG_WIDTH + Q_COLS + KV_COLS) // KV_COLS
    return pl.pallas_call(
        _qk_rope_kernel,
        grid=(n // tm,),
        in_specs=[pl.BlockSpec((tm, Q_COLS), lambda i: (i, q_blk)),
                  pl.BlockSpec((tm, KV_COLS), lambda i: (i, k_blk)),
                  pl.BlockSpec((tm, KV_COLS), lambda i: (i, v_blk)),
                  tab_spec, tab_spec, tab_spec,
                  pl.BlockSpec((1, HEAD_DIM), lambda i: (0, 0)),
                  pl.BlockSpec((1, HEAD_DIM), lambda i: (0, 0))],
        out_specs=[pl.BlockSpec((tm, Q_COLS), lambda i: (i, 0)),
                   pl.BlockSpec((tm, KV_COLS), lambda i: (i, 0)),
                   pl.BlockSpec((tm, KV_COLS), lambda i: (i, 0))],
        out_shape=[jax.ShapeDtypeStruct((n, Q_COLS), BF16),
                   jax.ShapeDtypeStruct((n, KV_COLS), BF16),
                   jax.ShapeDtypeStruct((n, KV_COLS), BF16)],
        compiler_params=_params(("parallel",), 32 << 20),
        name="qk_rope",
    )(proj, proj, proj, cos, s_next, s_prev,
      q_norm_w.reshape(1, HEAD_DIM), k_norm_w.reshape(1, HEAD_DIM))


def _attn_kernel(q_ref, k_ref, v_ref, o_ref):
    k = k_ref[...]
    v = v_ref[...]
    for h in range(Q_PER_KV):
        sl = slice(h * HEAD_DIM, (h + 1) * HEAD_DIM)
        s = _dot_nt(q_ref[:, sl], k)
        p = jnp.exp(s - jnp.max(s, axis=-1, keepdims=True))
        l = jnp.sum(p, axis=-1, keepdims=True)
        o = jnp.dot(p.astype(BF16), v, preferred_element_type=F32)
        o_ref[:, sl] = (o / l).astype(BF16)


def _attention(q, k, v, tok0, batch, t):
    tq = ATT_TQ
    nq = t // tq
    row0, seq0 = tok0 // tq, tok0 // t
    gcols = Q_PER_KV * HEAD_DIM
    vmem = 4 * t * HEAD_DIM * 2 + 3 * tq * t * 4 + (8 << 20)
    return pl.pallas_call(
        _attn_kernel,
        grid=(batch, N_KV_HEADS, nq),
        in_specs=[pl.BlockSpec((tq, gcols), lambda b, g, i: (row0 + b * nq + i, g)),
                  pl.BlockSpec((t, HEAD_DIM), lambda b, g, i: (seq0 + b, g)),
                  pl.BlockSpec((t, HEAD_DIM), lambda b, g, i: (seq0 + b, g))],
        out_specs=pl.BlockSpec((tq, gcols), lambda b, g, i: (b * nq + i, g)),
        out_shape=jax.ShapeDtypeStruct((batch * t, Q_COLS), BF16),
        compiler_params=_params(("parallel", "parallel", "arbitrary"), vmem),
        name="attention",
    )(q, k, v)


def _hgrn_constants():
    c = CHUNK
    t = np.arange(c)[:, None]
    u = np.arange(c)[None, :]
    sums = [(u <= t), (u > t)]
    masks = []
    for lvl in range(N_LEVELS):
        half = (c // 2) >> lvl
        bound = (t // (2 * half)) * (2 * half) + half - 1
        late = t > bound
        sums.append(np.where(late, (u > bound) & (u <= t), (u > t) & (u <= bound)))
        same = (t // (2 * half)) == (u // (2 * half))
        masks.append(same & late & (u <= bound))
    masks.append(t == u)
    sums = np.stack(sums).astype(np.float32)
    masks = np.stack(masks).astype(np.float32)
    both_s = np.stack([sums, sums[:, ::-1, ::-1]]).reshape(2, -1, c)
    both_m = np.stack([masks, masks[:, ::-1, ::-1]]).reshape(2, -1, c)
    return jnp.asarray(both_s, BF16), jnp.asarray(both_m, F32)


def _hgrn_kernel(q_ref, f_ref, v_ref, lb_ref, sums_ref, masks_ref, o_ref, st_ref, *, nchunk):
    d = pl.program_id(0)

    @pl.when(pl.program_id(2) == 0)
    def _():
        st_ref[...] = jnp.zeros_like(st_ref)

    lb = lb_ref[0]
    sums = sums_ref[0]
    c = CHUNK

    def chunk_body(j, carry):
        jj = j + d * (nchunk - 1 - 2 * j)
        r = pl.multiple_of(jj * c, c)
        rows = pl.ds(r, c)
        for h in range(HG_HEADS):
            sl = slice(h * HG_HEAD_DIM, (h + 1) * HG_HEAD_DIM)
            lbh = lb[:, sl]
            f = lbh + (1.0 - lbh) * jax.nn.sigmoid(f_ref[rows, sl])
            kh = 1.0 - f
            lf = jnp.log(f)
            qpre = q_ref[rows, sl]
            qh = qpre * jax.nn.sigmoid(qpre)
            vb = v_ref[rows, sl].astype(BF16)
            lf_hi = lf.astype(BF16)
            lf_lo = (lf - lf_hi.astype(F32)).astype(BF16)
            dec = (jnp.dot(sums, lf_hi, preferred_element_type=F32)
                   + jnp.dot(sums, lf_lo, preferred_element_type=F32))
            e = jnp.exp(dec)
            st = st_ref[h]
            o = _dot_nt((qh * e[0:c]).astype(BF16), st.astype(BF16))
            a = masks_ref[0, N_LEVELS * c:(N_LEVELS + 1) * c, :] * _dot_nt(qh.astype(BF16), kh.astype(BF16))
            for lvl in range(N_LEVELS):
                el = e[(2 + lvl) * c:(3 + lvl) * c]
                a = a + masks_ref[0, lvl * c:(lvl + 1) * c, :] * _dot_nt(
                    (qh * el).astype(BF16), (kh * el).astype(BF16))
            o = o + jnp.dot(a.astype(BF16), vb, preferred_element_type=F32)
            kd = (kh * e[c:2 * c]).astype(BF16)
            ut = lax.dot_general(vb, kd, (((0,), (0,)), ((), ())), preferred_element_type=F32)
            g = jnp.exp(jnp.sum(lf, axis=0, keepdims=True))
            st_ref[h] = st * g + ut
            o_ref[0, rows, sl] = o
        return carry

    lax.fori_loop(0, nchunk, chunk_body, 0)


def _hgrn(proj, lb, sums, masks, tok0, batch, t):
    tb = HG_TB
    nb = t // tb
    row0 = tok0 // tb
    nchunk = tb // CHUNK

    def rows(d, b, i):
        return row0 + b * nb + i + d * (nb - 1 - 2 * i)

    in_blk = lambda col: pl.BlockSpec((tb, HG_WIDTH), lambda d, b, i: (rows(d, b, i), col(d)))
    return pl.pallas_call(
        functools.partial(_hgrn_kernel, nchunk=nchunk),
        grid=(2, batch, nb),
        in_specs=[in_blk(lambda d: 0), in_blk(lambda d: 1 + d), in_blk(lambda d: 3),
                  pl.BlockSpec((1, 1, HG_WIDTH), lambda d, b, i: (d, 0, 0)),
                  pl.BlockSpec((1,) + sums.shape[1:], lambda d, b, i: (d, 0, 0)),
                  pl.BlockSpec((1,) + masks.shape[1:], lambda d, b, i: (d, 0, 0))],
        out_specs=pl.BlockSpec((1, tb, HG_WIDTH), lambda d, b, i: (d, rows(d, b, i) - row0, 0)),
        out_shape=jax.ShapeDtypeStruct((2, batch * t, HG_WIDTH), F32),
        scratch_shapes=[pltpu.VMEM((HG_HEADS, HG_HEAD_DIM, HG_HEAD_DIM), F32)],
        compiler_params=_params(("parallel", "parallel", "arbitrary"), 40 << 20),
        name="hgrn2",
    )(proj, proj, proj, lb.reshape(2, 1, HG_WIDTH), sums, masks)


def _out_router_kernel(of_ref, ob_ref, g_ref, att_ref, x_ref, hw_ref, wo_ref, fw_ref, rw_ref, rb_ref,
                       tri_ref, h_ref, xn_ref, eidx_ref, rank_ref, gate_ref, cnt_ref, cnt_acc):
    @pl.when(pl.program_id(0) == 0)
    def _():
        cnt_acc[...] = jnp.zeros_like(cnt_acc)

    o = of_ref[0] + ob_ref[0]
    parts = []
    for h in range(HG_HEADS):
        sl = slice(h * HG_HEAD_DIM, (h + 1) * HG_HEAD_DIM)
        oh = o[:, sl]
        gh = g_ref[:, sl]
        parts.append(((oh * _rms_scale(oh) * hw_ref[...]) * (gh * jax.nn.sigmoid(gh))).astype(BF16))
    oa = jnp.concatenate(parts, axis=1)
    h1 = (x_ref[...]
          + jnp.dot(oa, wo_ref[0:HG_WIDTH, :], preferred_element_type=F32)
          + jnp.dot(att_ref[...], wo_ref[HG_WIDTH:, :], preferred_element_type=F32))
    h_ref[...] = h1
    xn = h1 * _rms_scale(h1) * fw_ref[...]
    xn_ref[...] = xn

    logits = jnp.dot(xn, rw_ref[...], precision=lax.Precision.HIGHEST,
                     preferred_element_type=F32) + rb_ref[...]
    tm = logits.shape[0]
    lane = lax.broadcasted_iota(jnp.int32, (tm, LANES), 1)
    work = logits
    vals, idxs = [], []
    for _ in range(TOP_K):
        m = jnp.max(work, axis=-1, keepdims=True)
        idx = jnp.min(jnp.where(work == m, lane, LANES), axis=-1, keepdims=True)
        vals.append(m)
        idxs.append(idx)
        work = jnp.where(lane == idx, -jnp.inf, work)
    exps = [jnp.exp(v - vals[0]) for v in vals]
    denom = exps[0] + exps[1] + exps[2] + exps[3]

    onehot = jnp.zeros((tm, LANES), F32)
    for idx in idxs:
        onehot = onehot + (lane == idx).astype(F32)
    before = jnp.dot(tri_ref[...], onehot.astype(BF16), preferred_element_type=F32) + cnt_acc[...]
    eidx = jnp.zeros((tm, LANES), jnp.int32)
    rank = jnp.zeros((tm, LANES), jnp.int32)
    gate = jnp.zeros((tm, LANES), F32)
    for kk in range(TOP_K):
        rk = jnp.sum(jnp.where(lane == idxs[kk], before, 0.0), axis=-1, keepdims=True)
        eidx = jnp.where(lane == kk, idxs[kk], eidx)
        rank = jnp.where(lane == kk, rk.astype(jnp.int32), rank)
        gate = jnp.where(lane == kk, exps[kk] / denom, gate)
    eidx_ref[...] = eidx
    rank_ref[...] = rank
    gate_ref[...] = gate
    cnt_acc[...] = cnt_acc[...] + jnp.sum(onehot, axis=0, keepdims=True)
    cnt_ref[...] = cnt_acc[...]


def _out_router(o_dirs, proj, att, x, hgrn_norm_w, w_out_bf16, ffn_norm_w, router_w, router_b):
    n, d = x.shape
    tm = OUT_TM
    rw = jnp.zeros((d, LANES), F32).at[:, :N_EXPERTS].set(router_w.astype(F32))
    rb = jnp.full((1, LANES), -1e30, F32).at[0, :N_EXPERTS].set(router_b.astype(F32))
    tri = jnp.asarray(np.tril(np.ones((tm, tm), np.float32), -1), BF16)
    g_blk = 4
    row = lambda i: (i, 0)
    const = lambda i: (0, 0)
    vmem = 2 * d * d * 2 + 12 * tm * d * 4 + (12 << 20)
    return pl.pallas_call(
        _out_router_kernel,
        grid=(n // tm,),
        in_specs=[pl.BlockSpec((1, tm, HG_WIDTH), lambda i: (0, i, 0)),
                  pl.BlockSpec((1, tm, HG_WIDTH), lambda i: (1, i, 0)),
                  pl.BlockSpec((tm, HG_WIDTH), lambda i: (i, g_blk)),
                  pl.BlockSpec((tm, Q_COLS), row),
                  pl.BlockSpec((tm, d), row),
                  pl.BlockSpec((1, HG_HEAD_DIM), const),
                  pl.BlockSpec((d, d), const),
                  pl.BlockSpec((1, d), const),
                  pl.BlockSpec((d, LANES), const),
                  pl.BlockSpec((1, LANES), const),
                  pl.BlockSpec((tm, tm), const)],
        out_specs=[pl.BlockSpec((tm, d), row), pl.BlockSpec((tm, d), row),
                   pl.BlockSpec((tm, LANES), row), pl.BlockSpec((tm, LANES), row),
                   pl.BlockSpec((tm, LANES), row), pl.BlockSpec((1, LANES), const)],
        out_shape=[jax.ShapeDtypeStruct((n, d), F32), jax.ShapeDtypeStruct((n, d), F32),
                   jax.ShapeDtypeStruct((n, LANES), jnp.int32), jax.ShapeDtypeStruct((n, LANES), jnp.int32),
                   jax.ShapeDtypeStruct((n, LANES), F32), jax.ShapeDtypeStruct((1, LANES), F32)],
        scratch_shapes=[pltpu.VMEM((1, LANES), F32)],
        compiler_params=_params(("arbitrary",), vmem),
        name="out_router",
    )(o_dirs, o_dirs, proj, att, x, hgrn_norm_w.reshape(1, HG_HEAD_DIM), w_out_bf16,
      ffn_norm_w.reshape(1, d), rw, rb, tri)


def _gather_rows(idx_ref, n_rows, src_hbm, dst, sem):
    def issue(r, carry):
        pltpu.make_async_copy(src_hbm.at[pl.ds(idx_ref(r), 1)], dst.at[pl.ds(r, 1)], sem).start()
        return carry
    lax.fori_loop(0, n_rows, issue, 0)
    pltpu.make_async_copy(src_hbm.at[pl.ds(0, n_rows)], dst, sem).wait()


def _moe_gather_kernel(nv_ref, idx_ref, x_hbm, o_ref, buf, sem):
    valid = pl.program_id(0) < nv_ref[0]

    @pl.when(valid)
    def _():
        _gather_rows(lambda r: idx_ref[0, 0, r], buf.shape[0], x_hbm, buf, sem)
        o_ref[...] = buf[...].astype(BF16)

    @pl.when(jnp.logical_not(valid))
    def _():
        o_ref[...] = jnp.zeros_like(o_ref)


def _moe_gather(n_valid, slot_tok, xn):
    nsb = slot_tok.shape[0]
    r, d = MOE_R, xn.shape[1]
    return pl.pallas_call(
        _moe_gather_kernel,
        grid_spec=pltpu.PrefetchScalarGridSpec(
            num_scalar_prefetch=1,
            grid=(nsb,),
            in_specs=[pl.BlockSpec((1, 1, r), lambda i, nv: (i, 0, 0), memory_space=pltpu.SMEM),
                      pl.BlockSpec(memory_space=pl.ANY)],
            out_specs=pl.BlockSpec((r, d), lambda i, nv: (i, 0)),
            scratch_shapes=[pltpu.VMEM((r, d), F32), pltpu.SemaphoreType.DMA]),
        out_shape=jax.ShapeDtypeStruct((nsb * r, d), BF16),
        compiler_params=_params(("arbitrary",), 24 << 20),
        name="moe_gather",
    )(n_valid, slot_tok, xn)


def _moe_kernel(sbe_ref, nv_ref, x_ref, wg_ref, wl_ref, bg_ref, bl_ref, wd_ref, bd_ref, o_ref):
    j = pl.program_id(1)
    valid = pl.program_id(0) < nv_ref[0]

    @pl.when(jnp.logical_and(jnp.logical_not(valid), j == 0))
    def _():
        o_ref[...] = jnp.zeros_like(o_ref)

    @pl.when(valid)
    def _():
        x = x_ref[...]
        gl = jnp.dot(x, wg_ref[0].astype(BF16), preferred_element_type=F32) + bg_ref[0]
        lin = jnp.dot(x, wl_ref[0].astype(BF16), preferred_element_type=F32) + bl_ref[0]
        gl = jnp.minimum(gl, SWIGLU_LIMIT)
        lin = jnp.clip(lin, -SWIGLU_LIMIT, SWIGLU_LIMIT)
        act = (lin + 1.0) * (gl * jax.nn.sigmoid(SWIGLU_ALPHA * gl))
        part = jnp.dot(act.astype(BF16), wd_ref[0].astype(BF16), preferred_element_type=F32)

        @pl.when(j == 0)
        def _():
            o_ref[...] = part + bd_ref[0]

        @pl.when(j > 0)
        def _():
            o_ref[...] += part


def _moe_experts(sb_expert, n_valid, xs, w_gate_up, b_gate_up, w_down, b_down):
    r, tf = MOE_R, MOE_TF
    d = xs.shape[1]
    nsb = xs.shape[0] // r
    nf = D_FF // tf

    def sb(i, nv):
        return jnp.minimum(i, nv[0] - 1)

    def ft(i, j, nv):
        return jnp.where(i < nv[0], j, nf - 1)

    vmem = 2 * (3 * d * tf * 4 + r * d * 2 + r * d * 4) + 3 * d * tf * 2 + 4 * r * tf * 4 + r * d * 4 + (4 << 20)
    return pl.pallas_call(
        _moe_kernel,
        grid_spec=pltpu.PrefetchScalarGridSpec(
            num_scalar_prefetch=2,
            grid=(nsb, nf),
            in_specs=[pl.BlockSpec((r, d), lambda i, j, e, nv: (sb(i, nv), 0)),
                      pl.BlockSpec((1, d, tf), lambda i, j, e, nv: (e[sb(i, nv)], 0, ft(i, j, nv))),
                      pl.BlockSpec((1, d, tf), lambda i, j, e, nv: (e[sb(i, nv)], 0, nf + ft(i, j, nv))),
                      pl.BlockSpec((1, 1, tf), lambda i, j, e, nv: (e[sb(i, nv)], 0, ft(i, j, nv))),
                      pl.BlockSpec((1, 1, tf), lambda i, j, e, nv: (e[sb(i, nv)], 0, nf + ft(i, j, nv))),
                      pl.BlockSpec((1, tf, d), lambda i, j, e, nv: (e[sb(i, nv)], ft(i, j, nv), 0)),
                      pl.BlockSpec((1, 1, d), lambda i, j, e, nv: (e[sb(i, nv)], 0, 0))],
            out_specs=pl.BlockSpec((r, d), lambda i, j, e, nv: (i, 0))),
        out_shape=jax.ShapeDtypeStruct((nsb * r, d), F32),
        compiler_params=_params(("arbitrary", "arbitrary"), vmem),
        name="moe_experts",
    )(sb_expert, n_valid, xs, w_gate_up, w_gate_up, b_gate_up.reshape(N_EXPERTS, 1, 2 * D_FF),
      b_gate_up.reshape(N_EXPERTS, 1, 2 * D_FF), w_down, b_down.reshape(N_EXPERTS, 1, d))


def _final_kernel(dest_ref, y_hbm, h_ref, gate_ref, p_ref, pp_ref, pnw_ref, pg_ref, o_ref, buf, sem):
    tm = h_ref.shape[0]
    _gather_rows(lambda r: dest_ref[0, 0, r], TOP_K * tm, y_hbm, buf, sem)
    h = h_ref[...]
    for kk in range(TOP_K):
        h = h + gate_ref[:, kk:kk + 1] * buf[kk * tm:(kk + 1) * tm, :]
    gate = jax.nn.sigmoid(jnp.dot(h.astype(BF16), pg_ref[...], preferred_element_type=F32))
    pe = jnp.dot(p_ref[...].astype(BF16), pp_ref[...], preferred_element_type=F32)
    pe = pe * _rms_scale(pe) * pnw_ref[...]
    o_ref[...] = h + gate * pe


def _final(dest, y_sorted, h1, gates, p, ple_proj_bf16, ple_norm_w, ple_gate_bf16):
    n, d = h1.shape
    tm = FIN_TM
    row = lambda i: (i, 0)
    const = lambda i: (0, 0)
    vmem = TOP_K * tm * d * 4 + 2 * d * d * 2 + 10 * tm * d * 4 + (8 << 20)
    return pl.pallas_call(
        _final_kernel,
        grid=(n // tm,),
        in_specs=[pl.BlockSpec((1, 1, TOP_K * tm), lambda i: (i, 0, 0), memory_space=pltpu.SMEM),
                  pl.BlockSpec(memory_space=pl.ANY),
                  pl.BlockSpec((tm, d), row),
                  pl.BlockSpec((tm, LANES), row),
                  pl.BlockSpec((tm, PLE_DIM), row),
                  pl.BlockSpec((PLE_DIM, d), const),
                  pl.BlockSpec((1, d), const),
                  pl.BlockSpec((d, d), const)],
        out_specs=pl.BlockSpec((tm, d), row),
        out_shape=jax.ShapeDtypeStruct((n, d), F32),
        scratch_shapes=[pltpu.VMEM((TOP_K * tm, d), F32), pltpu.SemaphoreType.DMA],
        compiler_params=_params(("arbitrary",), vmem),
        name="combine_final",
    )(dest, y_sorted, h1, gates, p, ple_proj_bf16, ple_norm_w.reshape(1, d), ple_gate_bf16)


def _routing(eidx, rank, counts, n_tokens):
    r = MOE_R
    nsb = (n_tokens * TOP_K) // r + N_EXPERTS
    cnt = counts[0, :N_EXPERTS].astype(jnp.int32)
    padded = (cnt + r - 1) // r * r
    pends = jnp.cumsum(padded)
    pstarts = pends - padded
    e = eidx[:, :TOP_K]
    dest = pstarts[e] + rank[:, :TOP_K]
    tok = jnp.broadcast_to(jnp.arange(n_tokens, dtype=jnp.int32)[:, None], dest.shape)
    slot_tok = jnp.zeros((nsb * r,), jnp.int32).at[dest.reshape(-1)].set(tok.reshape(-1))
    n_valid = (pends[-1] // r).reshape(1).astype(jnp.int32)
    sb_expert = jnp.minimum(
        jnp.searchsorted(pends, jnp.arange(nsb, dtype=jnp.int32) * r, side="right"),
        N_EXPERTS - 1).astype(jnp.int32)
    dest_tiles = dest.reshape(n_tokens // FIN_TM, FIN_TM, TOP_K).transpose(0, 2, 1).reshape(
        n_tokens // FIN_TM, 1, TOP_K * FIN_TM)
    return slot_tok.reshape(nsb, 1, r), sb_expert, n_valid, dest_tiles.astype(jnp.int32)


def _layer(x, p, groups, lb, mix_norm_w, w_in, hgrn_norm_w, q_norm_w, k_norm_w, w_out, ffn_norm_w,
           router_w, router_b, w_gate_up, b_gate_up, w_down, b_down, ple_proj, ple_norm_w, ple_gate):
    n = x.shape[0]
    proj = _in_proj(x, mix_norm_w, w_in.astype(BF16))
    q, k, v = _qk_rope(proj, q_norm_w, k_norm_w, groups)
    sums, masks = _hgrn_constants()
    o_dirs, att = [], []
    tok0 = 0
    for n_tok, t in groups:
        o_dirs.append(_hgrn(proj, lb, sums, masks, tok0, n_tok // t, t))
        att.append(_attention(q, k, v, tok0, n_tok // t, t))
        tok0 += n_tok
    o_dirs = jnp.concatenate(o_dirs, axis=1)
    att = jnp.concatenate(att, axis=0)
    h1, xn, eidx, rank, gates, counts = _out_router(
        o_dirs, proj, att, x, hgrn_norm_w, w_out.astype(BF16), ffn_norm_w, router_w, router_b)
    slot_tok, sb_expert, n_valid, dest = _routing(eidx, rank, counts, n)
    xs = _moe_gather(n_valid, slot_tok, xn)
    y_sorted = _moe_experts(sb_expert, n_valid, xs, w_gate_up, b_gate_up, w_down, b_down)
    return _final(dest, y_sorted, h1, gates, p, ple_proj.astype(BF16), ple_norm_w, ple_gate.astype(BF16))


def kernel(x_prompt, x_sample, p_prompt, p_sample, mix_norm_w, w_in, hgrn_lb, hgrn_norm_w, q_norm_w, k_norm_w, w_out, ffn_norm_w, router_w, router_b, w_gate_up, b_gate_up, w_down, b_down, ple_proj, ple_norm_w, ple_gate):
    depth = w_in.shape[0]
    groups = ((x_prompt.shape[0] * x_prompt.shape[1], x_prompt.shape[1]),
              (x_sample.shape[0] * x_sample.shape[1], x_sample.shape[1]))
    d = x_prompt.shape[-1]
    h = jnp.concatenate([x_prompt.reshape(-1, d), x_sample.reshape(-1, d)], axis=0)
    lb_all = jnp.cumsum(jax.nn.softmax(hgrn_lb.astype(F32), axis=0), axis=0)
    for i in range(depth):
        p = jnp.concatenate([p_prompt[i].reshape(-1, PLE_DIM), p_sample[i].reshape(-1, PLE_DIM)], axis=0)
        h = _layer(h, p, groups, lb_all[i], mix_norm_w[i], w_in[i], hgrn_norm_w[i], q_norm_w[i], k_norm_w[i],
                   w_out[i], ffn_norm_w[i], router_w[i], router_b[i], w_gate_up[i], b_gate_up[i],
                   w_down[i], b_down[i], ple_proj[i], ple_norm_w[i], ple_gate[i])
    n_prompt = groups[0][0]
    return h[:n_prompt].reshape(x_prompt.shape), h[n_prompt:].reshape(x_sample.shape)
```

```python
import functools

import numpy as np
import jax
import jax.numpy as jnp
from jax import lax
from jax.experimental import pallas as pl
from jax.experimental.pallas import tpu as pltpu

F32 = jnp.float32
BF16 = jnp.bfloat16

D_MODEL = 2048
GRID_W = 64
HG_WIDTH = 1024
HG_HEAD_DIM = 128
HG_HEADS = HG_WIDTH // HG_HEAD_DIM
CHUNK = 64
HEAD_DIM = 128
N_Q_HEADS = 8
N_KV_HEADS = 2
Q_PER_KV = N_Q_HEADS // N_KV_HEADS
ROPE_THETA = 10000.0
ROPE_AXIS_PAIRS = HEAD_DIM // 4
N_EXPERTS = 32
TOP_K = 4
D_FF = D_MODEL
SWIGLU_LIMIT = 7.0
SWIGLU_ALPHA = 1.702
PLE_DIM = 256
NORM_EPS = 1e-6
Q_COLS = N_Q_HEADS * HEAD_DIM
KV_COLS = N_KV_HEADS * HEAD_DIM
IN_COLS = 5 * HG_WIDTH + Q_COLS + 2 * KV_COLS

LANES = 128
V7X_VMEM_BUDGET_BYTES = 56 * 1024 * 1024

IN_TM, IN_TN = 1024, 512
QK_TM = 512
ATT_TQ = 256
HG_TB = 256
OUT_TM = 256
MOE_R = 512
MOE_TF = 512
FIN_TM = 256
ROW_CHUNK = 128
N_LEVELS = 6


def _params(sem, vmem_bytes):
    return pltpu.CompilerParams(dimension_semantics=sem,
                                vmem_limit_bytes=min(int(vmem_bytes), V7X_VMEM_BUDGET_BYTES))


def _rms_scale(x):
    return lax.rsqrt(jnp.mean(x * x, axis=-1, keepdims=True) + NORM_EPS)


def _dot_nt(a, b):
    return lax.dot_general(a, b, (((1,), (1,)), ((), ())), preferred_element_type=F32)


def _pair_specs(block, n_first):
    zeros = (0,) * (len(block) - 1)
    return [pl.BlockSpec(block, lambda i, *_: (jnp.minimum(i, n_first - 1),) + zeros),
            pl.BlockSpec(block, lambda i, *_: (jnp.maximum(i - n_first, 0),) + zeros)]


def _in_proj_kernel(xa_ref, xb_ref, nw_ref, w_ref, o_ref, xn_ref, *, n_first):
    @pl.when(pl.program_id(1) == 0)
    def _():
        first = pl.program_id(0) < n_first

        def body(c, carry):
            rows = pl.ds(pl.multiple_of(c * ROW_CHUNK, ROW_CHUNK), ROW_CHUNK)
            x = jnp.where(first, xa_ref[rows, :], xb_ref[rows, :])
            xn_ref[rows, :] = (x * _rms_scale(x) * nw_ref[...]).astype(BF16)
            return carry
        lax.fori_loop(0, xa_ref.shape[0] // ROW_CHUNK, body, 0)

    o_ref[...] = jnp.dot(xn_ref[...], w_ref[...], preferred_element_type=F32)


def _in_proj(xa, xb, norm_w, w_bf16):
    d = xa.shape[1]
    n = xa.shape[0] + xb.shape[0]
    cols = w_bf16.shape[1]
    tm, tn = IN_TM, IN_TN
    vmem = 4 * tm * d * 4 + tm * d * 2 + 2 * d * tn * 2 + 2 * tm * tn * 4 + (8 << 20)
    return pl.pallas_call(
        functools.partial(_in_proj_kernel, n_first=xa.shape[0] // tm),
        grid=(n // tm, cols // tn),
        in_specs=_pair_specs((tm, d), xa.shape[0] // tm) + [
            pl.BlockSpec((1, d), lambda i, j: (0, 0)),
            pl.BlockSpec((d, tn), lambda i, j: (0, j))],
        out_specs=pl.BlockSpec((tm, tn), lambda i, j: (i, j)),
        out_shape=jax.ShapeDtypeStruct((n, cols), F32),
        scratch_shapes=[pltpu.VMEM((tm, d), BF16)],
        compiler_params=_params(("parallel", "arbitrary"), vmem),
        name="in_proj",
    )(xa, xb, norm_w.reshape(1, d), w_bf16)


def _rope_tables(t_max):
    rows = t_max // GRID_W
    row = jnp.repeat(jnp.arange(rows, dtype=F32), GRID_W)
    col = jnp.tile(jnp.arange(GRID_W, dtype=F32), rows)
    inv_freq = ROPE_THETA ** (-jnp.arange(ROPE_AXIS_PAIRS, dtype=F32) / ROPE_AXIS_PAIRS)
    ang = jnp.concatenate([row[:, None] * inv_freq, col[:, None] * inv_freq], axis=-1)
    cos = jnp.repeat(jnp.cos(ang), 2, axis=-1)
    sin = jnp.repeat(jnp.sin(ang), 2, axis=-1)
    even = (jnp.arange(HEAD_DIM) % 2 == 0)[None, :]
    s_next = jnp.where(even, -sin, 0.0)
    s_prev = jnp.where(even, 0.0, sin)
    return cos, s_next, s_prev


def _qk_rope_kernel(q_ref, k_ref, v_ref, c_ref, sn_ref, sp_ref, qw_ref, kw_ref,
                    qo_ref, ko_ref, vo_ref):
    c, sn, sp = c_ref[...], sn_ref[...], sp_ref[...]

    def norm_rope(x, w):
        y = x * _rms_scale(x) * w
        return y * c + pltpu.roll(y, HEAD_DIM - 1, 1) * sn + pltpu.roll(y, 1, 1) * sp

    scale = HEAD_DIM ** -0.5
    for h in range(N_Q_HEADS):
        sl = slice(h * HEAD_DIM, (h + 1) * HEAD_DIM)
        qo_ref[:, sl] = (norm_rope(q_ref[:, sl], qw_ref[...]) * scale).astype(BF16)
    for h in range(N_KV_HEADS):
        sl = slice(h * HEAD_DIM, (h + 1) * HEAD_DIM)
        ko_ref[:, sl] = norm_rope(k_ref[:, sl], kw_ref[...]).astype(BF16)
    vo_ref[...] = v_ref[...].astype(BF16)


def _qk_rope(proj, q_norm_w, k_norm_w, groups):
    n = proj.shape[0]
    tm = QK_TM
    t_max = max(t for _, t in groups)
    cos, s_next, s_prev = _rope_tables(t_max)

    def pos_block(i):
        blk = jnp.int32(0)
        start = 0
        for n_tok, t in groups:
            first = start // tm
            blk = jnp.where(i >= first, (i - first) % (t // tm), blk)
            start += n_tok
        return blk

    tab_spec = pl.BlockSpec((tm, HEAD_DIM), lambda i: (pos_block(i), 0))
    q_blk, k_blk, v_blk = (5 * HG_WIDTH) // Q_COLS, (5 * HG_WIDTH + Q_COLS) // KV_COLS, \
        (5 * HG_WIDTH + Q_COLS + KV_COLS) // KV_COLS
    return pl.pallas_call(
        _qk_rope_kernel,
        grid=(n // tm,),
        in_specs=[pl.BlockSpec((tm, Q_COLS), lambda i: (i, q_blk)),
                  pl.BlockSpec((tm, KV_COLS), lambda i: (i, k_blk)),
                  pl.BlockSpec((tm, KV_COLS), lambda i: (i, v_blk)),
                  tab_spec, tab_spec, tab_spec,
                  pl.BlockSpec((1, HEAD_DIM), lambda i: (0, 0)),
                  pl.BlockSpec((1, HEAD_DIM), lambda i: (0, 0))],
        out_specs=[pl.BlockSpec((tm, Q_COLS), lambda i: (i, 0)),
                   pl.BlockSpec((tm, KV_COLS), lambda i: (i, 0)),
                   pl.BlockSpec((tm, KV_COLS), lambda i: (i, 0))],
        out_shape=[jax.ShapeDtypeStruct((n, Q_COLS), BF16),
                   jax.ShapeDtypeStruct((n, KV_COLS), BF16),
                   jax.ShapeDtypeStruct((n, KV_COLS), BF16)],
        compiler_params=_params(("parallel",), 32 << 20),
        name="qk_rope",
    )(proj, proj, proj, cos, s_next, s_prev,
      q_norm_w.reshape(1, HEAD_DIM), k_norm_w.reshape(1, HEAD_DIM))


def _attn_kernel(q_ref, k_ref, v_ref, o_ref):
    k = k_ref[...]
    v = v_ref[...]
    for h in range(Q_PER_KV):
        sl = slice(h * HEAD_DIM, (h + 1) * HEAD_DIM)
        s = _dot_nt(q_ref[:, sl], k)
        p = jnp.exp(s - jnp.max(s, axis=-1, keepdims=True))
        l = jnp.sum(p, axis=-1, keepdims=True)
        o = jnp.dot(p.astype(BF16), v, preferred_element_type=F32)
        o_ref[:, sl] = (o / l).astype(BF16)


def _attention(q, k, v, tok0, batch, t):
    tq = ATT_TQ
    nq = t // tq
    row0, seq0 = tok0 // tq, tok0 // t
    gcols = Q_PER_KV * HEAD_DIM
    vmem = 4 * t * HEAD_DIM * 2 + 3 * tq * t * 4 + (8 << 20)
    q_spec = pl.BlockSpec((tq, gcols), lambda b, g, i: (row0 + b * nq + i, g))
    kv_spec = pl.BlockSpec((t, HEAD_DIM), lambda b, g, i: (seq0 + b, g))
    return pl.pallas_call(
        _attn_kernel,
        grid=(batch, N_KV_HEADS, nq),
        in_specs=[q_spec, kv_spec, kv_spec],
        out_specs=q_spec,
        out_shape=jax.ShapeDtypeStruct(q.shape, BF16),
        input_output_aliases={0: 0},
        compiler_params=_params(("parallel", "parallel", "arbitrary"), vmem),
        name="attention",
    )(q, k, v)


def _hgrn_constants():
    c = CHUNK
    t = np.arange(c)[:, None]
    u = np.arange(c)[None, :]
    sums = [(u <= t), (u > t)]
    masks = []
    for lvl in range(N_LEVELS):
        half = (c // 2) >> lvl
        bound = (t // (2 * half)) * (2 * half) + half - 1
        late = t > bound
        sums.append(np.where(late, (u > bound) & (u <= t), (u > t) & (u <= bound)))
        same = (t // (2 * half)) == (u // (2 * half))
        masks.append(same & late & (u <= bound))
    masks.append(t == u)
    sums = np.stack(sums).astype(np.float32)
    masks = np.stack(masks).astype(np.float32)
    both_s = np.stack([sums, sums[:, ::-1, ::-1]]).reshape(2, -1, c)
    both_m = np.stack([masks, masks[:, ::-1, ::-1]]).reshape(2, -1, c)
    return jnp.asarray(both_s, BF16), jnp.asarray(both_m, F32)


def _hgrn_kernel(q_ref, f_ref, v_ref, lb_ref, sums_ref, masks_ref, o_ref, st_ref, *, nchunk, nblk,
                 seq_first_blk, seq_last_blk):
    d = pl.program_id(0)
    i = pl.program_id(1)
    blk = i + d * (nblk - 1 - 2 * i)
    fresh = functools.reduce(
        jnp.logical_or,
        [jnp.logical_or(jnp.logical_and(d == 0, blk == f), jnp.logical_and(d == 1, blk == l))
         for f, l in zip(seq_first_blk, seq_last_blk)])

    @pl.when(fresh)
    def _():
        st_ref[...] = jnp.zeros_like(st_ref)

    lb = lb_ref[0]
    sums = sums_ref[0]
    c = CHUNK

    def chunk_body(j, carry):
        jj = j + d * (nchunk - 1 - 2 * j)
        r = pl.multiple_of(jj * c, c)
        rows = pl.ds(r, c)
        for h in range(HG_HEADS):
            sl = slice(h * HG_HEAD_DIM, (h + 1) * HG_HEAD_DIM)
            lbh = lb[:, sl]
            f = lbh + (1.0 - lbh) * jax.nn.sigmoid(f_ref[rows, sl])
            kh = 1.0 - f
            lf = jnp.log(f)
            qpre = q_ref[rows, sl]
            qh = qpre * jax.nn.sigmoid(qpre)
            vb = v_ref[rows, sl].astype(BF16)
            lf_hi = lf.astype(BF16)
            lf_lo = (lf - lf_hi.astype(F32)).astype(BF16)
            dec = (jnp.dot(sums, lf_hi, preferred_element_type=F32)
                   + jnp.dot(sums, lf_lo, preferred_element_type=F32))
            e = jnp.exp(dec)
            st = st_ref[h]
            o = _dot_nt((qh * e[0:c]).astype(BF16), st.astype(BF16))
            a = masks_ref[0, N_LEVELS * c:(N_LEVELS + 1) * c, :] * _dot_nt(qh.astype(BF16), kh.astype(BF16))
            for lvl in range(N_LEVELS):
                el = e[(2 + lvl) * c:(3 + lvl) * c]
                a = a + masks_ref[0, lvl * c:(lvl + 1) * c, :] * _dot_nt(
                    (qh * el).astype(BF16), (kh * el).astype(BF16))
            o = o + jnp.dot(a.astype(BF16), vb, preferred_element_type=F32)
            kd = (kh * e[c:2 * c]).astype(BF16)
            ut = lax.dot_general(vb, kd, (((0,), (0,)), ((), ())), preferred_element_type=F32)
            g = jnp.exp(jnp.sum(lf, axis=0, keepdims=True))
            st_ref[h] = st * g + ut
            o_ref[0, rows, sl] = o
        return carry

    lax.fori_loop(0, nchunk, chunk_body, 0)


def _hgrn(proj, lb, sums, masks, groups):
    n = proj.shape[0]
    tb = HG_TB
    nblk = n // tb
    first_blk, last_blk, tok = [], [], 0
    for n_tok, t in groups:
        for s in range(n_tok // t):
            first_blk.append((tok + s * t) // tb)
            last_blk.append((tok + (s + 1) * t) // tb - 1)
        tok += n_tok

    def rows(d, i):
        return i + d * (nblk - 1 - 2 * i)

    in_blk = lambda col: pl.BlockSpec((tb, HG_WIDTH), lambda d, i: (rows(d, i), col(d)))
    return pl.pallas_call(
        functools.partial(_hgrn_kernel, nchunk=tb // CHUNK, nblk=nblk,
                          seq_first_blk=tuple(first_blk), seq_last_blk=tuple(last_blk)),
        grid=(2, nblk),
        in_specs=[in_blk(lambda d: 0), in_blk(lambda d: 1 + d), in_blk(lambda d: 3),
                  pl.BlockSpec((1, 1, HG_WIDTH), lambda d, i: (d, 0, 0)),
                  pl.BlockSpec((1,) + sums.shape[1:], lambda d, i: (d, 0, 0)),
                  pl.BlockSpec((1,) + masks.shape[1:], lambda d, i: (d, 0, 0))],
        out_specs=pl.BlockSpec((1, tb, HG_WIDTH), lambda d, i: (d, rows(d, i), 0)),
        out_shape=jax.ShapeDtypeStruct((2, n, HG_WIDTH), F32),
        scratch_shapes=[pltpu.VMEM((HG_HEADS, HG_HEAD_DIM, HG_HEAD_DIM), F32)],
        compiler_params=_params(("parallel", "arbitrary"), 40 << 20),
        name="hgrn2",
    )(proj, proj, proj, lb.reshape(2, 1, HG_WIDTH), sums, masks)


def _out_router_kernel(xa_ref, xb_ref, of_ref, ob_ref, g_ref, att_ref, hw_ref, wo_ref, fw_ref, rw_ref, rb_ref,
                       tri_ref, h_ref, xn_ref, eidx_ref, rank_ref, gate_ref, cnt_ref, cnt_acc, *, n_first):
    @pl.when(pl.program_id(0) == 0)
    def _():
        cnt_acc[...] = jnp.zeros_like(cnt_acc)

    o = of_ref[0] + ob_ref[0]
    parts = []
    for h in range(HG_HEADS):
        sl = slice(h * HG_HEAD_DIM, (h + 1) * HG_HEAD_DIM)
        oh = o[:, sl]
        gh = g_ref[:, sl]
        parts.append(((oh * _rms_scale(oh) * hw_ref[...]) * (gh * jax.nn.sigmoid(gh))).astype(BF16))
    oa = jnp.concatenate(parts, axis=1)
    h1 = (jnp.where(pl.program_id(0) < n_first, xa_ref[...], xb_ref[...])
          + jnp.dot(oa, wo_ref[0:HG_WIDTH, :], preferred_element_type=F32)
          + jnp.dot(att_ref[...], wo_ref[HG_WIDTH:, :], preferred_element_type=F32))
    h_ref[...] = h1
    xn = h1 * _rms_scale(h1) * fw_ref[...]
    xn_ref[...] = xn

    logits = jnp.dot(xn, rw_ref[...], precision=lax.Precision.HIGHEST,
                     preferred_element_type=F32) + rb_ref[...]
    tm = logits.shape[0]
    lane = lax.broadcasted_iota(jnp.int32, (tm, LANES), 1)
    work = logits
    vals, idxs = [], []
    for _ in range(TOP_K):
        m = jnp.max(work, axis=-1, keepdims=True)
        idx = jnp.min(jnp.where(work == m, lane, LANES), axis=-1, keepdims=True)
        vals.append(m)
        idxs.append(idx)
        work = jnp.where(lane == idx, -jnp.inf, work)
    exps = [jnp.exp(v - vals[0]) for v in vals]
    denom = exps[0] + exps[1] + exps[2] + exps[3]

    onehot = jnp.zeros((tm, LANES), F32)
    for idx in idxs:
        onehot = onehot + (lane == idx).astype(F32)
    before = jnp.dot(tri_ref[...], onehot.astype(BF16), preferred_element_type=F32) + cnt_acc[...]
    eidx = jnp.zeros((tm, LANES), jnp.int32)
    rank = jnp.zeros((tm, LANES), jnp.int32)
    gate = jnp.zeros((tm, LANES), F32)
    for kk in range(TOP_K):
        rk = jnp.sum(jnp.where(lane == idxs[kk], before, 0.0), axis=-1, keepdims=True)
        eidx = jnp.where(lane == kk, idxs[kk], eidx)
        rank = jnp.where(lane == kk, rk.astype(jnp.int32), rank)
        gate = jnp.where(lane == kk, exps[kk] / denom, gate)
    eidx_ref[...] = eidx
    rank_ref[...] = rank
    gate_ref[...] = gate
    cnt_acc[...] = cnt_acc[...] + jnp.sum(onehot, axis=0, keepdims=True)
    cnt_ref[...] = cnt_acc[...]


def _out_router(xa, xb, o_dirs, proj, att, hgrn_norm_w, w_out_bf16, ffn_norm_w, router_w, router_b):
    d = xa.shape[1]
    n = xa.shape[0] + xb.shape[0]
    tm = OUT_TM
    rw = jnp.zeros((d, LANES), F32).at[:, :N_EXPERTS].set(router_w.astype(F32))
    rb = jnp.full((1, LANES), -1e30, F32).at[0, :N_EXPERTS].set(router_b.astype(F32))
    tri = jnp.asarray(np.tril(np.ones((tm, tm), np.float32), -1), BF16)
    g_blk = 4
    row = lambda i: (i, 0)
    const = lambda i: (0, 0)
    vmem = 2 * d * d * 2 + 12 * tm * d * 4 + (12 << 20)
    return pl.pallas_call(
        functools.partial(_out_router_kernel, n_first=xa.shape[0] // tm),
        grid=(n // tm,),
        in_specs=_pair_specs((tm, d), xa.shape[0] // tm) + [
                  pl.BlockSpec((1, tm, HG_WIDTH), lambda i: (0, i, 0)),
                  pl.BlockSpec((1, tm, HG_WIDTH), lambda i: (1, i, 0)),
                  pl.BlockSpec((tm, HG_WIDTH), lambda i: (i, g_blk)),
                  pl.BlockSpec((tm, Q_COLS), row),
                  pl.BlockSpec((1, HG_HEAD_DIM), const),
                  pl.BlockSpec((d, d), const),
                  pl.BlockSpec((1, d), const),
                  pl.BlockSpec((d, LANES), const),
                  pl.BlockSpec((1, LANES), const),
                  pl.BlockSpec((tm, tm), const)],
        out_specs=[pl.BlockSpec((tm, d), row), pl.BlockSpec((tm, d), row),
                   pl.BlockSpec((tm, LANES), row), pl.BlockSpec((tm, LANES), row),
                   pl.BlockSpec((tm, LANES), row), pl.BlockSpec((1, LANES), const)],
        out_shape=[jax.ShapeDtypeStruct((n, d), F32), jax.ShapeDtypeStruct((n, d), F32),
                   jax.ShapeDtypeStruct((n, LANES), jnp.int32), jax.ShapeDtypeStruct((n, LANES), jnp.int32),
                   jax.ShapeDtypeStruct((n, LANES), F32), jax.ShapeDtypeStruct((1, LANES), F32)],
        scratch_shapes=[pltpu.VMEM((1, LANES), F32)],
        compiler_params=_params(("arbitrary",), vmem),
        name="out_router",
    )(xa, xb, o_dirs, o_dirs, proj, att, hgrn_norm_w.reshape(1, HG_HEAD_DIM), w_out_bf16,
      ffn_norm_w.reshape(1, d), rw, rb, tri)


def _gather_rows(idx_ref, n_rows, src_hbm, dst, sem):
    def issue(r, carry):
        pltpu.make_async_copy(src_hbm.at[pl.ds(idx_ref(r), 1)], dst.at[pl.ds(r, 1)], sem).start()
        return carry
    lax.fori_loop(0, n_rows, issue, 0)
    pltpu.make_async_copy(src_hbm.at[pl.ds(0, n_rows)], dst, sem).wait()


def _moe_gather_kernel(nv_ref, idx_ref, x_hbm, o_ref, buf, sem):
    valid = pl.program_id(0) < nv_ref[0]

    @pl.when(valid)
    def _():
        _gather_rows(lambda r: idx_ref[0, 0, r], buf.shape[0], x_hbm, buf, sem)
        o_ref[...] = buf[...].astype(BF16)

    @pl.when(jnp.logical_not(valid))
    def _():
        o_ref[...] = jnp.zeros_like(o_ref)


def _moe_gather(n_valid, slot_tok, xn):
    nsb = slot_tok.shape[0]
    r, d = MOE_R, xn.shape[1]
    return pl.pallas_call(
        _moe_gather_kernel,
        grid_spec=pltpu.PrefetchScalarGridSpec(
            num_scalar_prefetch=1,
            grid=(nsb,),
            in_specs=[pl.BlockSpec((1, 1, r), lambda i, nv: (i, 0, 0), memory_space=pltpu.SMEM),
                      pl.BlockSpec(memory_space=pl.ANY)],
            out_specs=pl.BlockSpec((r, d), lambda i, nv: (i, 0)),
            scratch_shapes=[pltpu.VMEM((r, d), F32), pltpu.SemaphoreType.DMA]),
        out_shape=jax.ShapeDtypeStruct((nsb * r, d), BF16),
        compiler_params=_params(("arbitrary",), 24 << 20),
        name="moe_gather",
    )(n_valid, slot_tok, xn)


def _moe_kernel(sbe_ref, nv_ref, x_ref, wg_ref, wl_ref, bg_ref, bl_ref, wd_ref, bd_ref, o_ref):
    j = pl.program_id(1)
    valid = pl.program_id(0) < nv_ref[0]

    @pl.when(jnp.logical_and(jnp.logical_not(valid), j == 0))
    def _():
        o_ref[...] = jnp.zeros_like(o_ref)

    @pl.when(valid)
    def _():
        x = x_ref[...]
        gl = jnp.dot(x, wg_ref[0].astype(BF16), preferred_element_type=F32) + bg_ref[0]
        lin = jnp.dot(x, wl_ref[0].astype(BF16), preferred_element_type=F32) + bl_ref[0]
        gl = jnp.minimum(gl, SWIGLU_LIMIT)
        lin = jnp.clip(lin, -SWIGLU_LIMIT, SWIGLU_LIMIT)
        act = (lin + 1.0) * (gl * jax.nn.sigmoid(SWIGLU_ALPHA * gl))
        part = jnp.dot(act.astype(BF16), wd_ref[0].astype(BF16), preferred_element_type=F32)

        @pl.when(j == 0)
        def _():
            o_ref[...] = part + bd_ref[0]

        @pl.when(j > 0)
        def _():
            o_ref[...] += part


def _moe_experts(sb_expert, n_valid, xs, w_gate_up, b_gate_up, w_down, b_down):
    r, tf = MOE_R, MOE_TF
    d = xs.shape[1]
    nsb = xs.shape[0] // r
    nf = D_FF // tf

    def sb(i, nv):
        return jnp.minimum(i, nv[0] - 1)

    def ft(i, j, nv):
        return jnp.where(i < nv[0], j, nf - 1)

    vmem = 2 * (3 * d * tf * 4 + r * d * 2 + r * d * 4) + 3 * d * tf * 2 + 4 * r * tf * 4 + r * d * 4 + (4 << 20)
    return pl.pallas_call(
        _moe_kernel,
        grid_spec=pltpu.PrefetchScalarGridSpec(
            num_scalar_prefetch=2,
            grid=(nsb, nf),
            in_specs=[pl.BlockSpec((r, d), lambda i, j, e, nv: (sb(i, nv), 0)),
                      pl.BlockSpec((1, d, tf), lambda i, j, e, nv: (e[sb(i, nv)], 0, ft(i, j, nv))),
                      pl.BlockSpec((1, d, tf), lambda i, j, e, nv: (e[sb(i, nv)], 0, nf + ft(i, j, nv))),
                      pl.BlockSpec((1, 1, tf), lambda i, j, e, nv: (e[sb(i, nv)], 0, ft(i, j, nv))),
                      pl.BlockSpec((1, 1, tf), lambda i, j, e, nv: (e[sb(i, nv)], 0, nf + ft(i, j, nv))),
                      pl.BlockSpec((1, tf, d), lambda i, j, e, nv: (e[sb(i, nv)], ft(i, j, nv), 0)),
                      pl.BlockSpec((1, 1, d), lambda i, j, e, nv: (e[sb(i, nv)], 0, 0))],
            out_specs=pl.BlockSpec((r, d), lambda i, j, e, nv: (i, 0))),
        out_shape=jax.ShapeDtypeStruct((nsb * r, d), F32),
        compiler_params=_params(("arbitrary", "arbitrary"), vmem),
        name="moe_experts",
    )(sb_expert, n_valid, xs, w_gate_up, w_gate_up, b_gate_up.reshape(N_EXPERTS, 1, 2 * D_FF),
      b_gate_up.reshape(N_EXPERTS, 1, 2 * D_FF), w_down, b_down.reshape(N_EXPERTS, 1, d))


def _final_kernel(dest_ref, y_hbm, h_ref, gate_ref, pa_ref, pb_ref, pp_ref, pnw_ref, pg_ref,
                  oa_ref, ob_ref, buf, sem, *, n_first):
    tm = h_ref.shape[0]
    first = pl.program_id(0) < n_first
    _gather_rows(lambda r: dest_ref[0, 0, r], TOP_K * tm, y_hbm, buf, sem)
    h = h_ref[...]
    for kk in range(TOP_K):
        h = h + gate_ref[:, kk:kk + 1] * buf[kk * tm:(kk + 1) * tm, :]
    gate = jax.nn.sigmoid(jnp.dot(h.astype(BF16), pg_ref[...], preferred_element_type=F32))
    p = jnp.where(first, pa_ref[...], pb_ref[...])
    pe = jnp.dot(p.astype(BF16), pp_ref[...], preferred_element_type=F32)
    pe = pe * _rms_scale(pe) * pnw_ref[...]
    out = h + gate * pe

    @pl.when(first)
    def _():
        oa_ref[...] = out

    @pl.when(jnp.logical_not(first))
    def _():
        ob_ref[...] = out


def _final(dest, y_sorted, h1, gates, pa, pb, ple_proj_bf16, ple_norm_w, ple_gate_bf16):
    n, d = h1.shape
    tm = FIN_TM
    n_first = pa.shape[0] // tm
    row = lambda i: (i, 0)
    const = lambda i: (0, 0)
    vmem = TOP_K * tm * d * 4 + 2 * d * d * 2 + 12 * tm * d * 4 + (8 << 20)
    return pl.pallas_call(
        functools.partial(_final_kernel, n_first=n_first),
        grid=(n // tm,),
        in_specs=[pl.BlockSpec((1, 1, TOP_K * tm), lambda i: (i, 0, 0), memory_space=pltpu.SMEM),
                  pl.BlockSpec(memory_space=pl.ANY),
                  pl.BlockSpec((tm, d), row),
                  pl.BlockSpec((tm, LANES), row)]
        + _pair_specs((tm, PLE_DIM), n_first)
        + [pl.BlockSpec((PLE_DIM, d), const),
           pl.BlockSpec((1, d), const),
           pl.BlockSpec((d, d), const)],
        out_specs=_pair_specs((tm, d), n_first),
        out_shape=[jax.ShapeDtypeStruct((pa.shape[0], d), F32), jax.ShapeDtypeStruct((pb.shape[0], d), F32)],
        scratch_shapes=[pltpu.VMEM((TOP_K * tm, d), F32), pltpu.SemaphoreType.DMA],
        compiler_params=_params(("arbitrary",), vmem),
        name="combine_final",
    )(dest, y_sorted, h1, gates, pa, pb, ple_proj_bf16, ple_norm_w.reshape(1, d), ple_gate_bf16)


def _routing(eidx, rank, counts, n_tokens):
    r = MOE_R
    nsb = (n_tokens * TOP_K) // r + N_EXPERTS
    cnt = counts[0, :N_EXPERTS].astype(jnp.int32)
    padded = (cnt + r - 1) // r * r
    pends = jnp.cumsum(padded)
    pstarts = pends - padded
    e = eidx[:, :TOP_K]
    dest = pstarts[e] + rank[:, :TOP_K]
    tok = jnp.broadcast_to(jnp.arange(n_tokens, dtype=jnp.int32)[:, None], dest.shape)
    slot_tok = jnp.zeros((nsb * r,), jnp.int32).at[dest.reshape(-1)].set(tok.reshape(-1))
    n_valid = (pends[-1] // r).reshape(1).astype(jnp.int32)
    sb_row = jnp.arange(nsb, dtype=jnp.int32) * r
    sb_expert = jnp.minimum(jnp.sum((pends[None, :] <= sb_row[:, None]).astype(jnp.int32), axis=1),
                            N_EXPERTS - 1)
    dest_tiles = dest.reshape(n_tokens // FIN_TM, FIN_TM, TOP_K).transpose(0, 2, 1).reshape(
        n_tokens // FIN_TM, 1, TOP_K * FIN_TM)
    return slot_tok.reshape(nsb, 1, r), sb_expert, n_valid, dest_tiles.astype(jnp.int32)


def _layer(xa, xb, pa, pb, groups, lb, mix_norm_w, w_in, hgrn_norm_w, q_norm_w, k_norm_w, w_out, ffn_norm_w,
           router_w, router_b, w_gate_up, b_gate_up, w_down, b_down, ple_proj, ple_norm_w, ple_gate):
    n = xa.shape[0] + xb.shape[0]
    proj = _in_proj(xa, xb, mix_norm_w, w_in.astype(BF16))
    att, k, v = _qk_rope(proj, q_norm_w, k_norm_w, groups)
    sums, masks = _hgrn_constants()
    o_dirs = _hgrn(proj, lb, sums, masks, groups)
    tok0 = 0
    for n_tok, t in groups:
        att = _attention(att, k, v, tok0, n_tok // t, t)
        tok0 += n_tok
    h1, xn, eidx, rank, gates, counts = _out_router(
        xa, xb, o_dirs, proj, att, hgrn_norm_w, w_out.astype(BF16), ffn_norm_w, router_w, router_b)
    slot_tok, sb_expert, n_valid, dest = _routing(eidx, rank, counts, n)
    xs = _moe_gather(n_valid, slot_tok, xn)
    y_sorted = _moe_experts(sb_expert, n_valid, xs, w_gate_up, b_gate_up, w_down, b_down)
    return _final(dest, y_sorted, h1, gates, pa, pb, ple_proj.astype(BF16), ple_norm_w, ple_gate.astype(BF16))


def kernel(x_prompt, x_sample, p_prompt, p_sample, mix_norm_w, w_in, hgrn_lb, hgrn_norm_w, q_norm_w, k_norm_w, w_out, ffn_norm_w, router_w, router_b, w_gate_up, b_gate_up, w_down, b_down, ple_proj, ple_norm_w, ple_gate):
    depth = w_in.shape[0]
    groups = ((x_prompt.shape[0] * x_prompt.shape[1], x_prompt.shape[1]),
              (x_sample.shape[0] * x_sample.shape[1], x_sample.shape[1]))
    d = x_prompt.shape[-1]
    ha, hb = x_prompt.reshape(-1, d), x_sample.reshape(-1, d)
    lb_all = jnp.cumsum(jax.nn.softmax(hgrn_lb.astype(F32), axis=0), axis=0)
    for i in range(depth):
        ha, hb = _layer(ha, hb, p_prompt[i].reshape(-1, PLE_DIM), p_sample[i].reshape(-1, PLE_DIM), groups,
                        lb_all[i], mix_norm_w[i], w_in[i], hgrn_norm_w[i], q_norm_w[i], k_norm_w[i],
                        w_out[i], ffn_norm_w[i], router_w[i], router_b[i], w_gate_up[i], b_gate_up[i],
                        w_down[i], b_down[i], ple_proj[i], ple_norm_w[i], ple_gate[i])
    return ha.reshape(x_prompt.shape), hb.reshape(x_sample.shape)
```

```python
import functools

import numpy as np
import jax
import jax.numpy as jnp
from jax import lax
from jax.experimental import pallas as pl
from jax.experimental.pallas import tpu as pltpu

F32 = jnp.float32
BF16 = jnp.bfloat16

D_MODEL = 2048
GRID_W = 64
HG_WIDTH = 1024
HG_HEAD_DIM = 128
HG_HEADS = HG_WIDTH // HG_HEAD_DIM
CHUNK = 64
HEAD_DIM = 128
N_Q_HEADS = 8
N_KV_HEADS = 2
Q_PER_KV = N_Q_HEADS // N_KV_HEADS
ROPE_THETA = 10000.0
ROPE_AXIS_PAIRS = HEAD_DIM // 4
N_EXPERTS = 32
TOP_K = 4
D_FF = D_MODEL
SWIGLU_LIMIT = 7.0
SWIGLU_ALPHA = 1.702
PLE_DIM = 256
NORM_EPS = 1e-6
Q_COLS = N_Q_HEADS * HEAD_DIM
KV_COLS = N_KV_HEADS * HEAD_DIM
IN_COLS = 5 * HG_WIDTH + Q_COLS + 2 * KV_COLS

LANES = 128
V7X_VMEM_BUDGET_BYTES = 56 * 1024 * 1024

IN_TM, IN_TN = 1024, 512
QK_TM = 512
ATT_TQ = 256
HG_TB = 256
OUT_TM = 256
MOE_R = 512
MOE_TF = 512
FIN_TM = 256
ROW_CHUNK = 128
N_LEVELS = 6


def _params(sem, vmem_bytes):
    return pltpu.CompilerParams(dimension_semantics=sem,
                                vmem_limit_bytes=min(int(vmem_bytes), V7X_VMEM_BUDGET_BYTES))


def _rms_scale(x):
    return lax.rsqrt(jnp.mean(x * x, axis=-1, keepdims=True) + NORM_EPS)


def _dot_nt(a, b):
    return lax.dot_general(a, b, (((1,), (1,)), ((), ())), preferred_element_type=F32)


def _pair_specs(block, n_first):
    zeros = (0,) * (len(block) - 1)
    return [pl.BlockSpec(block, lambda i, *_: (jnp.minimum(i, n_first - 1),) + zeros),
            pl.BlockSpec(block, lambda i, *_: (jnp.maximum(i - n_first, 0),) + zeros)]


def _in_proj_kernel(xa_ref, xb_ref, nw_ref, w_ref, o_ref, xn_ref, *, n_first):
    @pl.when(pl.program_id(1) == 0)
    def _():
        first = pl.program_id(0) < n_first

        def body(c, carry):
            rows = pl.ds(pl.multiple_of(c * ROW_CHUNK, ROW_CHUNK), ROW_CHUNK)
            x = jnp.where(first, xa_ref[rows, :], xb_ref[rows, :])
            xn_ref[rows, :] = (x * _rms_scale(x) * nw_ref[...]).astype(BF16)
            return carry
        lax.fori_loop(0, xa_ref.shape[0] // ROW_CHUNK, body, 0)

    o_ref[...] = jnp.dot(xn_ref[...], w_ref[...], preferred_element_type=F32)


def _in_proj(xa, xb, norm_w, w_bf16):
    d = xa.shape[1]
    n = xa.shape[0] + xb.shape[0]
    cols = w_bf16.shape[1]
    tm, tn = IN_TM, IN_TN
    vmem = 4 * tm * d * 4 + tm * d * 2 + 2 * d * tn * 2 + 2 * tm * tn * 4 + (8 << 20)
    return pl.pallas_call(
        functools.partial(_in_proj_kernel, n_first=xa.shape[0] // tm),
        grid=(n // tm, cols // tn),
        in_specs=_pair_specs((tm, d), xa.shape[0] // tm) + [
            pl.BlockSpec((1, d), lambda i, j: (0, 0)),
            pl.BlockSpec((d, tn), lambda i, j: (0, j))],
        out_specs=pl.BlockSpec((tm, tn), lambda i, j: (i, j)),
        out_shape=jax.ShapeDtypeStruct((n, cols), F32),
        scratch_shapes=[pltpu.VMEM((tm, d), BF16)],
        compiler_params=_params(("parallel", "arbitrary"), vmem),
        name="in_proj",
    )(xa, xb, norm_w.reshape(1, d), w_bf16)


def _rope_tables(t_max):
    rows = t_max // GRID_W
    row = jnp.repeat(jnp.arange(rows, dtype=F32), GRID_W)
    col = jnp.tile(jnp.arange(GRID_W, dtype=F32), rows)
    inv_freq = ROPE_THETA ** (-jnp.arange(ROPE_AXIS_PAIRS, dtype=F32) / ROPE_AXIS_PAIRS)
    ang = jnp.concatenate([row[:, None] * inv_freq, col[:, None] * inv_freq], axis=-1)
    cos = jnp.repeat(jnp.cos(ang), 2, axis=-1)
    sin = jnp.repeat(jnp.sin(ang), 2, axis=-1)
    even = (jnp.arange(HEAD_DIM) % 2 == 0)[None, :]
    s_next = jnp.where(even, -sin, 0.0)
    s_prev = jnp.where(even, 0.0, sin)
    return cos, s_next, s_prev


def _qk_rope_kernel(q_ref, k_ref, v_ref, c_ref, sn_ref, sp_ref, qw_ref, kw_ref,
                    qo_ref, ko_ref, vo_ref):
    c, sn, sp = c_ref[...], sn_ref[...], sp_ref[...]

    def norm_rope(x, w):
        y = x * _rms_scale(x) * w
        return y * c + pltpu.roll(y, HEAD_DIM - 1, 1) * sn + pltpu.roll(y, 1, 1) * sp

    scale = HEAD_DIM ** -0.5
    for h in range(N_Q_HEADS):
        sl = slice(h * HEAD_DIM, (h + 1) * HEAD_DIM)
        qo_ref[:, sl] = (norm_rope(q_ref[:, sl], qw_ref[...]) * scale).astype(BF16)
    for h in range(N_KV_HEADS):
        sl = slice(h * HEAD_DIM, (h + 1) * HEAD_DIM)
        ko_ref[:, sl] = norm_rope(k_ref[:, sl], kw_ref[...]).astype(BF16)
    vo_ref[...] = v_ref[...].astype(BF16)


def _qk_rope(proj, q_norm_w, k_norm_w, groups):
    n = proj.shape[0]
    tm = QK_TM
    t_max = max(t for _, t in groups)
    cos, s_next, s_prev = _rope_tables(t_max)

    def pos_block(i):
        blk = jnp.int32(0)
        start = 0
        for n_tok, t in groups:
            first = start // tm
            blk = jnp.where(i >= first, (i - first) % (t // tm), blk)
            start += n_tok
        return blk

    tab_spec = pl.BlockSpec((tm, HEAD_DIM), lambda i: (pos_block(i), 0))
    q_blk, k_blk, v_blk = (5 * HG_WIDTH) // Q_COLS, (5 * HG_WIDTH + Q_COLS) // KV_COLS, \
        (5 * HG_WIDTH + Q_COLS + KV_COLS) // KV_COLS
    return pl.pallas_call(
        _qk_rope_kernel,
        grid=(n // tm,),
        in_specs=[pl.BlockSpec((tm, Q_COLS), lambda i: (i, q_blk)),
                  pl.BlockSpec((tm, KV_COLS), lambda i: (i, k_blk)),
                  pl.BlockSpec((tm, KV_COLS), lambda i: (i, v_blk)),
                  tab_spec, tab_spec, tab_spec,
                  pl.BlockSpec((1, HEAD_DIM), lambda i: (0, 0)),
                  pl.BlockSpec((1, HEAD_DIM), lambda i: (0, 0))],
        out_specs=[pl.BlockSpec((tm, Q_COLS), lambda i: (i, 0)),
                   pl.BlockSpec((tm, KV_COLS), lambda i: (i, 0)),
                   pl.BlockSpec((tm, KV_COLS), lambda i: (i, 0))],
        out_shape=[jax.ShapeDtypeStruct((n, Q_COLS), BF16),
                   jax.ShapeDtypeStruct((n, KV_COLS), BF16),
                   jax.ShapeDtypeStruct((n, KV_COLS), BF16)],
        compiler_params=_params(("parallel",), 32 << 20),
        name="qk_rope",
    )(proj, proj, proj, cos, s_next, s_prev,
      q_norm_w.reshape(1, HEAD_DIM), k_norm_w.reshape(1, HEAD_DIM))


def _attn_kernel(q_ref, k_ref, v_ref, o_ref):
    k = k_ref[...]
    v = v_ref[...]
    for h in range(Q_PER_KV):
        sl = slice(h * HEAD_DIM, (h + 1) * HEAD_DIM)
        s = _dot_nt(q_ref[:, sl], k)
        p = jnp.exp(s - jnp.max(s, axis=-1, keepdims=True))
        l = jnp.sum(p, axis=-1, keepdims=True)
        o = jnp.dot(p.astype(BF16), v, preferred_element_type=F32)
        o_ref[:, sl] = (o / l).astype(BF16)


def _attention(q, k, v, tok0, batch, t):
    tq = ATT_TQ
    nq = t // tq
    row0, seq0 = tok0 // tq, tok0 // t
    gcols = Q_PER_KV * HEAD_DIM
    vmem = 4 * t * HEAD_DIM * 2 + 3 * tq * t * 4 + (8 << 20)
    q_spec = pl.BlockSpec((tq, gcols), lambda b, g, i: (row0 + b * nq + i, g))
    kv_spec = pl.BlockSpec((t, HEAD_DIM), lambda b, g, i: (seq0 + b, g))
    return pl.pallas_call(
        _attn_kernel,
        grid=(batch, N_KV_HEADS, nq),
        in_specs=[q_spec, kv_spec, kv_spec],
        out_specs=q_spec,
        out_shape=jax.ShapeDtypeStruct(q.shape, BF16),
        input_output_aliases={0: 0},
        compiler_params=_params(("parallel", "parallel", "arbitrary"), vmem),
        name="attention",
    )(q, k, v)


def _hgrn_constants():
    c = CHUNK
    t = np.arange(c)[:, None]
    u = np.arange(c)[None, :]
    sums = [(u <= t), (u > t)]
    masks = []
    for lvl in range(N_LEVELS):
        half = (c // 2) >> lvl
        bound = (t // (2 * half)) * (2 * half) + half - 1
        late = t > bound
        sums.append(np.where(late, (u > bound) & (u <= t), (u > t) & (u <= bound)))
        same = (t // (2 * half)) == (u // (2 * half))
        masks.append(same & late & (u <= bound))
    masks.append(t == u)
    sums = np.stack(sums).astype(np.float32)
    masks = np.stack(masks).astype(np.float32)
    both_s = np.stack([sums, sums[:, ::-1, ::-1]]).reshape(2, -1, c)
    both_m = np.stack([masks, masks[:, ::-1, ::-1]]).reshape(2, -1, c)
    return jnp.asarray(both_s, BF16), jnp.asarray(both_m, F32)


def _hgrn_kernel(q_ref, f_ref, v_ref, lb_ref, sums_ref, masks_ref, o_ref, st_ref, *, nchunk, nblk,
                 seq_first_blk, seq_last_blk):
    d = pl.program_id(0)
    i = pl.program_id(1)
    blk = i + d * (nblk - 1 - 2 * i)
    fresh = functools.reduce(
        jnp.logical_or,
        [jnp.logical_or(jnp.logical_and(d == 0, blk == f), jnp.logical_and(d == 1, blk == l))
         for f, l in zip(seq_first_blk, seq_last_blk)])

    @pl.when(fresh)
    def _():
        st_ref[...] = jnp.zeros_like(st_ref)

    lb = lb_ref[0]
    sums = sums_ref[0]
    c = CHUNK

    def chunk_body(j, carry):
        jj = j + d * (nchunk - 1 - 2 * j)
        rows = pl.ds(pl.multiple_of(jj * c, c), c)
        f = lb + (1.0 - lb) * jax.nn.sigmoid(f_ref[rows, :])
        k = 1.0 - f
        lf = jnp.log(f)
        qpre = q_ref[rows, :]
        q = qpre * jax.nn.sigmoid(qpre)
        vb = v_ref[rows, :].astype(BF16)
        lf_hi = lf.astype(BF16)
        lf_lo = (lf - lf_hi.astype(F32)).astype(BF16)
        e = jnp.exp(jnp.dot(sums, lf_hi, preferred_element_type=F32)
                    + jnp.dot(sums, lf_lo, preferred_element_type=F32))
        g = jnp.exp(jnp.sum(lf, axis=0, keepdims=True))
        qt = (q * e[0:c]).astype(BF16)
        kd = (k * e[c:2 * c]).astype(BF16)
        ql = [q.astype(BF16)] + [(q * e[(2 + lvl) * c:(3 + lvl) * c]).astype(BF16) for lvl in range(N_LEVELS)]
        kl = [k.astype(BF16)] + [(k * e[(2 + lvl) * c:(3 + lvl) * c]).astype(BF16) for lvl in range(N_LEVELS)]
        mk = [masks_ref[0, N_LEVELS * c:(N_LEVELS + 1) * c, :]] + [
            masks_ref[0, lvl * c:(lvl + 1) * c, :] for lvl in range(N_LEVELS)]
        heads = [slice(h * HG_HEAD_DIM, (h + 1) * HG_HEAD_DIM) for h in range(HG_HEADS)]
        scores = []
        for sl in heads:
            a = mk[0] * _dot_nt(ql[0][:, sl], kl[0][:, sl])
            for lvl in range(1, N_LEVELS + 1):
                a = a + mk[lvl] * _dot_nt(ql[lvl][:, sl], kl[lvl][:, sl])
            scores.append(a.astype(BF16))
        for h, sl in enumerate(heads):
            st = st_ref[h]
            o = _dot_nt(qt[:, sl], st.astype(BF16)) + jnp.dot(scores[h], vb[:, sl], preferred_element_type=F32)
            ut = lax.dot_general(vb[:, sl], kd[:, sl], (((0,), (0,)), ((), ())), preferred_element_type=F32)
            st_ref[h] = st * g[:, sl] + ut
            o_ref[0, rows, sl] = o
        return carry

    lax.fori_loop(0, nchunk, chunk_body, 0)


def _hgrn(proj, lb, sums, masks, groups):
    n = proj.shape[0]
    tb = HG_TB
    nblk = n // tb
    first_blk, last_blk, tok = [], [], 0
    for n_tok, t in groups:
        for s in range(n_tok // t):
            first_blk.append((tok + s * t) // tb)
            last_blk.append((tok + (s + 1) * t) // tb - 1)
        tok += n_tok

    def rows(d, i):
        return i + d * (nblk - 1 - 2 * i)

    in_blk = lambda col: pl.BlockSpec((tb, HG_WIDTH), lambda d, i: (rows(d, i), col(d)))
    return pl.pallas_call(
        functools.partial(_hgrn_kernel, nchunk=tb // CHUNK, nblk=nblk,
                          seq_first_blk=tuple(first_blk), seq_last_blk=tuple(last_blk)),
        grid=(2, nblk),
        in_specs=[in_blk(lambda d: 0), in_blk(lambda d: 1 + d), in_blk(lambda d: 3),
                  pl.BlockSpec((1, 1, HG_WIDTH), lambda d, i: (d, 0, 0)),
                  pl.BlockSpec((1,) + sums.shape[1:], lambda d, i: (d, 0, 0)),
                  pl.BlockSpec((1,) + masks.shape[1:], lambda d, i: (d, 0, 0))],
        out_specs=pl.BlockSpec((1, tb, HG_WIDTH), lambda d, i: (d, rows(d, i), 0)),
        out_shape=jax.ShapeDtypeStruct((2, n, HG_WIDTH), F32),
        scratch_shapes=[pltpu.VMEM((HG_HEADS, HG_HEAD_DIM, HG_HEAD_DIM), F32)],
        compiler_params=_params(("parallel", "arbitrary"), 40 << 20),
        name="hgrn2",
    )(proj, proj, proj, lb.reshape(2, 1, HG_WIDTH), sums, masks)


def _out_router_kernel(xa_ref, xb_ref, of_ref, ob_ref, g_ref, att_ref, hw_ref, wo_ref, fw_ref, rw_ref, rb_ref,
                       tri_ref, h_ref, xn_ref, eidx_ref, rank_ref, gate_ref, cnt_ref, cnt_acc, *, n_first):
    @pl.when(pl.program_id(0) == 0)
    def _():
        cnt_acc[...] = jnp.zeros_like(cnt_acc)

    o = of_ref[0] + ob_ref[0]
    parts = []
    for h in range(HG_HEADS):
        sl = slice(h * HG_HEAD_DIM, (h + 1) * HG_HEAD_DIM)
        oh = o[:, sl]
        gh = g_ref[:, sl]
        parts.append(((oh * _rms_scale(oh) * hw_ref[...]) * (gh * jax.nn.sigmoid(gh))).astype(BF16))
    oa = jnp.concatenate(parts, axis=1)
    h1 = (jnp.where(pl.program_id(0) < n_first, xa_ref[...], xb_ref[...])
          + jnp.dot(oa, wo_ref[0:HG_WIDTH, :], preferred_element_type=F32)
          + jnp.dot(att_ref[...], wo_ref[HG_WIDTH:, :], preferred_element_type=F32))
    h_ref[...] = h1
    xn = h1 * _rms_scale(h1) * fw_ref[...]
    for cb in range(xn_ref.shape[1]):
        xn_ref[:, cb, :] = xn[:, cb * LANES:(cb + 1) * LANES]

    logits = jnp.dot(xn, rw_ref[...], precision=lax.Precision.HIGHEST,
                     preferred_element_type=F32) + rb_ref[...]
    tm = logits.shape[0]
    lane = lax.broadcasted_iota(jnp.int32, (tm, LANES), 1)
    work = logits
    vals, idxs = [], []
    for _ in range(TOP_K):
        m = jnp.max(work, axis=-1, keepdims=True)
        idx = jnp.min(jnp.where(work == m, lane, LANES), axis=-1, keepdims=True)
        vals.append(m)
        idxs.append(idx)
        work = jnp.where(lane == idx, -jnp.inf, work)
    exps = [jnp.exp(v - vals[0]) for v in vals]
    denom = exps[0] + exps[1] + exps[2] + exps[3]

    onehot = jnp.zeros((tm, LANES), F32)
    for idx in idxs:
        onehot = onehot + (lane == idx).astype(F32)
    before = jnp.dot(tri_ref[...], onehot.astype(BF16), preferred_element_type=F32) + cnt_acc[...]
    eidx = jnp.zeros((tm, LANES), jnp.int32)
    rank = jnp.zeros((tm, LANES), jnp.int32)
    gate = jnp.zeros((tm, LANES), F32)
    for kk in range(TOP_K):
        rk = jnp.sum(jnp.where(lane == idxs[kk], before, 0.0), axis=-1, keepdims=True)
        eidx = jnp.where(lane == kk, idxs[kk], eidx)
        rank = jnp.where(lane == kk, rk.astype(jnp.int32), rank)
        gate = jnp.where(lane == kk, exps[kk] / denom, gate)
    eidx_ref[...] = eidx
    rank_ref[...] = rank
    gate_ref[...] = gate
    cnt_acc[...] = cnt_acc[...] + jnp.sum(onehot, axis=0, keepdims=True)
    cnt_ref[...] = cnt_acc[...]


def _out_router(xa, xb, o_dirs, proj, att, hgrn_norm_w, w_out_bf16, ffn_norm_w, router_w, router_b):
    d = xa.shape[1]
    n = xa.shape[0] + xb.shape[0]
    tm = OUT_TM
    rw = jnp.zeros((d, LANES), F32).at[:, :N_EXPERTS].set(router_w.astype(F32))
    rb = jnp.full((1, LANES), -1e30, F32).at[0, :N_EXPERTS].set(router_b.astype(F32))
    tri = jnp.asarray(np.tril(np.ones((tm, tm), np.float32), -1), BF16)
    g_blk = 4
    row = lambda i: (i, 0)
    const = lambda i: (0, 0)
    vmem = 2 * d * d * 2 + 12 * tm * d * 4 + (12 << 20)
    return pl.pallas_call(
        functools.partial(_out_router_kernel, n_first=xa.shape[0] // tm),
        grid=(n // tm,),
        in_specs=_pair_specs((tm, d), xa.shape[0] // tm) + [
                  pl.BlockSpec((1, tm, HG_WIDTH), lambda i: (0, i, 0)),
                  pl.BlockSpec((1, tm, HG_WIDTH), lambda i: (1, i, 0)),
                  pl.BlockSpec((tm, HG_WIDTH), lambda i: (i, g_blk)),
                  pl.BlockSpec((tm, Q_COLS), row),
                  pl.BlockSpec((1, HG_HEAD_DIM), const),
                  pl.BlockSpec((d, d), const),
                  pl.BlockSpec((1, d), const),
                  pl.BlockSpec((d, LANES), const),
                  pl.BlockSpec((1, LANES), const),
                  pl.BlockSpec((tm, tm), const)],
        out_specs=[pl.BlockSpec((tm, d), row), pl.BlockSpec((tm, d // LANES, LANES), lambda i: (i, 0, 0)),
                   pl.BlockSpec((tm, LANES), row), pl.BlockSpec((tm, LANES), row),
                   pl.BlockSpec((tm, LANES), row), pl.BlockSpec((1, LANES), const)],
        out_shape=[jax.ShapeDtypeStruct((n, d), F32), jax.ShapeDtypeStruct((n, d // LANES, LANES), F32),
                   jax.ShapeDtypeStruct((n, LANES), jnp.int32), jax.ShapeDtypeStruct((n, LANES), jnp.int32),
                   jax.ShapeDtypeStruct((n, LANES), F32), jax.ShapeDtypeStruct((1, LANES), F32)],
        scratch_shapes=[pltpu.VMEM((1, LANES), F32)],
        compiler_params=_params(("arbitrary",), vmem),
        name="out_router",
    )(xa, xb, o_dirs, o_dirs, proj, att, hgrn_norm_w.reshape(1, HG_HEAD_DIM), w_out_bf16,
      ffn_norm_w.reshape(1, d), rw, rb, tri)


ROW_PITCH = 20
GATHER_UNROLL = 8


def _gather_rows(idx, n_rows, src_hbm, buf, sem):
    slabs = src_hbm.shape[1]

    def issue(r, carry):
        pltpu.make_async_copy(src_hbm.at[idx(r)], buf.at[pl.ds(r * ROW_PITCH, slabs), :], sem).start()
        return carry
    lax.fori_loop(0, n_rows, issue, 0, unroll=GATHER_UNROLL)
    done = buf.at[pl.ds(0, n_rows * slabs), :]
    pltpu.make_async_copy(done, done, sem).wait()


def _gathered_cols(buf, row0, n_rows, cb):
    return buf[pl.ds(row0 * ROW_PITCH + cb, n_rows, stride=ROW_PITCH), :]


def _moe_gather_kernel(nv_ref, idx_ref, x_hbm, o_ref, buf, sem):
    valid = pl.program_id(0) < nv_ref[0]
    r = o_ref.shape[0]

    @pl.when(valid)
    def _():
        _gather_rows(lambda s: idx_ref[0, 0, s], r, x_hbm, buf, sem)
        for cb in range(x_hbm.shape[1]):
            o_ref[:, cb * LANES:(cb + 1) * LANES] = _gathered_cols(buf, 0, r, cb).astype(BF16)

    @pl.when(jnp.logical_not(valid))
    def _():
        o_ref[...] = jnp.zeros_like(o_ref)


def _moe_gather(n_valid, slot_tok, xn3):
    nsb = slot_tok.shape[0]
    r, d = MOE_R, xn3.shape[1] * xn3.shape[2]
    return pl.pallas_call(
        _moe_gather_kernel,
        grid_spec=pltpu.PrefetchScalarGridSpec(
            num_scalar_prefetch=1,
            grid=(nsb,),
            in_specs=[pl.BlockSpec((1, 1, r), lambda i, nv: (i, 0, 0), memory_space=pltpu.SMEM),
                      pl.BlockSpec(memory_space=pl.ANY)],
            out_specs=pl.BlockSpec((r, d), lambda i, nv: (i, 0)),
            scratch_shapes=[pltpu.VMEM((r * ROW_PITCH, LANES), F32), pltpu.SemaphoreType.DMA]),
        out_shape=jax.ShapeDtypeStruct((nsb * r, d), BF16),
        compiler_params=_params(("arbitrary",), 24 << 20),
        name="moe_gather",
    )(n_valid, slot_tok, xn3)


def _moe_kernel(sbe_ref, nv_ref, x_ref, wg_ref, wl_ref, bg_ref, bl_ref, wd_ref, bd_ref, o_ref, acc_ref):
    j = pl.program_id(1)
    valid = pl.program_id(0) < nv_ref[0]

    @pl.when(jnp.logical_and(jnp.logical_not(valid), j == 0))
    def _():
        o_ref[...] = jnp.zeros_like(o_ref)

    @pl.when(valid)
    def _():
        x = x_ref[...]
        gl = jnp.dot(x, wg_ref[0].astype(BF16), preferred_element_type=F32) + bg_ref[0]
        lin = jnp.dot(x, wl_ref[0].astype(BF16), preferred_element_type=F32) + bl_ref[0]
        gl = jnp.minimum(gl, SWIGLU_LIMIT)
        lin = jnp.clip(lin, -SWIGLU_LIMIT, SWIGLU_LIMIT)
        act = (lin + 1.0) * (gl * jax.nn.sigmoid(SWIGLU_ALPHA * gl))
        part = jnp.dot(act.astype(BF16), wd_ref[0].astype(BF16), preferred_element_type=F32)

        @pl.when(j == 0)
        def _():
            acc_ref[...] = part + bd_ref[0]

        @pl.when(j > 0)
        def _():
            acc_ref[...] += part

        @pl.when(j == pl.num_programs(1) - 1)
        def _():
            for cb in range(o_ref.shape[1]):
                o_ref[:, cb, :] = acc_ref[:, cb * LANES:(cb + 1) * LANES]


def _moe_experts(sb_expert, n_valid, xs, w_gate_up, b_gate_up, w_down, b_down):
    r, tf = MOE_R, MOE_TF
    d = xs.shape[1]
    nsb = xs.shape[0] // r
    nf = D_FF // tf

    def sb(i, nv):
        return jnp.maximum(jnp.minimum(i, nv[0] - 1), 0)

    def ft(i, j, nv):
        return jnp.where(i < nv[0], j, nf - 1)

    vmem = 2 * (3 * d * tf * 4 + r * d * 2 + r * d * 4) + 3 * d * tf * 2 + 4 * r * tf * 4 + 2 * r * d * 4 + (4 << 20)
    return pl.pallas_call(
        _moe_kernel,
        grid_spec=pltpu.PrefetchScalarGridSpec(
            num_scalar_prefetch=2,
            grid=(nsb, nf),
            in_specs=[pl.BlockSpec((r, d), lambda i, j, e, nv: (sb(i, nv), 0)),
                      pl.BlockSpec((1, d, tf), lambda i, j, e, nv: (e[sb(i, nv)], 0, ft(i, j, nv))),
                      pl.BlockSpec((1, d, tf), lambda i, j, e, nv: (e[sb(i, nv)], 0, nf + ft(i, j, nv))),
                      pl.BlockSpec((1, 1, tf), lambda i, j, e, nv: (e[sb(i, nv)], 0, ft(i, j, nv))),
                      pl.BlockSpec((1, 1, tf), lambda i, j, e, nv: (e[sb(i, nv)], 0, nf + ft(i, j, nv))),
                      pl.BlockSpec((1, tf, d), lambda i, j, e, nv: (e[sb(i, nv)], ft(i, j, nv), 0)),
                      pl.BlockSpec((1, 1, d), lambda i, j, e, nv: (e[sb(i, nv)], 0, 0))],
            out_specs=pl.BlockSpec((r, d // LANES, LANES), lambda i, j, e, nv: (i, 0, 0)),
            scratch_shapes=[pltpu.VMEM((r, d), F32)]),
        out_shape=jax.ShapeDtypeStruct((nsb * r, d // LANES, LANES), F32),
        compiler_params=_params(("arbitrary", "arbitrary"), vmem),
        name="moe_experts",
    )(sb_expert, n_valid, xs, w_gate_up, w_gate_up, b_gate_up.reshape(N_EXPERTS, 1, 2 * D_FF),
      b_gate_up.reshape(N_EXPERTS, 1, 2 * D_FF), w_down, b_down.reshape(N_EXPERTS, 1, d))


def _final_kernel(dest_ref, y_hbm, h_ref, gate_ref, pa_ref, pb_ref, pp_ref, pnw_ref, pg_ref,
                  oa_ref, ob_ref, buf, sem, *, n_first):
    tm = h_ref.shape[0]
    first = pl.program_id(0) < n_first
    _gather_rows(lambda r: dest_ref[0, 0, r], TOP_K * tm, y_hbm, buf, sem)
    gates = [gate_ref[:, kk:kk + 1] for kk in range(TOP_K)]
    cols = []
    for cb in range(y_hbm.shape[1]):
        hc = h_ref[:, cb * LANES:(cb + 1) * LANES]
        for kk in range(TOP_K):
            hc = hc + gates[kk] * _gathered_cols(buf, kk * tm, tm, cb)
        cols.append(hc)
    h = jnp.concatenate(cols, axis=1)
    gate = jax.nn.sigmoid(jnp.dot(h.astype(BF16), pg_ref[...], preferred_element_type=F32))
    p = jnp.where(first, pa_ref[...], pb_ref[...])
    pe = jnp.dot(p.astype(BF16), pp_ref[...], preferred_element_type=F32)
    pe = pe * _rms_scale(pe) * pnw_ref[...]
    out = h + gate * pe

    @pl.when(first)
    def _():
        oa_ref[...] = out

    @pl.when(jnp.logical_not(first))
    def _():
        ob_ref[...] = out


def _final(dest, y_sorted, h1, gates, pa, pb, ple_proj_bf16, ple_norm_w, ple_gate_bf16):
    n, d = h1.shape
    tm = FIN_TM
    n_first = pa.shape[0] // tm
    row = lambda i: (i, 0)
    const = lambda i: (0, 0)
    vmem = TOP_K * tm * ROW_PITCH * LANES * 4 + 2 * d * d * 2 + 12 * tm * d * 4 + (8 << 20)
    return pl.pallas_call(
        functools.partial(_final_kernel, n_first=n_first),
        grid=(n // tm,),
        in_specs=[pl.BlockSpec((1, 1, TOP_K * tm), lambda i: (i, 0, 0), memory_space=pltpu.SMEM),
                  pl.BlockSpec(memory_space=pl.ANY),
                  pl.BlockSpec((tm, d), row),
                  pl.BlockSpec((tm, LANES), row)]
        + _pair_specs((tm, PLE_DIM), n_first)
        + [pl.BlockSpec((PLE_DIM, d), const),
           pl.BlockSpec((1, d), const),
           pl.BlockSpec((d, d), const)],
        out_specs=_pair_specs((tm, d), n_first),
        out_shape=[jax.ShapeDtypeStruct((pa.shape[0], d), F32), jax.ShapeDtypeStruct((pb.shape[0], d), F32)],
        scratch_shapes=[pltpu.VMEM((TOP_K * tm * ROW_PITCH, LANES), F32), pltpu.SemaphoreType.DMA],
        compiler_params=_params(("arbitrary",), vmem),
        name="combine_final",
    )(dest, y_sorted, h1, gates, pa, pb, ple_proj_bf16, ple_norm_w.reshape(1, d), ple_gate_bf16)


def _routing(eidx, rank, counts, n_tokens):
    r = MOE_R
    nsb = (n_tokens * TOP_K) // r + N_EXPERTS
    cnt = counts[0, :N_EXPERTS].astype(jnp.int32)
    padded = (cnt + r - 1) // r * r
    pends = jnp.cumsum(padded)
    pstarts = pends - padded
    e = eidx[:, :TOP_K]
    dest = pstarts[e] + rank[:, :TOP_K]
    tok = jnp.broadcast_to(jnp.arange(n_tokens, dtype=jnp.int32)[:, None], dest.shape)
    slot_tok = jnp.zeros((nsb * r,), jnp.int32).at[dest.reshape(-1)].set(tok.reshape(-1))
    n_valid = (pends[-1] // r).reshape(1).astype(jnp.int32)
    sb_row = jnp.arange(nsb, dtype=jnp.int32) * r
    sb_expert = jnp.minimum(jnp.sum((pends[None, :] <= sb_row[:, None]).astype(jnp.int32), axis=1),
                            N_EXPERTS - 1)
    dest_tiles = dest.reshape(n_tokens // FIN_TM, FIN_TM, TOP_K).transpose(0, 2, 1).reshape(
        n_tokens // FIN_TM, 1, TOP_K * FIN_TM)
    return slot_tok.reshape(nsb, 1, r), sb_expert, n_valid, dest_tiles.astype(jnp.int32)


def _layer(xa, xb, pa, pb, groups, lb, mix_norm_w, w_in, hgrn_norm_w, q_norm_w, k_norm_w, w_out, ffn_norm_w,
           router_w, router_b, w_gate_up, b_gate_up, w_down, b_down, ple_proj, ple_norm_w, ple_gate):
    n = xa.shape[0] + xb.shape[0]
    proj = _in_proj(xa, xb, mix_norm_w, w_in.astype(BF16))
    att, k, v = _qk_rope(proj, q_norm_w, k_norm_w, groups)
    sums, masks = _hgrn_constants()
    o_dirs = _hgrn(proj, lb, sums, masks, groups)
    tok0 = 0
    for n_tok, t in groups:
        att = _attention(att, k, v, tok0, n_tok // t, t)
        tok0 += n_tok
    h1, xn, eidx, rank, gates, counts = _out_router(
        xa, xb, o_dirs, proj, att, hgrn_norm_w, w_out.astype(BF16), ffn_norm_w, router_w, router_b)
    slot_tok, sb_expert, n_valid, dest = _routing(eidx, rank, counts, n)
    xs = _moe_gather(n_valid, slot_tok, xn)
    y_sorted = _moe_experts(sb_expert, n_valid, xs, w_gate_up, b_gate_up, w_down, b_down)
    return _final(dest, y_sorted, h1, gates, pa, pb, ple_proj.astype(BF16), ple_norm_w, ple_gate.astype(BF16))


def kernel(x_prompt, x_sample, p_prompt, p_sample, mix_norm_w, w_in, hgrn_lb, hgrn_norm_w, q_norm_w, k_norm_w, w_out, ffn_norm_w, router_w, router_b, w_gate_up, b_gate_up, w_down, b_down, ple_proj, ple_norm_w, ple_gate):
    depth = w_in.shape[0]
    groups = ((x_prompt.shape[0] * x_prompt.shape[1], x_prompt.shape[1]),
              (x_sample.shape[0] * x_sample.shape[1], x_sample.shape[1]))
    d = x_prompt.shape[-1]
    ha, hb = x_prompt.reshape(-1, d), x_sample.reshape(-1, d)
    lb_all = jnp.cumsum(jax.nn.softmax(hgrn_lb.astype(F32), axis=0), axis=0)
    for i in range(depth):
        ha, hb = _layer(ha, hb, p_prompt[i].reshape(-1, PLE_DIM), p_sample[i].reshape(-1, PLE_DIM), groups,
                        lb_all[i], mix_norm_w[i], w_in[i], hgrn_norm_w[i], q_norm_w[i], k_norm_w[i],
                        w_out[i], ffn_norm_w[i], router_w[i], router_b[i], w_gate_up[i], b_gate_up[i],
                        w_down[i], b_down[i], ple_proj[i], ple_norm_w[i], ple_gate[i])
    return ha.reshape(x_prompt.shape), hb.reshape(x_sample.shape)
```

```python
import functools

import numpy as np
import jax
import jax.numpy as jnp
from jax import lax
from jax.experimental import pallas as pl
from jax.experimental.pallas import tpu as pltpu

F32 = jnp.float32
BF16 = jnp.bfloat16

D_MODEL = 2048
GRID_W = 64
HG_WIDTH = 1024
HG_HEAD_DIM = 128
HG_HEADS = HG_WIDTH // HG_HEAD_DIM
CHUNK = 64
HEAD_DIM = 128
N_Q_HEADS = 8
N_KV_HEADS = 2
Q_PER_KV = N_Q_HEADS // N_KV_HEADS
ROPE_THETA = 10000.0
ROPE_AXIS_PAIRS = HEAD_DIM // 4
N_EXPERTS = 32
TOP_K = 4
D_FF = D_MODEL
SWIGLU_LIMIT = 7.0
SWIGLU_ALPHA = 1.702
PLE_DIM = 256
NORM_EPS = 1e-6
Q_COLS = N_Q_HEADS * HEAD_DIM
KV_COLS = N_KV_HEADS * HEAD_DIM
IN_COLS = 5 * HG_WIDTH + Q_COLS + 2 * KV_COLS

LANES = 128
V7X_VMEM_BUDGET_BYTES = 56 * 1024 * 1024

IN_TM, IN_TN = 1024, 512
QK_TM = 512
ATT_TQ = 256
HG_TB = 256
OUT_TM = 256
MOE_R = 1024
MOE_SUB = 256
MOE_TF = 256
FIN_TM = 256
ROW_CHUNK = 128
N_LEVELS = 6


def _params(sem, vmem_bytes):
    return pltpu.CompilerParams(dimension_semantics=sem,
                                vmem_limit_bytes=min(int(vmem_bytes), V7X_VMEM_BUDGET_BYTES))


def _rms_scale(x):
    return lax.rsqrt(jnp.mean(x * x, axis=-1, keepdims=True) + NORM_EPS)


def _dot_nt(a, b):
    return lax.dot_general(a, b, (((1,), (1,)), ((), ())), preferred_element_type=F32)


def _pair_specs(block, n_first):
    zeros = (0,) * (len(block) - 1)
    return [pl.BlockSpec(block, lambda i, *_: (jnp.minimum(i, n_first - 1),) + zeros),
            pl.BlockSpec(block, lambda i, *_: (jnp.maximum(i - n_first, 0),) + zeros)]


def _in_proj_kernel(xa_ref, xb_ref, nw_ref, w_ref, o_ref, xn_ref, *, n_first):
    @pl.when(pl.program_id(1) == 0)
    def _():
        first = pl.program_id(0) < n_first

        def body(c, carry):
            rows = pl.ds(pl.multiple_of(c * ROW_CHUNK, ROW_CHUNK), ROW_CHUNK)
            x = jnp.where(first, xa_ref[rows, :], xb_ref[rows, :])
            xn_ref[rows, :] = (x * _rms_scale(x) * nw_ref[...]).astype(BF16)
            return carry
        lax.fori_loop(0, xa_ref.shape[0] // ROW_CHUNK, body, 0)

    o_ref[...] = jnp.dot(xn_ref[...], w_ref[...], preferred_element_type=F32)


def _in_proj(xa, xb, norm_w, w_bf16):
    d = xa.shape[1]
    n = xa.shape[0] + xb.shape[0]
    cols = w_bf16.shape[1]
    tm, tn = IN_TM, IN_TN
    vmem = 4 * tm * d * 4 + tm * d * 2 + 2 * d * tn * 2 + 2 * tm * tn * 4 + (8 << 20)
    return pl.pallas_call(
        functools.partial(_in_proj_kernel, n_first=xa.shape[0] // tm),
        grid=(n // tm, cols // tn),
        in_specs=_pair_specs((tm, d), xa.shape[0] // tm) + [
            pl.BlockSpec((1, d), lambda i, j: (0, 0)),
            pl.BlockSpec((d, tn), lambda i, j: (0, j))],
        out_specs=pl.BlockSpec((tm, tn), lambda i, j: (i, j)),
        out_shape=jax.ShapeDtypeStruct((n, cols), F32),
        scratch_shapes=[pltpu.VMEM((tm, d), BF16)],
        compiler_params=_params(("parallel", "arbitrary"), vmem),
        name="in_proj",
    )(xa, xb, norm_w.reshape(1, d), w_bf16)


def _rope_tables(t_max):
    rows = t_max // GRID_W
    row = jnp.repeat(jnp.arange(rows, dtype=F32), GRID_W)
    col = jnp.tile(jnp.arange(GRID_W, dtype=F32), rows)
    inv_freq = ROPE_THETA ** (-jnp.arange(ROPE_AXIS_PAIRS, dtype=F32) / ROPE_AXIS_PAIRS)
    ang = jnp.concatenate([row[:, None] * inv_freq, col[:, None] * inv_freq], axis=-1)
    cos = jnp.repeat(jnp.cos(ang), 2, axis=-1)
    sin = jnp.repeat(jnp.sin(ang), 2, axis=-1)
    even = (jnp.arange(HEAD_DIM) % 2 == 0)[None, :]
    s_next = jnp.where(even, -sin, 0.0)
    s_prev = jnp.where(even, 0.0, sin)
    return cos, s_next, s_prev


def _qk_rope_kernel(q_ref, k_ref, v_ref, c_ref, sn_ref, sp_ref, qw_ref, kw_ref,
                    qo_ref, ko_ref, vo_ref):
    c, sn, sp = c_ref[...], sn_ref[...], sp_ref[...]

    def norm_rope(x, w):
        y = x * _rms_scale(x) * w
        return y * c + pltpu.roll(y, HEAD_DIM - 1, 1) * sn + pltpu.roll(y, 1, 1) * sp

    scale = HEAD_DIM ** -0.5
    for h in range(N_Q_HEADS):
        sl = slice(h * HEAD_DIM, (h + 1) * HEAD_DIM)
        qo_ref[:, sl] = (norm_rope(q_ref[:, sl], qw_ref[...]) * scale).astype(BF16)
    for h in range(N_KV_HEADS):
        sl = slice(h * HEAD_DIM, (h + 1) * HEAD_DIM)
        ko_ref[:, sl] = norm_rope(k_ref[:, sl], kw_ref[...]).astype(BF16)
    vo_ref[...] = v_ref[...].astype(BF16)


def _qk_rope(proj, q_norm_w, k_norm_w, groups):
    n = proj.shape[0]
    tm = QK_TM
    t_max = max(t for _, t in groups)
    cos, s_next, s_prev = _rope_tables(t_max)

    def pos_block(i):
        blk = jnp.int32(0)
        start = 0
        for n_tok, t in groups:
            first = start // tm
            blk = jnp.where(i >= first, (i - first) % (t // tm), blk)
            start += n_tok
        return blk

    tab_spec = pl.BlockSpec((tm, HEAD_DIM), lambda i: (pos_block(i), 0))
    q_blk, k_blk, v_blk = (5 * HG_WIDTH) // Q_COLS, (5 * HG_WIDTH + Q_COLS) // KV_COLS, \
        (5 * HG_WIDTH + Q_COLS + KV_COLS) // KV_COLS
    return pl.pallas_call(
        _qk_rope_kernel,
        grid=(n // tm,),
        in_specs=[pl.BlockSpec((tm, Q_COLS), lambda i: (i, q_blk)),
                  pl.BlockSpec((tm, KV_COLS), lambda i: (i, k_blk)),
                  pl.BlockSpec((tm, KV_COLS), lambda i: (i, v_blk)),
                  tab_spec, tab_spec, tab_spec,
                  pl.BlockSpec((1, HEAD_DIM), lambda i: (0, 0)),
                  pl.BlockSpec((1, HEAD_DIM), lambda i: (0, 0))],
        out_specs=[pl.BlockSpec((tm, Q_COLS), lambda i: (i, 0)),
                   pl.BlockSpec((tm, KV_COLS), lambda i: (i, 0)),
                   pl.BlockSpec((tm, KV_COLS), lambda i: (i, 0))],
        out_shape=[jax.ShapeDtypeStruct((n, Q_COLS), BF16),
                   jax.ShapeDtypeStruct((n, KV_COLS), BF16),
                   jax.ShapeDtypeStruct((n, KV_COLS), BF16)],
        compiler_params=_params(("parallel",), 32 << 20),
        name="qk_rope",
    )(proj, proj, proj, cos, s_next, s_prev,
      q_norm_w.reshape(1, HEAD_DIM), k_norm_w.reshape(1, HEAD_DIM))


def _attn_kernel(q_ref, k_ref, v_ref, o_ref):
    k = k_ref[...]
    v = v_ref[...]
    for h in range(Q_PER_KV):
        sl = slice(h * HEAD_DIM, (h + 1) * HEAD_DIM)
        s = _dot_nt(q_ref[:, sl], k)
        p = jnp.exp(s - jnp.max(s, axis=-1, keepdims=True))
        l = jnp.sum(p, axis=-1, keepdims=True)
        o = jnp.dot(p.astype(BF16), v, preferred_element_type=F32)
        o_ref[:, sl] = (o / l).astype(BF16)


def _attention(q, k, v, tok0, batch, t):
    tq = ATT_TQ
    nq = t // tq
    row0, seq0 = tok0 // tq, tok0 // t
    gcols = Q_PER_KV * HEAD_DIM
    vmem = 4 * t * HEAD_DIM * 2 + 3 * tq * t * 4 + (8 << 20)
    q_spec = pl.BlockSpec((tq, gcols), lambda b, g, i: (row0 + b * nq + i, g))
    kv_spec = pl.BlockSpec((t, HEAD_DIM), lambda b, g, i: (seq0 + b, g))
    return pl.pallas_call(
        _attn_kernel,
        grid=(batch, N_KV_HEADS, nq),
        in_specs=[q_spec, kv_spec, kv_spec],
        out_specs=q_spec,
        out_shape=jax.ShapeDtypeStruct(q.shape, BF16),
        input_output_aliases={0: 0},
        compiler_params=_params(("parallel", "parallel", "arbitrary"), vmem),
        name="attention",
    )(q, k, v)


def _hgrn_constants():
    c = CHUNK
    t = np.arange(c)[:, None]
    u = np.arange(c)[None, :]
    sums = [(u <= t), (u > t)]
    masks = []
    for lvl in range(N_LEVELS):
        half = (c // 2) >> lvl
        bound = (t // (2 * half)) * (2 * half) + half - 1
        late = t > bound
        sums.append(np.where(late, (u > bound) & (u <= t), (u > t) & (u <= bound)))
        same = (t // (2 * half)) == (u // (2 * half))
        masks.append(same & late & (u <= bound))
    masks.append(t == u)
    sums = np.stack(sums).astype(np.float32)
    masks = np.stack(masks).astype(np.float32)
    both_s = np.stack([sums, sums[:, ::-1, ::-1]]).reshape(2, -1, c)
    both_m = np.stack([masks, masks[:, ::-1, ::-1]]).reshape(2, -1, c)
    return jnp.asarray(both_s, BF16), jnp.asarray(both_m, F32)


def _hgrn_kernel(q_ref, f_ref, v_ref, lb_ref, sums_ref, masks_ref, o_ref, st_ref, *, nchunk, nblk,
                 seq_first_blk, seq_last_blk):
    d = pl.program_id(0)
    i = pl.program_id(1)
    blk = i + d * (nblk - 1 - 2 * i)
    fresh = functools.reduce(
        jnp.logical_or,
        [jnp.logical_or(jnp.logical_and(d == 0, blk == f), jnp.logical_and(d == 1, blk == l))
         for f, l in zip(seq_first_blk, seq_last_blk)])

    @pl.when(fresh)
    def _():
        st_ref[...] = jnp.zeros_like(st_ref)

    lb = lb_ref[0]
    sums = sums_ref[0]
    c = CHUNK

    def chunk_body(j, carry):
        jj = j + d * (nchunk - 1 - 2 * j)
        rows = pl.ds(pl.multiple_of(jj * c, c), c)
        f = lb + (1.0 - lb) * jax.nn.sigmoid(f_ref[rows, :])
        k = 1.0 - f
        lf = jnp.log(f)
        qpre = q_ref[rows, :]
        q = qpre * jax.nn.sigmoid(qpre)
        vb = v_ref[rows, :].astype(BF16)
        lf_hi = lf.astype(BF16)
        lf_lo = (lf - lf_hi.astype(F32)).astype(BF16)
        e = jnp.exp(jnp.dot(sums, lf_hi, preferred_element_type=F32)
                    + jnp.dot(sums, lf_lo, preferred_element_type=F32))
        g = jnp.exp(jnp.sum(lf, axis=0, keepdims=True))
        qt = (q * e[0:c]).astype(BF16)
        kd = (k * e[c:2 * c]).astype(BF16)
        ql = [q.astype(BF16)] + [(q * e[(2 + lvl) * c:(3 + lvl) * c]).astype(BF16) for lvl in range(N_LEVELS)]
        kl = [k.astype(BF16)] + [(k * e[(2 + lvl) * c:(3 + lvl) * c]).astype(BF16) for lvl in range(N_LEVELS)]
        mk = [masks_ref[0, N_LEVELS * c:(N_LEVELS + 1) * c, :]] + [
            masks_ref[0, lvl * c:(lvl + 1) * c, :] for lvl in range(N_LEVELS)]
        heads = [slice(h * HG_HEAD_DIM, (h + 1) * HG_HEAD_DIM) for h in range(HG_HEADS)]
        scores = []
        for sl in heads:
            a = mk[0] * _dot_nt(ql[0][:, sl], kl[0][:, sl])
            for lvl in range(1, N_LEVELS + 1):
                a = a + mk[lvl] * _dot_nt(ql[lvl][:, sl], kl[lvl][:, sl])
            scores.append(a.astype(BF16))
        for h, sl in enumerate(heads):
            st = st_ref[h]
            o = _dot_nt(qt[:, sl], st.astype(BF16)) + jnp.dot(scores[h], vb[:, sl], preferred_element_type=F32)
            ut = lax.dot_general(vb[:, sl], kd[:, sl], (((0,), (0,)), ((), ())), preferred_element_type=F32)
            st_ref[h] = st * g[:, sl] + ut
            o_ref[0, rows, sl] = o
        return carry

    lax.fori_loop(0, nchunk, chunk_body, 0)


def _hgrn(proj, lb, sums, masks, groups):
    n = proj.shape[0]
    tb = HG_TB
    nblk = n // tb
    first_blk, last_blk, tok = [], [], 0
    for n_tok, t in groups:
        for s in range(n_tok // t):
            first_blk.append((tok + s * t) // tb)
            last_blk.append((tok + (s + 1) * t) // tb - 1)
        tok += n_tok

    def rows(d, i):
        return i + d * (nblk - 1 - 2 * i)

    in_blk = lambda col: pl.BlockSpec((tb, HG_WIDTH), lambda d, i: (rows(d, i), col(d)))
    return pl.pallas_call(
        functools.partial(_hgrn_kernel, nchunk=tb // CHUNK, nblk=nblk,
                          seq_first_blk=tuple(first_blk), seq_last_blk=tuple(last_blk)),
        grid=(2, nblk),
        in_specs=[in_blk(lambda d: 0), in_blk(lambda d: 1 + d), in_blk(lambda d: 3),
                  pl.BlockSpec((1, 1, HG_WIDTH), lambda d, i: (d, 0, 0)),
                  pl.BlockSpec((1,) + sums.shape[1:], lambda d, i: (d, 0, 0)),
                  pl.BlockSpec((1,) + masks.shape[1:], lambda d, i: (d, 0, 0))],
        out_specs=pl.BlockSpec((1, tb, HG_WIDTH), lambda d, i: (d, rows(d, i), 0)),
        out_shape=jax.ShapeDtypeStruct((2, n, HG_WIDTH), F32),
        scratch_shapes=[pltpu.VMEM((HG_HEADS, HG_HEAD_DIM, HG_HEAD_DIM), F32)],
        compiler_params=_params(("parallel", "arbitrary"), 40 << 20),
        name="hgrn2",
    )(proj, proj, proj, lb.reshape(2, 1, HG_WIDTH), sums, masks)


def _out_router_kernel(xa_ref, xb_ref, of_ref, ob_ref, g_ref, att_ref, hw_ref, wo_ref, fw_ref, rw_ref, rb_ref,
                       h_ref, xn_ref, eidx_ref, gate_ref, cnt_ref, cnt_acc, *, n_first):
    @pl.when(pl.program_id(0) == 0)
    def _():
        cnt_acc[...] = jnp.zeros_like(cnt_acc)

    o = of_ref[0] + ob_ref[0]
    parts = []
    for h in range(HG_HEADS):
        sl = slice(h * HG_HEAD_DIM, (h + 1) * HG_HEAD_DIM)
        oh = o[:, sl]
        gh = g_ref[:, sl]
        parts.append(((oh * _rms_scale(oh) * hw_ref[...]) * (gh * jax.nn.sigmoid(gh))).astype(BF16))
    oa = jnp.concatenate(parts, axis=1)
    h1 = (jnp.where(pl.program_id(0) < n_first, xa_ref[...], xb_ref[...])
          + jnp.dot(oa, wo_ref[0:HG_WIDTH, :], preferred_element_type=F32)
          + jnp.dot(att_ref[...], wo_ref[HG_WIDTH:, :], preferred_element_type=F32))
    h_ref[...] = h1
    xn = h1 * _rms_scale(h1) * fw_ref[...]
    for cb in range(xn_ref.shape[1]):
        xn_ref[:, cb, :] = xn[:, cb * LANES:(cb + 1) * LANES]

    logits = jnp.dot(xn, rw_ref[...], precision=lax.Precision.HIGHEST,
                     preferred_element_type=F32) + rb_ref[...]
    tm = logits.shape[0]
    lane = lax.broadcasted_iota(jnp.int32, (tm, LANES), 1)
    work = logits
    vals, idxs = [], []
    for _ in range(TOP_K):
        m = jnp.max(work, axis=-1, keepdims=True)
        idx = jnp.min(jnp.where(work == m, lane, LANES), axis=-1, keepdims=True)
        vals.append(m)
        idxs.append(idx)
        work = jnp.where(lane == idx, -jnp.inf, work)
    exps = [jnp.exp(v - vals[0]) for v in vals]
    denom = exps[0] + exps[1] + exps[2] + exps[3]

    onehot = jnp.zeros((tm, LANES), F32)
    for idx in idxs:
        onehot = onehot + (lane == idx).astype(F32)
    eidx = jnp.zeros((tm, LANES), jnp.int32)
    gate = jnp.zeros((tm, LANES), F32)
    for kk in range(TOP_K):
        eidx = jnp.where(lane == kk, idxs[kk], eidx)
        gate = jnp.where(lane == kk, exps[kk] / denom, gate)
    eidx_ref[...] = eidx
    gate_ref[...] = gate
    cnt_acc[...] = cnt_acc[...] + jnp.sum(onehot, axis=0, keepdims=True)
    cnt_ref[...] = cnt_acc[...]


def _out_router(xa, xb, o_dirs, proj, att, hgrn_norm_w, w_out_bf16, ffn_norm_w, router_w, router_b):
    d = xa.shape[1]
    n = xa.shape[0] + xb.shape[0]
    tm = OUT_TM
    rw = jnp.zeros((d, LANES), F32).at[:, :N_EXPERTS].set(router_w.astype(F32))
    rb = jnp.full((1, LANES), -1e30, F32).at[0, :N_EXPERTS].set(router_b.astype(F32))
    g_blk = 4
    row = lambda i: (i, 0)
    const = lambda i: (0, 0)
    vmem = 2 * d * d * 2 + 12 * tm * d * 4 + (12 << 20)
    return pl.pallas_call(
        functools.partial(_out_router_kernel, n_first=xa.shape[0] // tm),
        grid=(n // tm,),
        in_specs=_pair_specs((tm, d), xa.shape[0] // tm) + [
                  pl.BlockSpec((1, tm, HG_WIDTH), lambda i: (0, i, 0)),
                  pl.BlockSpec((1, tm, HG_WIDTH), lambda i: (1, i, 0)),
                  pl.BlockSpec((tm, HG_WIDTH), lambda i: (i, g_blk)),
                  pl.BlockSpec((tm, Q_COLS), row),
                  pl.BlockSpec((1, HG_HEAD_DIM), const),
                  pl.BlockSpec((d, d), const),
                  pl.BlockSpec((1, d), const),
                  pl.BlockSpec((d, LANES), const),
                  pl.BlockSpec((1, LANES), const)],
        out_specs=[pl.BlockSpec((tm, d), row), pl.BlockSpec((tm, d // LANES, LANES), lambda i: (i, 0, 0)),
                   pl.BlockSpec((tm, LANES), row), pl.BlockSpec((tm, LANES), row),
                   pl.BlockSpec((1, LANES), const)],
        out_shape=[jax.ShapeDtypeStruct((n, d), F32), jax.ShapeDtypeStruct((n, d // LANES, LANES), F32),
                   jax.ShapeDtypeStruct((n, LANES), jnp.int32),
                   jax.ShapeDtypeStruct((n, LANES), F32), jax.ShapeDtypeStruct((1, LANES), F32)],
        scratch_shapes=[pltpu.VMEM((1, LANES), F32)],
        compiler_params=_params(("arbitrary",), vmem),
        name="out_router",
    )(xa, xb, o_dirs, o_dirs, proj, att, hgrn_norm_w.reshape(1, HG_HEAD_DIM), w_out_bf16,
      ffn_norm_w.reshape(1, d), rw, rb)


ROW_PITCH = 20
GATHER_UNROLL = 8


def _start_row_gather(idx_ref, n_rows, src_hbm, buf, sem):
    slabs = src_hbm.shape[1]

    def issue(r, carry):
        pltpu.make_async_copy(src_hbm.at[idx_ref[0, 0, r]], buf.at[pl.ds(r * ROW_PITCH, slabs), :], sem).start()
        return carry
    lax.fori_loop(0, n_rows, issue, 0, unroll=GATHER_UNROLL)


def _wait_row_gather(n_rows, slabs, buf, sem):
    done = buf.at[pl.ds(0, n_rows * slabs), :]
    pltpu.make_async_copy(done, done, sem).wait()


def _gathered_cols(buf, row0, n_rows, cb):
    return buf[pl.ds(row0 * ROW_PITCH + cb, n_rows, stride=ROW_PITCH), :]


def _start_row_scatter(idx_ref, src, dst_hbm, sem):
    def issue(r, carry):
        pltpu.make_async_copy(src.at[pl.ds(r, 1), :], dst_hbm.at[pl.ds(idx_ref[0, 0, r], 1), :], sem).start()
        return carry
    lax.fori_loop(0, src.shape[0], issue, 0, unroll=GATHER_UNROLL)


def _wait_row_scatter(src, sem):
    pltpu.make_async_copy(src, src, sem).wait()


def _moe_kernel(vb_ref, ve_ref, vlo_ref, vhi_ref, nvis_ref,
                tok_ref, tok_next_ref, orow_ref, xn_hbm, wg_ref, wl_ref, bg_ref, bl_ref, wd_ref, bd_ref,
                y_hbm, raw, xb, acc, wgb, wlb, wdb, sem_in, sem_out, *, nblk):
    w = pl.program_id(0)
    j = pl.program_id(1)
    nf = pl.num_programs(1)
    r = xb.shape[0]
    slabs = xn_hbm.shape[1]
    valid = w < nvis_ref[0]
    blk, lo, hi = vb_ref[w], vlo_ref[w], vhi_ref[w]
    slot = lax.rem(blk, 2)
    acc_blk = acc.at[slot]

    @pl.when(jnp.logical_and(valid, jnp.logical_and(lo == 0, j == 0)))
    def _():
        @pl.when(w == 0)
        def _():
            _start_row_gather(tok_ref, r, xn_hbm, raw, sem_in)
        _wait_row_gather(r, slabs, raw, sem_in)
        for cb in range(slabs):
            xb[:, cb * LANES:(cb + 1) * LANES] = _gathered_cols(raw, 0, r, cb).astype(BF16)

        @pl.when(blk + 1 < nblk)
        def _():
            _start_row_gather(tok_next_ref, r, xn_hbm, raw, sem_in)

        @pl.when(blk >= 2)
        def _():
            _wait_row_scatter(acc_blk, sem_out.at[slot])

    @pl.when(valid)
    def _():
        wgb[...] = wg_ref[0].astype(BF16)
        wlb[...] = wl_ref[0].astype(BF16)
        wdb[...] = wd_ref[0].astype(BF16)
        for s in range(r // MOE_SUB):
            r0 = s * MOE_SUB
            rows = slice(r0, r0 + MOE_SUB)

            @pl.when(jnp.logical_and(lo < r0 + MOE_SUB, hi > r0))
            def _():
                xs = xb[rows, :]
                gl = jnp.dot(xs, wgb[...], preferred_element_type=F32) + bg_ref[0]
                lin = jnp.dot(xs, wlb[...], preferred_element_type=F32) + bl_ref[0]
                gl = jnp.minimum(gl, SWIGLU_LIMIT)
                lin = jnp.clip(lin, -SWIGLU_LIMIT, SWIGLU_LIMIT)
                act = (lin + 1.0) * (gl * jax.nn.sigmoid(SWIGLU_ALPHA * gl))
                rid = r0 + lax.broadcasted_iota(jnp.int32, (MOE_SUB, 1), 0)
                mine = jnp.logical_and(rid >= lo, rid < hi).astype(F32)
                part = jnp.dot((act * mine).astype(BF16), wdb[...], preferred_element_type=F32)
                starts_here = lo <= r0

                @pl.when(jnp.logical_and(j == 0, starts_here))
                def _():
                    acc_blk[rows, :] = part + mine * bd_ref[0]

                @pl.when(jnp.logical_and(j == 0, jnp.logical_not(starts_here)))
                def _():
                    acc_blk[rows, :] += part + mine * bd_ref[0]

                @pl.when(j > 0)
                def _():
                    acc_blk[rows, :] += part

        @pl.when(jnp.logical_and(hi == r, j == nf - 1))
        def _():
            _start_row_scatter(orow_ref, acc_blk, y_hbm, sem_out.at[slot])

    @pl.when(jnp.logical_and(w == pl.num_programs(0) - 1, j == nf - 1))
    def _():
        for s in range(min(2, nblk)):
            _wait_row_scatter(acc.at[s], sem_out.at[s])


def _moe_experts(vis_blk, vis_exp, vis_lo, vis_hi, n_vis, slot_tok, out_row, xn3,
                 w_gate_up, b_gate_up, w_down, b_down):
    r, tf = MOE_R, MOE_TF
    n, slabs, _ = xn3.shape
    d = slabs * LANES
    nblk = slot_tok.shape[0]
    nf = D_FF // tf

    def tile(w, j, nv):
        return jnp.where(w < nv[0], j, nf - 1)

    idx_spec = lambda f: pl.BlockSpec((1, 1, r), lambda w, j, vb, ve, lo, hi, nv: (f(vb[w]), 0, 0),
                                      memory_space=pltpu.SMEM)
    vmem = (r * ROW_PITCH * LANES * 4 + r * d * 2 + 2 * r * d * 4 + 2 * 3 * d * tf * 4 + 3 * d * tf * 2
            + 3 * MOE_SUB * d * 4 + (6 << 20))
    return pl.pallas_call(
        functools.partial(_moe_kernel, nblk=nblk),
        grid_spec=pltpu.PrefetchScalarGridSpec(
            num_scalar_prefetch=5,
            grid=(vis_blk.shape[0], nf),
            in_specs=[idx_spec(lambda b: b), idx_spec(lambda b: jnp.minimum(b + 1, nblk - 1)), idx_spec(lambda b: b),
                      pl.BlockSpec(memory_space=pl.ANY),
                      pl.BlockSpec((1, d, tf), lambda w, j, vb, ve, lo, hi, nv: (ve[w], 0, tile(w, j, nv))),
                      pl.BlockSpec((1, d, tf), lambda w, j, vb, ve, lo, hi, nv: (ve[w], 0, nf + tile(w, j, nv))),
                      pl.BlockSpec((1, 1, tf), lambda w, j, vb, ve, lo, hi, nv: (ve[w], 0, tile(w, j, nv))),
                      pl.BlockSpec((1, 1, tf), lambda w, j, vb, ve, lo, hi, nv: (ve[w], 0, nf + tile(w, j, nv))),
                      pl.BlockSpec((1, tf, d), lambda w, j, vb, ve, lo, hi, nv: (ve[w], tile(w, j, nv), 0)),
                      pl.BlockSpec((1, 1, d), lambda w, j, vb, ve, lo, hi, nv: (ve[w], 0, 0))],
            out_specs=pl.BlockSpec(memory_space=pl.ANY),
            scratch_shapes=[pltpu.VMEM((r * ROW_PITCH, LANES), F32),
                            pltpu.VMEM((r, d), BF16),
                            pltpu.VMEM((2, r, d), F32),
                            pltpu.VMEM((d, tf), BF16), pltpu.VMEM((d, tf), BF16), pltpu.VMEM((tf, d), BF16),
                            pltpu.SemaphoreType.DMA, pltpu.SemaphoreType.DMA((2,))]),
        out_shape=jax.ShapeDtypeStruct((TOP_K * n, d), F32),
        compiler_params=_params(("arbitrary", "arbitrary"), vmem),
        name="moe_experts",
    )(vis_blk, vis_exp, vis_lo, vis_hi, n_vis, slot_tok, slot_tok, out_row, xn3,
      w_gate_up, w_gate_up, b_gate_up.reshape(N_EXPERTS, 1, 2 * D_FF),
      b_gate_up.reshape(N_EXPERTS, 1, 2 * D_FF), w_down, b_down.reshape(N_EXPERTS, 1, d))


def _final_kernel(h_ref, gate_ref, y0_ref, y1_ref, y2_ref, y3_ref, pa_ref, pb_ref, pp_ref, pnw_ref, pg_ref,
                  oa_ref, ob_ref, *, n_first):
    first = pl.program_id(0) < n_first
    h = h_ref[...]
    for kk, y_ref in enumerate((y0_ref, y1_ref, y2_ref, y3_ref)):
        h = h + gate_ref[:, kk:kk + 1] * y_ref[...]
    gate = jax.nn.sigmoid(jnp.dot(h.astype(BF16), pg_ref[...], preferred_element_type=F32))
    p = jnp.where(first, pa_ref[...], pb_ref[...])
    pe = jnp.dot(p.astype(BF16), pp_ref[...], preferred_element_type=F32)
    pe = pe * _rms_scale(pe) * pnw_ref[...]
    out = h + gate * pe

    @pl.when(first)
    def _():
        oa_ref[...] = out

    @pl.when(jnp.logical_not(first))
    def _():
        ob_ref[...] = out


def _final(y4, h1, gates, pa, pb, ple_proj_bf16, ple_norm_w, ple_gate_bf16):
    n, d = h1.shape
    tm = FIN_TM
    nt = n // tm
    n_first = pa.shape[0] // tm
    row = lambda i: (i, 0)
    const = lambda i: (0, 0)
    vmem = 2 * d * d * 2 + 20 * tm * d * 4 + (8 << 20)
    return pl.pallas_call(
        functools.partial(_final_kernel, n_first=n_first),
        grid=(nt,),
        in_specs=[pl.BlockSpec((tm, d), row),
                  pl.BlockSpec((tm, LANES), row)]
        + [pl.BlockSpec((tm, d), lambda i, kk=kk: (kk * nt + i, 0)) for kk in range(TOP_K)]
        + _pair_specs((tm, PLE_DIM), n_first)
        + [pl.BlockSpec((PLE_DIM, d), const),
           pl.BlockSpec((1, d), const),
           pl.BlockSpec((d, d), const)],
        out_specs=_pair_specs((tm, d), n_first),
        out_shape=[jax.ShapeDtypeStruct((pa.shape[0], d), F32), jax.ShapeDtypeStruct((pb.shape[0], d), F32)],
        compiler_params=_params(("arbitrary",), vmem),
        name="combine_final",
    )(h1, gates, y4, y4, y4, y4, pa, pb, ple_proj_bf16, ple_norm_w.reshape(1, d), ple_gate_bf16)


def _routing(eidx, counts, n_tokens):
    r = MOE_R
    nblk = (n_tokens * TOP_K) // r
    nvis_max = nblk + N_EXPERTS - 1
    order = jnp.argsort(eidx[:, :TOP_K].reshape(-1), stable=True).astype(jnp.int32)
    slot_tok = order // TOP_K
    out_row = (order % TOP_K) * n_tokens + slot_tok
    cnt = counts[0, :N_EXPERTS].astype(jnp.int32)
    ends = jnp.cumsum(cnt)
    starts = ends - cnt
    first_blk = starts // r
    n_blk = jnp.where(cnt > 0, (ends - 1) // r - first_blk + 1, 0)
    vis_end = jnp.cumsum(n_blk)
    n_vis = vis_end[-1]
    w = jnp.arange(nvis_max, dtype=jnp.int32)
    valid = w < n_vis
    e = jnp.minimum(jnp.sum((vis_end[None, :] <= w[:, None]).astype(jnp.int32), axis=1), N_EXPERTS - 1)
    e = jnp.where(valid, e, jnp.max(jnp.where(valid, e, 0)))
    blk = jnp.where(valid, first_blk[e] + w - (vis_end[e] - n_blk[e]), nblk - 1)
    lo = jnp.where(valid, jnp.clip(starts[e] - blk * r, 0, r), 0)
    hi = jnp.where(valid, jnp.clip(ends[e] - blk * r, 0, r), 0)
    i32 = lambda a: a.astype(jnp.int32)
    return (i32(blk), i32(e), i32(lo), i32(hi), i32(n_vis).reshape(1),
            slot_tok.reshape(nblk, 1, r), out_row.reshape(nblk, 1, r))


def _layer(xa, xb, pa, pb, groups, lb, mix_norm_w, w_in, hgrn_norm_w, q_norm_w, k_norm_w, w_out, ffn_norm_w,
           router_w, router_b, w_gate_up, b_gate_up, w_down, b_down, ple_proj, ple_norm_w, ple_gate):
    n = xa.shape[0] + xb.shape[0]
    proj = _in_proj(xa, xb, mix_norm_w, w_in.astype(BF16))
    att, k, v = _qk_rope(proj, q_norm_w, k_norm_w, groups)
    sums, masks = _hgrn_constants()
    o_dirs = _hgrn(proj, lb, sums, masks, groups)
    tok0 = 0
    for n_tok, t in groups:
        att = _attention(att, k, v, tok0, n_tok // t, t)
        tok0 += n_tok
    h1, xn3, eidx, gates, counts = _out_router(
        xa, xb, o_dirs, proj, att, hgrn_norm_w, w_out.astype(BF16), ffn_norm_w, router_w, router_b)
    y4 = _moe_experts(*_routing(eidx, counts, n), xn3, w_gate_up, b_gate_up, w_down, b_down)
    return _final(y4, h1, gates, pa, pb, ple_proj.astype(BF16), ple_norm_w, ple_gate.astype(BF16))


def kernel(x_prompt, x_sample, p_prompt, p_sample, mix_norm_w, w_in, hgrn_lb, hgrn_norm_w, q_norm_w, k_norm_w, w_out, ffn_norm_w, router_w, router_b, w_gate_up, b_gate_up, w_down, b_down, ple_proj, ple_norm_w, ple_gate):
    depth = w_in.shape[0]
    groups = ((x_prompt.shape[0] * x_prompt.shape[1], x_prompt.shape[1]),
              (x_sample.shape[0] * x_sample.shape[1], x_sample.shape[1]))
    d = x_prompt.shape[-1]
    ha, hb = x_prompt.reshape(-1, d), x_sample.reshape(-1, d)
    lb_all = jnp.cumsum(jax.nn.softmax(hgrn_lb.astype(F32), axis=0), axis=0)
    for i in range(depth):
        ha, hb = _layer(ha, hb, p_prompt[i].reshape(-1, PLE_DIM), p_sample[i].reshape(-1, PLE_DIM), groups,
                        lb_all[i], mix_norm_w[i], w_in[i], hgrn_norm_w[i], q_norm_w[i], k_norm_w[i],
                        w_out[i], ffn_norm_w[i], router_w[i], router_b[i], w_gate_up[i], b_gate_up[i],
                        w_down[i], b_down[i], ple_proj[i], ple_norm_w[i], ple_gate[i])
    return ha.reshape(x_prompt.shape), hb.reshape(x_sample.shape)
```

```python
import functools

import numpy as np
import jax
import jax.numpy as jnp
from jax import lax
from jax.experimental import pallas as pl
from jax.experimental.pallas import tpu as pltpu

F32 = jnp.float32
BF16 = jnp.bfloat16

D_MODEL = 2048
GRID_W = 64
HG_WIDTH = 1024
HG_HEAD_DIM = 128
HG_HEADS = HG_WIDTH // HG_HEAD_DIM
CHUNK = 64
HEAD_DIM = 128
N_Q_HEADS = 8
N_KV_HEADS = 2
Q_PER_KV = N_Q_HEADS // N_KV_HEADS
ROPE_THETA = 10000.0
ROPE_AXIS_PAIRS = HEAD_DIM // 4
N_EXPERTS = 32
TOP_K = 4
D_FF = D_MODEL
SWIGLU_LIMIT = 7.0
SWIGLU_ALPHA = 1.702
PLE_DIM = 256
NORM_EPS = 1e-6
Q_COLS = N_Q_HEADS * HEAD_DIM
KV_COLS = N_KV_HEADS * HEAD_DIM
IN_COLS = 5 * HG_WIDTH + Q_COLS + 2 * KV_COLS

LANES = 128
V7X_VMEM_BUDGET_BYTES = 56 * 1024 * 1024

IN_TM, IN_TN = 1024, 512
QK_TM = 512
ATT_TQ = 256
HG_TB = 256
OUT_TM = 256
MOE_R = 1024
MOE_SUB = 256
MOE_TF = 256
FIN_TM = 256
ROW_CHUNK = 128
N_LEVELS = 6


def _params(sem, vmem_bytes):
    return pltpu.CompilerParams(dimension_semantics=sem,
                                vmem_limit_bytes=min(int(vmem_bytes), V7X_VMEM_BUDGET_BYTES))


def _rms_scale(x):
    return lax.rsqrt(jnp.mean(x * x, axis=-1, keepdims=True) + NORM_EPS)


def _dot_nt(a, b):
    return lax.dot_general(a, b, (((1,), (1,)), ((), ())), preferred_element_type=F32)


def _pair_specs(block, n_first):
    zeros = (0,) * (len(block) - 1)
    return [pl.BlockSpec(block, lambda i, *_: (jnp.minimum(i, n_first - 1),) + zeros),
            pl.BlockSpec(block, lambda i, *_: (jnp.maximum(i - n_first, 0),) + zeros)]


def _in_proj_kernel(xa_ref, xb_ref, nw_ref, w_ref, o_ref, xn_ref, *, n_first):
    @pl.when(pl.program_id(1) == 0)
    def _():
        first = pl.program_id(0) < n_first

        def body(c, carry):
            rows = pl.ds(pl.multiple_of(c * ROW_CHUNK, ROW_CHUNK), ROW_CHUNK)
            x = jnp.where(first, xa_ref[rows, :], xb_ref[rows, :])
            xn_ref[rows, :] = (x * _rms_scale(x) * nw_ref[...]).astype(BF16)
            return carry
        lax.fori_loop(0, xa_ref.shape[0] // ROW_CHUNK, body, 0)

    o_ref[...] = jnp.dot(xn_ref[...], w_ref[...], preferred_element_type=F32)


def _in_proj(xa, xb, norm_w, w_bf16):
    d = xa.shape[1]
    n = xa.shape[0] + xb.shape[0]
    cols = w_bf16.shape[1]
    tm, tn = IN_TM, IN_TN
    vmem = 4 * tm * d * 4 + tm * d * 2 + 2 * d * tn * 2 + 2 * tm * tn * 4 + (8 << 20)
    return pl.pallas_call(
        functools.partial(_in_proj_kernel, n_first=xa.shape[0] // tm),
        grid=(n // tm, cols // tn),
        in_specs=_pair_specs((tm, d), xa.shape[0] // tm) + [
            pl.BlockSpec((1, d), lambda i, j: (0, 0)),
            pl.BlockSpec((d, tn), lambda i, j: (0, j))],
        out_specs=pl.BlockSpec((tm, tn), lambda i, j: (i, j)),
        out_shape=jax.ShapeDtypeStruct((n, cols), F32),
        scratch_shapes=[pltpu.VMEM((tm, d), BF16)],
        compiler_params=_params(("parallel", "arbitrary"), vmem),
        name="in_proj",
    )(xa, xb, norm_w.reshape(1, d), w_bf16)


def _rope_tables(t_max):
    rows = t_max // GRID_W
    row = jnp.repeat(jnp.arange(rows, dtype=F32), GRID_W)
    col = jnp.tile(jnp.arange(GRID_W, dtype=F32), rows)
    inv_freq = ROPE_THETA ** (-jnp.arange(ROPE_AXIS_PAIRS, dtype=F32) / ROPE_AXIS_PAIRS)
    ang = jnp.concatenate([row[:, None] * inv_freq, col[:, None] * inv_freq], axis=-1)
    cos = jnp.repeat(jnp.cos(ang), 2, axis=-1)
    sin = jnp.repeat(jnp.sin(ang), 2, axis=-1)
    even = (jnp.arange(HEAD_DIM) % 2 == 0)[None, :]
    s_next = jnp.where(even, -sin, 0.0)
    s_prev = jnp.where(even, 0.0, sin)
    return cos, s_next, s_prev


def _qk_rope_kernel(q_ref, k_ref, v_ref, c_ref, sn_ref, sp_ref, qw_ref, kw_ref,
                    qo_ref, ko_ref, vo_ref):
    c, sn, sp = c_ref[...], sn_ref[...], sp_ref[...]

    def norm_rope(x, w):
        y = x * _rms_scale(x) * w
        return y * c + pltpu.roll(y, HEAD_DIM - 1, 1) * sn + pltpu.roll(y, 1, 1) * sp

    scale = HEAD_DIM ** -0.5
    for h in range(N_Q_HEADS):
        sl = slice(h * HEAD_DIM, (h + 1) * HEAD_DIM)
        qo_ref[:, sl] = (norm_rope(q_ref[:, sl], qw_ref[...]) * scale).astype(BF16)
    for h in range(N_KV_HEADS):
        sl = slice(h * HEAD_DIM, (h + 1) * HEAD_DIM)
        ko_ref[:, sl] = norm_rope(k_ref[:, sl], kw_ref[...]).astype(BF16)
    vo_ref[...] = v_ref[...].astype(BF16)


def _qk_rope(proj, q_norm_w, k_norm_w, groups):
    n = proj.shape[0]
    tm = QK_TM
    t_max = max(t for _, t in groups)
    cos, s_next, s_prev = _rope_tables(t_max)

    def pos_block(i):
        blk = jnp.int32(0)
        start = 0
        for n_tok, t in groups:
            first = start // tm
            blk = jnp.where(i >= first, (i - first) % (t // tm), blk)
            start += n_tok
        return blk

    tab_spec = pl.BlockSpec((tm, HEAD_DIM), lambda i: (pos_block(i), 0))
    q_blk, k_blk, v_blk = (5 * HG_WIDTH) // Q_COLS, (5 * HG_WIDTH + Q_COLS) // KV_COLS, \
        (5 * HG_WIDTH + Q_COLS + KV_COLS) // KV_COLS
    return pl.pallas_call(
        _qk_rope_kernel,
        grid=(n // tm,),
        in_specs=[pl.BlockSpec((tm, Q_COLS), lambda i: (i, q_blk)),
                  pl.BlockSpec((tm, KV_COLS), lambda i: (i, k_blk)),
                  pl.BlockSpec((tm, KV_COLS), lambda i: (i, v_blk)),
                  tab_spec, tab_spec, tab_spec,
                  pl.BlockSpec((1, HEAD_DIM), lambda i: (0, 0)),
                  pl.BlockSpec((1, HEAD_DIM), lambda i: (0, 0))],
        out_specs=[pl.BlockSpec((tm, Q_COLS), lambda i: (i, 0)),
                   pl.BlockSpec((tm, KV_COLS), lambda i: (i, 0)),
                   pl.BlockSpec((tm, KV_COLS), lambda i: (i, 0))],
        out_shape=[jax.ShapeDtypeStruct((n, Q_COLS), BF16),
                   jax.ShapeDtypeStruct((n, KV_COLS), BF16),
                   jax.ShapeDtypeStruct((n, KV_COLS), BF16)],
        compiler_params=_params(("parallel",), 32 << 20),
        name="qk_rope",
    )(proj, proj, proj, cos, s_next, s_prev,
      q_norm_w.reshape(1, HEAD_DIM), k_norm_w.reshape(1, HEAD_DIM))


def _attn_kernel(q_ref, k_ref, v_ref, o_ref):
    k = k_ref[...]
    v = v_ref[...]
    for h in range(Q_PER_KV):
        sl = slice(h * HEAD_DIM, (h + 1) * HEAD_DIM)
        s = _dot_nt(q_ref[:, sl], k)
        p = jnp.exp(s - jnp.max(s, axis=-1, keepdims=True))
        l = jnp.sum(p, axis=-1, keepdims=True)
        o = jnp.dot(p.astype(BF16), v, preferred_element_type=F32)
        o_ref[:, sl] = (o / l).astype(BF16)


def _attention(q, k, v, tok0, batch, t):
    tq = ATT_TQ
    nq = t // tq
    row0, seq0 = tok0 // tq, tok0 // t
    gcols = Q_PER_KV * HEAD_DIM
    vmem = 4 * t * HEAD_DIM * 2 + 3 * tq * t * 4 + (8 << 20)
    q_spec = pl.BlockSpec((tq, gcols), lambda b, g, i: (row0 + b * nq + i, g))
    kv_spec = pl.BlockSpec((t, HEAD_DIM), lambda b, g, i: (seq0 + b, g))
    return pl.pallas_call(
        _attn_kernel,
        grid=(batch, N_KV_HEADS, nq),
        in_specs=[q_spec, kv_spec, kv_spec],
        out_specs=q_spec,
        out_shape=jax.ShapeDtypeStruct(q.shape, BF16),
        input_output_aliases={0: 0},
        compiler_params=_params(("parallel", "parallel", "arbitrary"), vmem),
        name="attention",
    )(q, k, v)


def _hgrn_constants():
    c = CHUNK
    t = np.arange(c)[:, None]
    u = np.arange(c)[None, :]
    sums = [(u <= t), (u > t)]
    masks = []
    for lvl in range(N_LEVELS):
        half = (c // 2) >> lvl
        bound = (t // (2 * half)) * (2 * half) + half - 1
        late = t > bound
        sums.append(np.where(late, (u > bound) & (u <= t), (u > t) & (u <= bound)))
        same = (t // (2 * half)) == (u // (2 * half))
        masks.append(same & late & (u <= bound))
    masks.append(t == u)
    sums = np.stack(sums).astype(np.float32)
    masks = np.stack(masks).astype(np.float32)
    both_s = np.stack([sums, sums[:, ::-1, ::-1]]).reshape(2, -1, c)
    both_m = np.stack([masks, masks[:, ::-1, ::-1]]).reshape(2, -1, c)
    return jnp.asarray(both_s, BF16), jnp.asarray(both_m, F32)


def _hgrn_kernel(q_ref, f_ref, v_ref, lb_ref, sums_ref, masks_ref, o_ref, st_ref, *, nchunk, nblk,
                 seq_first_blk, seq_last_blk):
    d = pl.program_id(0)
    i = pl.program_id(1)
    blk = i + d * (nblk - 1 - 2 * i)
    fresh = functools.reduce(
        jnp.logical_or,
        [jnp.logical_or(jnp.logical_and(d == 0, blk == f), jnp.logical_and(d == 1, blk == l))
         for f, l in zip(seq_first_blk, seq_last_blk)])

    @pl.when(fresh)
    def _():
        st_ref[...] = jnp.zeros_like(st_ref)

    lb = lb_ref[0]
    sums = sums_ref[0]
    c = CHUNK

    def chunk_body(j, carry):
        jj = j + d * (nchunk - 1 - 2 * j)
        rows = pl.ds(pl.multiple_of(jj * c, c), c)
        f = lb + (1.0 - lb) * jax.nn.sigmoid(f_ref[rows, :])
        k = 1.0 - f
        lf = jnp.log(f)
        qpre = q_ref[rows, :]
        q = qpre * jax.nn.sigmoid(qpre)
        vb = v_ref[rows, :].astype(BF16)
        lf_hi = lf.astype(BF16)
        lf_lo = (lf - lf_hi.astype(F32)).astype(BF16)
        e = jnp.exp(jnp.dot(sums, lf_hi, preferred_element_type=F32)
                    + jnp.dot(sums, lf_lo, preferred_element_type=F32))
        g = jnp.exp(jnp.sum(lf, axis=0, keepdims=True))
        qt = (q * e[0:c]).astype(BF16)
        kd = (k * e[c:2 * c]).astype(BF16)
        ql = [q.astype(BF16)] + [(q * e[(2 + lvl) * c:(3 + lvl) * c]).astype(BF16) for lvl in range(N_LEVELS)]
        kl = [k.astype(BF16)] + [(k * e[(2 + lvl) * c:(3 + lvl) * c]).astype(BF16) for lvl in range(N_LEVELS)]
        mk = [masks_ref[0, N_LEVELS * c:(N_LEVELS + 1) * c, :]] + [
            masks_ref[0, lvl * c:(lvl + 1) * c, :] for lvl in range(N_LEVELS)]
        heads = [slice(h * HG_HEAD_DIM, (h + 1) * HG_HEAD_DIM) for h in range(HG_HEADS)]
        scores = []
        for sl in heads:
            a = mk[0] * _dot_nt(ql[0][:, sl], kl[0][:, sl])
            for lvl in range(1, N_LEVELS + 1):
                a = a + mk[lvl] * _dot_nt(ql[lvl][:, sl], kl[lvl][:, sl])
            scores.append(a.astype(BF16))
        for h, sl in enumerate(heads):
            st = st_ref[h]
            o = _dot_nt(qt[:, sl], st.astype(BF16)) + jnp.dot(scores[h], vb[:, sl], preferred_element_type=F32)
            ut = lax.dot_general(vb[:, sl], kd[:, sl], (((0,), (0,)), ((), ())), preferred_element_type=F32)
            st_ref[h] = st * g[:, sl] + ut
            o_ref[0, rows, sl] = o
        return carry

    lax.fori_loop(0, nchunk, chunk_body, 0)


def _hgrn(proj, lb, sums, masks, groups):
    n = proj.shape[0]
    tb = HG_TB
    nblk = n // tb
    first_blk, last_blk, tok = [], [], 0
    for n_tok, t in groups:
        for s in range(n_tok // t):
            first_blk.append((tok + s * t) // tb)
            last_blk.append((tok + (s + 1) * t) // tb - 1)
        tok += n_tok

    def rows(d, i):
        return i + d * (nblk - 1 - 2 * i)

    in_blk = lambda col: pl.BlockSpec((tb, HG_WIDTH), lambda d, i: (rows(d, i), col(d)))
    return pl.pallas_call(
        functools.partial(_hgrn_kernel, nchunk=tb // CHUNK, nblk=nblk,
                          seq_first_blk=tuple(first_blk), seq_last_blk=tuple(last_blk)),
        grid=(2, nblk),
        in_specs=[in_blk(lambda d: 0), in_blk(lambda d: 1 + d), in_blk(lambda d: 3),
                  pl.BlockSpec((1, 1, HG_WIDTH), lambda d, i: (d, 0, 0)),
                  pl.BlockSpec((1,) + sums.shape[1:], lambda d, i: (d, 0, 0)),
                  pl.BlockSpec((1,) + masks.shape[1:], lambda d, i: (d, 0, 0))],
        out_specs=pl.BlockSpec((1, tb, HG_WIDTH), lambda d, i: (d, rows(d, i), 0)),
        out_shape=jax.ShapeDtypeStruct((2, n, HG_WIDTH), F32),
        scratch_shapes=[pltpu.VMEM((HG_HEADS, HG_HEAD_DIM, HG_HEAD_DIM), F32)],
        compiler_params=_params(("parallel", "arbitrary"), 40 << 20),
        name="hgrn2",
    )(proj, proj, proj, lb.reshape(2, 1, HG_WIDTH), sums, masks)


def _out_router_kernel(xa_ref, xb_ref, of_ref, ob_ref, g_ref, att_ref, hw_ref, wo_ref, fw_ref, rw_ref, rb_ref,
                       h_ref, xn_ref, eidx_ref, gate_ref, cnt_ref, cnt_acc, *, n_first):
    @pl.when(pl.program_id(0) == 0)
    def _():
        cnt_acc[...] = jnp.zeros_like(cnt_acc)

    o = of_ref[0] + ob_ref[0]
    parts = []
    for h in range(HG_HEADS):
        sl = slice(h * HG_HEAD_DIM, (h + 1) * HG_HEAD_DIM)
        oh = o[:, sl]
        gh = g_ref[:, sl]
        parts.append(((oh * _rms_scale(oh) * hw_ref[...]) * (gh * jax.nn.sigmoid(gh))).astype(BF16))
    oa = jnp.concatenate(parts, axis=1)
    h1 = (jnp.where(pl.program_id(0) < n_first, xa_ref[...], xb_ref[...])
          + jnp.dot(oa, wo_ref[0:HG_WIDTH, :], preferred_element_type=F32)
          + jnp.dot(att_ref[...], wo_ref[HG_WIDTH:, :], preferred_element_type=F32))
    h_ref[...] = h1
    xn = h1 * _rms_scale(h1) * fw_ref[...]
    for cb in range(xn_ref.shape[1]):
        xn_ref[:, cb, :] = xn[:, cb * LANES:(cb + 1) * LANES]

    logits = jnp.dot(xn, rw_ref[...], precision=lax.Precision.HIGHEST,
                     preferred_element_type=F32) + rb_ref[...]
    tm = logits.shape[0]
    lane = lax.broadcasted_iota(jnp.int32, (tm, LANES), 1)
    work = logits
    vals, idxs = [], []
    for _ in range(TOP_K):
        m = jnp.max(work, axis=-1, keepdims=True)
        idx = jnp.min(jnp.where(work == m, lane, LANES), axis=-1, keepdims=True)
        vals.append(m)
        idxs.append(idx)
        work = jnp.where(lane == idx, -jnp.inf, work)
    exps = [jnp.exp(v - vals[0]) for v in vals]
    denom = exps[0] + exps[1] + exps[2] + exps[3]

    onehot = jnp.zeros((tm, LANES), F32)
    for idx in idxs:
        onehot = onehot + (lane == idx).astype(F32)
    eidx = jnp.zeros((tm, LANES), jnp.int32)
    gate = jnp.zeros((tm, LANES), F32)
    for kk in range(TOP_K):
        eidx = jnp.where(lane == kk, idxs[kk], eidx)
        gate = jnp.where(lane == kk, exps[kk] / denom, gate)
    eidx_ref[...] = eidx
    gate_ref[...] = gate
    cnt_acc[...] = cnt_acc[...] + jnp.sum(onehot, axis=0, keepdims=True)
    cnt_ref[...] = cnt_acc[...]


def _out_router(xa, xb, o_dirs, proj, att, hgrn_norm_w, w_out_bf16, ffn_norm_w, router_w, router_b):
    d = xa.shape[1]
    n = xa.shape[0] + xb.shape[0]
    tm = OUT_TM
    rw = jnp.zeros((d, LANES), F32).at[:, :N_EXPERTS].set(router_w.astype(F32))
    rb = jnp.full((1, LANES), -1e30, F32).at[0, :N_EXPERTS].set(router_b.astype(F32))
    g_blk = 4
    row = lambda i: (i, 0)
    const = lambda i: (0, 0)
    vmem = 2 * d * d * 2 + 12 * tm * d * 4 + (12 << 20)
    return pl.pallas_call(
        functools.partial(_out_router_kernel, n_first=xa.shape[0] // tm),
        grid=(n // tm,),
        in_specs=_pair_specs((tm, d), xa.shape[0] // tm) + [
                  pl.BlockSpec((1, tm, HG_WIDTH), lambda i: (0, i, 0)),
                  pl.BlockSpec((1, tm, HG_WIDTH), lambda i: (1, i, 0)),
                  pl.BlockSpec((tm, HG_WIDTH), lambda i: (i, g_blk)),
                  pl.BlockSpec((tm, Q_COLS), row),
                  pl.BlockSpec((1, HG_HEAD_DIM), const),
                  pl.BlockSpec((d, d), const),
                  pl.BlockSpec((1, d), const),
                  pl.BlockSpec((d, LANES), const),
                  pl.BlockSpec((1, LANES), const)],
        out_specs=[pl.BlockSpec((tm, d), row), pl.BlockSpec((tm, d // LANES, LANES), lambda i: (i, 0, 0)),
                   pl.BlockSpec((tm, LANES), row), pl.BlockSpec((tm, LANES), row),
                   pl.BlockSpec((1, LANES), const)],
        out_shape=[jax.ShapeDtypeStruct((n, d), F32), jax.ShapeDtypeStruct((n, d // LANES, LANES), F32),
                   jax.ShapeDtypeStruct((n, LANES), jnp.int32),
                   jax.ShapeDtypeStruct((n, LANES), F32), jax.ShapeDtypeStruct((1, LANES), F32)],
        scratch_shapes=[pltpu.VMEM((1, LANES), F32)],
        compiler_params=_params(("arbitrary",), vmem),
        name="out_router",
    )(xa, xb, o_dirs, o_dirs, proj, att, hgrn_norm_w.reshape(1, HG_HEAD_DIM), w_out_bf16,
      ffn_norm_w.reshape(1, d), rw, rb)


ROW_PITCH = 20
GATHER_UNROLL = 8


def _start_row_gather(idx_ref, n_rows, src_hbm, buf, sem):
    slabs = src_hbm.shape[1]

    def issue(r, carry):
        pltpu.make_async_copy(src_hbm.at[idx_ref[0, 0, r]], buf.at[pl.ds(r * ROW_PITCH, slabs), :], sem).start()
        return carry
    lax.fori_loop(0, n_rows, issue, 0, unroll=GATHER_UNROLL)


def _wait_row_gather(n_rows, slabs, buf, sem):
    done = buf.at[pl.ds(0, n_rows * slabs), :]
    pltpu.make_async_copy(done, done, sem).wait()


def _gathered_cols(buf, row0, n_rows, cb):
    return buf[pl.ds(row0 * ROW_PITCH + cb, n_rows, stride=ROW_PITCH), :]


def _start_row_scatter(idx_ref, src, dst_hbm, sem):
    def issue(r, carry):
        pltpu.make_async_copy(src.at[pl.ds(r, 1), :], dst_hbm.at[pl.ds(idx_ref[0, 0, r], 1), :], sem).start()
        return carry
    lax.fori_loop(0, src.shape[0], issue, 0, unroll=GATHER_UNROLL)


def _wait_row_scatter(src, sem):
    pltpu.make_async_copy(src, src, sem).wait()


def _moe_kernel(vb_ref, ve_ref, vlo_ref, vhi_ref, nvis_ref,
                tok_ref, tok_next_ref, orow_ref, xn_hbm, wg_ref, wl_ref, bg_ref, bl_ref, wd_ref, bd_ref,
                y_hbm, raw, xb, acc, wgb, wlb, wdb, sem_in, sem_out, *, nblk):
    w = pl.program_id(0)
    j = pl.program_id(1)
    nf = pl.num_programs(1)
    r = xb.shape[0]
    slabs = xn_hbm.shape[1]
    valid = w < nvis_ref[0]
    blk, lo, hi = vb_ref[w], vlo_ref[w], vhi_ref[w]
    slot = lax.rem(blk, 2)
    acc_blk = acc.at[slot]

    @pl.when(jnp.logical_and(valid, jnp.logical_and(lo == 0, j == 0)))
    def _():
        @pl.when(w == 0)
        def _():
            _start_row_gather(tok_ref, r, xn_hbm, raw, sem_in)
        _wait_row_gather(r, slabs, raw, sem_in)
        for cb in range(slabs):
            xb[:, cb * LANES:(cb + 1) * LANES] = _gathered_cols(raw, 0, r, cb).astype(BF16)

        @pl.when(blk + 1 < nblk)
        def _():
            _start_row_gather(tok_next_ref, r, xn_hbm, raw, sem_in)

        @pl.when(blk >= 2)
        def _():
            _wait_row_scatter(acc_blk, sem_out.at[slot])
        acc_blk[...] = jnp.zeros(acc_blk.shape, F32)

    @pl.when(valid)
    def _():
        wgb[...] = wg_ref[0].astype(BF16)
        wlb[...] = wl_ref[0].astype(BF16)
        wdb[...] = wd_ref[0].astype(BF16)
        sub_lo = lax.shift_right_logical(lo, MOE_SUB.bit_length() - 1)
        sub_hi = lax.shift_right_logical(hi + (MOE_SUB - 1), MOE_SUB.bit_length() - 1)
        n_sub = r // MOE_SUB
        for s0 in range(n_sub):
            for s1 in range(s0 + 1, n_sub + 1):
                rows = slice(s0 * MOE_SUB, s1 * MOE_SUB)

                @pl.when(jnp.logical_and(sub_lo == s0, sub_hi == s1))
                def _(rows=rows):
                    m = rows.stop - rows.start
                    xs = xb[rows, :]
                    gl = jnp.dot(xs, wgb[...], preferred_element_type=F32) + bg_ref[0]
                    lin = jnp.dot(xs, wlb[...], preferred_element_type=F32) + bl_ref[0]
                    gl = jnp.minimum(gl, SWIGLU_LIMIT)
                    lin = jnp.clip(lin, -SWIGLU_LIMIT, SWIGLU_LIMIT)
                    act = (lin + 1.0) * (gl * jax.nn.sigmoid(SWIGLU_ALPHA * gl))
                    rid = rows.start + lax.broadcasted_iota(jnp.int32, (m, 1), 0)
                    mine = jnp.logical_and(rid >= lo, rid < hi).astype(F32)
                    acc_blk[rows, :] += jnp.dot((act * mine).astype(BF16), wdb[...], preferred_element_type=F32)

                    @pl.when(j == 0)
                    def _():
                        acc_blk[rows, :] += mine * bd_ref[0]

        @pl.when(jnp.logical_and(hi == r, j == nf - 1))
        def _():
            _start_row_scatter(orow_ref, acc_blk, y_hbm, sem_out.at[slot])

    @pl.when(jnp.logical_and(w == pl.num_programs(0) - 1, j == nf - 1))
    def _():
        for s in range(min(2, nblk)):
            _wait_row_scatter(acc.at[s], sem_out.at[s])


def _moe_experts(vis_blk, vis_exp, vis_lo, vis_hi, n_vis, slot_tok, out_row, xn3,
                 w_gate_up, b_gate_up, w_down, b_down):
    r, tf = MOE_R, MOE_TF
    n, slabs, _ = xn3.shape
    d = slabs * LANES
    nblk = slot_tok.shape[0]
    nf = D_FF // tf

    def tile(w, j, nv):
        return jnp.where(w < nv[0], j, nf - 1)

    idx_spec = lambda f: pl.BlockSpec((1, 1, r), lambda w, j, vb, ve, lo, hi, nv: (f(vb[w]), 0, 0),
                                      memory_space=pltpu.SMEM)
    vmem = (r * ROW_PITCH * LANES * 4 + r * d * 2 + 2 * r * d * 4 + 2 * 3 * d * tf * 4 + 3 * d * tf * 2
            + 3 * MOE_SUB * d * 4 + (6 << 20))
    return pl.pallas_call(
        functools.partial(_moe_kernel, nblk=nblk),
        grid_spec=pltpu.PrefetchScalarGridSpec(
            num_scalar_prefetch=5,
            grid=(vis_blk.shape[0], nf),
            in_specs=[idx_spec(lambda b: b), idx_spec(lambda b: jnp.minimum(b + 1, nblk - 1)), idx_spec(lambda b: b),
                      pl.BlockSpec(memory_space=pl.ANY),
                      pl.BlockSpec((1, d, tf), lambda w, j, vb, ve, lo, hi, nv: (ve[w], 0, tile(w, j, nv))),
                      pl.BlockSpec((1, d, tf), lambda w, j, vb, ve, lo, hi, nv: (ve[w], 0, nf + tile(w, j, nv))),
                      pl.BlockSpec((1, 1, tf), lambda w, j, vb, ve, lo, hi, nv: (ve[w], 0, tile(w, j, nv))),
                      pl.BlockSpec((1, 1, tf), lambda w, j, vb, ve, lo, hi, nv: (ve[w], 0, nf + tile(w, j, nv))),
                      pl.BlockSpec((1, tf, d), lambda w, j, vb, ve, lo, hi, nv: (ve[w], tile(w, j, nv), 0)),
                      pl.BlockSpec((1, 1, d), lambda w, j, vb, ve, lo, hi, nv: (ve[w], 0, 0))],
            out_specs=pl.BlockSpec(memory_space=pl.ANY),
            scratch_shapes=[pltpu.VMEM((r * ROW_PITCH, LANES), F32),
                            pltpu.VMEM((r, d), BF16),
                            pltpu.VMEM((2, r, d), F32),
                            pltpu.VMEM((d, tf), BF16), pltpu.VMEM((d, tf), BF16), pltpu.VMEM((tf, d), BF16),
                            pltpu.SemaphoreType.DMA, pltpu.SemaphoreType.DMA((2,))]),
        out_shape=jax.ShapeDtypeStruct((TOP_K * n, d), F32),
        compiler_params=_params(("arbitrary", "arbitrary"), vmem),
        name="moe_experts",
    )(vis_blk, vis_exp, vis_lo, vis_hi, n_vis, slot_tok, slot_tok, out_row, xn3,
      w_gate_up, w_gate_up, b_gate_up.reshape(N_EXPERTS, 1, 2 * D_FF),
      b_gate_up.reshape(N_EXPERTS, 1, 2 * D_FF), w_down, b_down.reshape(N_EXPERTS, 1, d))


def _final_kernel(h_ref, gate_ref, y0_ref, y1_ref, y2_ref, y3_ref, pa_ref, pb_ref, pp_ref, pnw_ref, pg_ref,
                  oa_ref, ob_ref, *, n_first):
    first = pl.program_id(0) < n_first
    h = h_ref[...]
    for kk, y_ref in enumerate((y0_ref, y1_ref, y2_ref, y3_ref)):
        h = h + gate_ref[:, kk:kk + 1] * y_ref[...]
    gate = jax.nn.sigmoid(jnp.dot(h.astype(BF16), pg_ref[...], preferred_element_type=F32))
    p = jnp.where(first, pa_ref[...], pb_ref[...])
    pe = jnp.dot(p.astype(BF16), pp_ref[...], preferred_element_type=F32)
    pe = pe * _rms_scale(pe) * pnw_ref[...]
    out = h + gate * pe

    @pl.when(first)
    def _():
        oa_ref[...] = out

    @pl.when(jnp.logical_not(first))
    def _():
        ob_ref[...] = out


def _final(y4, h1, gates, pa, pb, ple_proj_bf16, ple_norm_w, ple_gate_bf16):
    n, d = h1.shape
    tm = FIN_TM
    nt = n // tm
    n_first = pa.shape[0] // tm
    row = lambda i: (i, 0)
    const = lambda i: (0, 0)
    vmem = 2 * d * d * 2 + 20 * tm * d * 4 + (8 << 20)
    return pl.pallas_call(
        functools.partial(_final_kernel, n_first=n_first),
        grid=(nt,),
        in_specs=[pl.BlockSpec((tm, d), row),
                  pl.BlockSpec((tm, LANES), row)]
        + [pl.BlockSpec((tm, d), lambda i, kk=kk: (kk * nt + i, 0)) for kk in range(TOP_K)]
        + _pair_specs((tm, PLE_DIM), n_first)
        + [pl.BlockSpec((PLE_DIM, d), const),
           pl.BlockSpec((1, d), const),
           pl.BlockSpec((d, d), const)],
        out_specs=_pair_specs((tm, d), n_first),
        out_shape=[jax.ShapeDtypeStruct((pa.shape[0], d), F32), jax.ShapeDtypeStruct((pb.shape[0], d), F32)],
        compiler_params=_params(("arbitrary",), vmem),
        name="combine_final",
    )(h1, gates, y4, y4, y4, y4, pa, pb, ple_proj_bf16, ple_norm_w.reshape(1, d), ple_gate_bf16)


def _routing(eidx, counts, n_tokens):
    r = MOE_R
    nblk = (n_tokens * TOP_K) // r
    nvis_max = nblk + N_EXPERTS - 1
    order = jnp.argsort(eidx[:, :TOP_K].reshape(-1), stable=True).astype(jnp.int32)
    slot_tok = order // TOP_K
    out_row = (order % TOP_K) * n_tokens + slot_tok
    cnt = counts[0, :N_EXPERTS].astype(jnp.int32)
    ends = jnp.cumsum(cnt)
    starts = ends - cnt
    first_blk = starts // r
    n_blk = jnp.where(cnt > 0, (ends - 1) // r - first_blk + 1, 0)
    vis_end = jnp.cumsum(n_blk)
    n_vis = vis_end[-1]
    w = jnp.arange(nvis_max, dtype=jnp.int32)
    valid = w < n_vis
    e = jnp.minimum(jnp.sum((vis_end[None, :] <= w[:, None]).astype(jnp.int32), axis=1), N_EXPERTS - 1)
    e = jnp.where(valid, e, jnp.max(jnp.where(valid, e, 0)))
    blk = jnp.where(valid, first_blk[e] + w - (vis_end[e] - n_blk[e]), nblk - 1)
    lo = jnp.where(valid, jnp.clip(starts[e] - blk * r, 0, r), 0)
    hi = jnp.where(valid, jnp.clip(ends[e] - blk * r, 0, r), 0)
    i32 = lambda a: a.astype(jnp.int32)
    return (i32(blk), i32(e), i32(lo), i32(hi), i32(n_vis).reshape(1),
            slot_tok.reshape(nblk, 1, r), out_row.reshape(nblk, 1, r))


def _layer(xa, xb, pa, pb, groups, lb, mix_norm_w, w_in, hgrn_norm_w, q_norm_w, k_norm_w, w_out, ffn_norm_w,
           router_w, router_b, w_gate_up, b_gate_up, w_down, b_down, ple_proj, ple_norm_w, ple_gate):
    n = xa.shape[0] + xb.shape[0]
    proj = _in_proj(xa, xb, mix_norm_w, w_in.astype(BF16))
    att, k, v = _qk_rope(proj, q_norm_w, k_norm_w, groups)
    sums, masks = _hgrn_constants()
    o_dirs = _hgrn(proj, lb, sums, masks, groups)
    tok0 = 0
    for n_tok, t in groups:
        att = _attention(att, k, v, tok0, n_tok // t, t)
        tok0 += n_tok
    h1, xn3, eidx, gates, counts = _out_router(
        xa, xb, o_dirs, proj, att, hgrn_norm_w, w_out.astype(BF16), ffn_norm_w, router_w, router_b)
    y4 = _moe_experts(*_routing(eidx, counts, n), xn3, w_gate_up, b_gate_up, w_down, b_down)
    return _final(y4, h1, gates, pa, pb, ple_proj.astype(BF16), ple_norm_w, ple_gate.astype(BF16))


def kernel(x_prompt, x_sample, p_prompt, p_sample, mix_norm_w, w_in, hgrn_lb, hgrn_norm_w, q_norm_w, k_norm_w, w_out, ffn_norm_w, router_w, router_b, w_gate_up, b_gate_up, w_down, b_down, ple_proj, ple_norm_w, ple_gate):
    depth = w_in.shape[0]
    groups = ((x_prompt.shape[0] * x_prompt.shape[1], x_prompt.shape[1]),
              (x_sample.shape[0] * x_sample.shape[1], x_sample.shape[1]))
    d = x_prompt.shape[-1]
    ha, hb = x_prompt.reshape(-1, d), x_sample.reshape(-1, d)
    lb_all = jnp.cumsum(jax.nn.softmax(hgrn_lb.astype(F32), axis=0), axis=0)
    for i in range(depth):
        ha, hb = _layer(ha, hb, p_prompt[i].reshape(-1, PLE_DIM), p_sample[i].reshape(-1, PLE_DIM), groups,
                        lb_all[i], mix_norm_w[i], w_in[i], hgrn_norm_w[i], q_norm_w[i], k_norm_w[i],
                        w_out[i], ffn_norm_w[i], router_w[i], router_b[i], w_gate_up[i], b_gate_up[i],
                        w_down[i], b_down[i], ple_proj[i], ple_norm_w[i], ple_gate[i])
    return ha.reshape(x_prompt.shape), hb.reshape(x_sample.shape)
```

```python
import functools

import numpy as np
import jax
import jax.numpy as jnp
from jax import lax
from jax.experimental import pallas as pl
from jax.experimental.pallas import tpu as pltpu

F32 = jnp.float32
BF16 = jnp.bfloat16

D_MODEL = 2048
GRID_W = 64
HG_WIDTH = 1024
HG_HEAD_DIM = 128
HG_HEADS = HG_WIDTH // HG_HEAD_DIM
CHUNK = 64
HEAD_DIM = 128
N_Q_HEADS = 8
N_KV_HEADS = 2
Q_PER_KV = N_Q_HEADS // N_KV_HEADS
ROPE_THETA = 10000.0
ROPE_AXIS_PAIRS = HEAD_DIM // 4
N_EXPERTS = 32
TOP_K = 4
D_FF = D_MODEL
SWIGLU_LIMIT = 7.0
SWIGLU_ALPHA = 1.702
PLE_DIM = 256
NORM_EPS = 1e-6
Q_COLS = N_Q_HEADS * HEAD_DIM
KV_COLS = N_KV_HEADS * HEAD_DIM
IN_COLS = 5 * HG_WIDTH + Q_COLS + 2 * KV_COLS

LANES = 128
V7X_VMEM_BUDGET_BYTES = 56 * 1024 * 1024

IN_TM, IN_TN = 1024, 512
QK_TM = 512
ATT_TQ = 256
HG_TB = 256
OUT_TM = 256
MOE_R = 1024
MOE_SUB = 256
MOE_TF = 256
FIN_TM = 256
ROW_CHUNK = 128
N_LEVELS = 6


def _params(sem, vmem_bytes):
    return pltpu.CompilerParams(dimension_semantics=sem,
                                vmem_limit_bytes=min(int(vmem_bytes), V7X_VMEM_BUDGET_BYTES))


def _rms_scale(x):
    return lax.rsqrt(jnp.mean(x * x, axis=-1, keepdims=True) + NORM_EPS)


def _dot_nt(a, b):
    return lax.dot_general(a, b, (((1,), (1,)), ((), ())), preferred_element_type=F32)


def _pair_specs(block, n_first):
    zeros = (0,) * (len(block) - 1)
    return [pl.BlockSpec(block, lambda i, *_: (jnp.minimum(i, n_first - 1),) + zeros),
            pl.BlockSpec(block, lambda i, *_: (jnp.maximum(i - n_first, 0),) + zeros)]


def _in_proj_kernel(xa_ref, xb_ref, nw_ref, w_ref, o_ref, xn_ref, *, n_first):
    @pl.when(pl.program_id(1) == 0)
    def _():
        first = pl.program_id(0) < n_first

        def body(c, carry):
            rows = pl.ds(pl.multiple_of(c * ROW_CHUNK, ROW_CHUNK), ROW_CHUNK)
            x = jnp.where(first, xa_ref[rows, :], xb_ref[rows, :])
            xn_ref[rows, :] = (x * _rms_scale(x) * nw_ref[...]).astype(BF16)
            return carry
        lax.fori_loop(0, xa_ref.shape[0] // ROW_CHUNK, body, 0)

    o_ref[...] = jnp.dot(xn_ref[...], w_ref[...], preferred_element_type=F32)


def _in_proj(xa, xb, norm_w, w_bf16):
    d = xa.shape[1]
    n = xa.shape[0] + xb.shape[0]
    cols = w_bf16.shape[1]
    tm, tn = IN_TM, IN_TN
    vmem = 4 * tm * d * 4 + tm * d * 2 + 2 * d * tn * 2 + 2 * tm * tn * 4 + (8 << 20)
    return pl.pallas_call(
        functools.partial(_in_proj_kernel, n_first=xa.shape[0] // tm),
        grid=(n // tm, cols // tn),
        in_specs=_pair_specs((tm, d), xa.shape[0] // tm) + [
            pl.BlockSpec((1, d), lambda i, j: (0, 0)),
            pl.BlockSpec((d, tn), lambda i, j: (0, j))],
        out_specs=pl.BlockSpec((tm, tn), lambda i, j: (i, j)),
        out_shape=jax.ShapeDtypeStruct((n, cols), F32),
        scratch_shapes=[pltpu.VMEM((tm, d), BF16)],
        compiler_params=_params(("parallel", "arbitrary"), vmem),
        name="in_proj",
    )(xa, xb, norm_w.reshape(1, d), w_bf16)


def _rope_tables(t_max):
    rows = t_max // GRID_W
    row = jnp.repeat(jnp.arange(rows, dtype=F32), GRID_W)
    col = jnp.tile(jnp.arange(GRID_W, dtype=F32), rows)
    inv_freq = ROPE_THETA ** (-jnp.arange(ROPE_AXIS_PAIRS, dtype=F32) / ROPE_AXIS_PAIRS)
    ang = jnp.concatenate([row[:, None] * inv_freq, col[:, None] * inv_freq], axis=-1)
    cos = jnp.repeat(jnp.cos(ang), 2, axis=-1)
    sin = jnp.repeat(jnp.sin(ang), 2, axis=-1)
    even = (jnp.arange(HEAD_DIM) % 2 == 0)[None, :]
    s_next = jnp.where(even, -sin, 0.0)
    s_prev = jnp.where(even, 0.0, sin)
    return cos, s_next, s_prev


def _qk_rope_kernel(q_ref, k_ref, v_ref, c_ref, sn_ref, sp_ref, qw_ref, kw_ref,
                    qo_ref, ko_ref, vo_ref):
    c, sn, sp = c_ref[...], sn_ref[...], sp_ref[...]

    def norm_rope(x, w):
        y = x * _rms_scale(x) * w
        return y * c + pltpu.roll(y, HEAD_DIM - 1, 1) * sn + pltpu.roll(y, 1, 1) * sp

    scale = HEAD_DIM ** -0.5
    for h in range(N_Q_HEADS):
        sl = slice(h * HEAD_DIM, (h + 1) * HEAD_DIM)
        qo_ref[:, sl] = (norm_rope(q_ref[:, sl], qw_ref[...]) * scale).astype(BF16)
    for h in range(N_KV_HEADS):
        sl = slice(h * HEAD_DIM, (h + 1) * HEAD_DIM)
        ko_ref[:, sl] = norm_rope(k_ref[:, sl], kw_ref[...]).astype(BF16)
    vo_ref[...] = v_ref[...].astype(BF16)


def _qk_rope(proj, q_norm_w, k_norm_w, groups):
    n = proj.shape[0]
    tm = QK_TM
    t_max = max(t for _, t in groups)
    cos, s_next, s_prev = _rope_tables(t_max)

    def pos_block(i):
        blk = jnp.int32(0)
        start = 0
        for n_tok, t in groups:
            first = start // tm
            blk = jnp.where(i >= first, (i - first) % (t // tm), blk)
            start += n_tok
        return blk

    tab_spec = pl.BlockSpec((tm, HEAD_DIM), lambda i: (pos_block(i), 0))
    q_blk, k_blk, v_blk = (5 * HG_WIDTH) // Q_COLS, (5 * HG_WIDTH + Q_COLS) // KV_COLS, \
        (5 * HG_WIDTH + Q_COLS + KV_COLS) // KV_COLS
    return pl.pallas_call(
        _qk_rope_kernel,
        grid=(n // tm,),
        in_specs=[pl.BlockSpec((tm, Q_COLS), lambda i: (i, q_blk)),
                  pl.BlockSpec((tm, KV_COLS), lambda i: (i, k_blk)),
                  pl.BlockSpec((tm, KV_COLS), lambda i: (i, v_blk)),
                  tab_spec, tab_spec, tab_spec,
                  pl.BlockSpec((1, HEAD_DIM), lambda i: (0, 0)),
                  pl.BlockSpec((1, HEAD_DIM), lambda i: (0, 0))],
        out_specs=[pl.BlockSpec((tm, Q_COLS), lambda i: (i, 0)),
                   pl.BlockSpec((tm, KV_COLS), lambda i: (i, 0)),
                   pl.BlockSpec((tm, KV_COLS), lambda i: (i, 0))],
        out_shape=[jax.ShapeDtypeStruct((n, Q_COLS), BF16),
                   jax.ShapeDtypeStruct((n, KV_COLS), BF16),
                   jax.ShapeDtypeStruct((n, KV_COLS), BF16)],
        compiler_params=_params(("parallel",), 32 << 20),
        name="qk_rope",
    )(proj, proj, proj, cos, s_next, s_prev,
      q_norm_w.reshape(1, HEAD_DIM), k_norm_w.reshape(1, HEAD_DIM))


def _attn_kernel(q_ref, k_ref, v_ref, o_ref):
    k = k_ref[...]
    v = v_ref[...]
    heads = [slice(h * HEAD_DIM, (h + 1) * HEAD_DIM) for h in range(Q_PER_KV)]

    def scores(sl):
        return _dot_nt(q_ref[:, sl], k)

    def softmax_pv(s, sl):
        p = jnp.exp(s - jnp.max(s, axis=-1, keepdims=True))
        l = jnp.sum(p, axis=-1, keepdims=True)
        o = jnp.dot(p.astype(BF16), v, preferred_element_type=F32)
        o_ref[:, sl] = (o / l).astype(BF16)

    s = scores(heads[0])
    for h, sl in enumerate(heads):
        s_next = scores(heads[h + 1]) if h + 1 < len(heads) else None
        softmax_pv(s, sl)
        s = s_next


def _attention(q, k, v, tok0, batch, t):
    tq = ATT_TQ
    nq = t // tq
    row0, seq0 = tok0 // tq, tok0 // t
    gcols = Q_PER_KV * HEAD_DIM
    vmem = 4 * t * HEAD_DIM * 2 + 3 * tq * t * 4 + (8 << 20)
    q_spec = pl.BlockSpec((tq, gcols), lambda b, g, i: (row0 + b * nq + i, g))
    kv_spec = pl.BlockSpec((t, HEAD_DIM), lambda b, g, i: (seq0 + b, g))
    return pl.pallas_call(
        _attn_kernel,
        grid=(batch, N_KV_HEADS, nq),
        in_specs=[q_spec, kv_spec, kv_spec],
        out_specs=q_spec,
        out_shape=jax.ShapeDtypeStruct(q.shape, BF16),
        input_output_aliases={0: 0},
        compiler_params=_params(("parallel", "parallel", "arbitrary"), vmem),
        name="attention",
    )(q, k, v)


def _hgrn_constants():
    c = CHUNK
    t = np.arange(c)[:, None]
    u = np.arange(c)[None, :]
    sums = [(u <= t), (u > t)]
    masks = []
    for lvl in range(N_LEVELS):
        half = (c // 2) >> lvl
        bound = (t // (2 * half)) * (2 * half) + half - 1
        late = t > bound
        sums.append(np.where(late, (u > bound) & (u <= t), (u > t) & (u <= bound)))
        same = (t // (2 * half)) == (u // (2 * half))
        masks.append(same & late & (u <= bound))
    masks.append(t == u)
    sums = np.stack(sums).astype(np.float32)
    masks = np.stack(masks).astype(np.float32)
    both_s = np.stack([sums, sums[:, ::-1, ::-1]]).reshape(2, -1, c)
    both_m = np.stack([masks, masks[:, ::-1, ::-1]]).reshape(2, -1, c)
    return jnp.asarray(both_s, BF16), jnp.asarray(both_m, F32)


def _hgrn_kernel(q_ref, f_ref, v_ref, lb_ref, sums_ref, masks_ref, o_ref, st_ref, *, nchunk, nblk,
                 seq_first_blk, seq_last_blk):
    d = pl.program_id(0)
    i = pl.program_id(1)
    blk = i + d * (nblk - 1 - 2 * i)
    fresh = functools.reduce(
        jnp.logical_or,
        [jnp.logical_or(jnp.logical_and(d == 0, blk == f), jnp.logical_and(d == 1, blk == l))
         for f, l in zip(seq_first_blk, seq_last_blk)])

    @pl.when(fresh)
    def _():
        st_ref[...] = jnp.zeros_like(st_ref)

    lb = lb_ref[0]
    sums = sums_ref[0]
    c = CHUNK

    def chunk_body(j, carry):
        jj = j + d * (nchunk - 1 - 2 * j)
        rows = pl.ds(pl.multiple_of(jj * c, c), c)
        f = lb + (1.0 - lb) * jax.nn.sigmoid(f_ref[rows, :])
        k = 1.0 - f
        lf = jnp.log(f)
        qpre = q_ref[rows, :]
        q = qpre * jax.nn.sigmoid(qpre)
        vb = v_ref[rows, :].astype(BF16)
        lf_hi = lf.astype(BF16)
        lf_lo = (lf - lf_hi.astype(F32)).astype(BF16)
        e = jnp.exp(jnp.dot(sums, lf_hi, preferred_element_type=F32)
                    + jnp.dot(sums, lf_lo, preferred_element_type=F32))
        g = jnp.exp(jnp.sum(lf, axis=0, keepdims=True))
        qt = (q * e[0:c]).astype(BF16)
        kd = (k * e[c:2 * c]).astype(BF16)
        ql = [q.astype(BF16)] + [(q * e[(2 + lvl) * c:(3 + lvl) * c]).astype(BF16) for lvl in range(N_LEVELS)]
        kl = [k.astype(BF16)] + [(k * e[(2 + lvl) * c:(3 + lvl) * c]).astype(BF16) for lvl in range(N_LEVELS)]
        mk = [masks_ref[0, N_LEVELS * c:(N_LEVELS + 1) * c, :]] + [
            masks_ref[0, lvl * c:(lvl + 1) * c, :] for lvl in range(N_LEVELS)]
        heads = [slice(h * HG_HEAD_DIM, (h + 1) * HG_HEAD_DIM) for h in range(HG_HEADS)]
        scores = []
        for sl in heads:
            a = mk[0] * _dot_nt(ql[0][:, sl], kl[0][:, sl])
            for lvl in range(1, N_LEVELS + 1):
                a = a + mk[lvl] * _dot_nt(ql[lvl][:, sl], kl[lvl][:, sl])
            scores.append(a.astype(BF16))
        for h, sl in enumerate(heads):
            st = st_ref[h]
            o = _dot_nt(qt[:, sl], st.astype(BF16)) + jnp.dot(scores[h], vb[:, sl], preferred_element_type=F32)
            ut = lax.dot_general(vb[:, sl], kd[:, sl], (((0,), (0,)), ((), ())), preferred_element_type=F32)
            st_ref[h] = st * g[:, sl] + ut
            o_ref[0, rows, sl] = o
        return carry

    lax.fori_loop(0, nchunk, chunk_body, 0, unroll=2)


def _hgrn(proj, lb, sums, masks, groups):
    n = proj.shape[0]
    tb = HG_TB
    nblk = n // tb
    first_blk, last_blk, tok = [], [], 0
    for n_tok, t in groups:
        for s in range(n_tok // t):
            first_blk.append((tok + s * t) // tb)
            last_blk.append((tok + (s + 1) * t) // tb - 1)
        tok += n_tok

    def rows(d, i):
        return i + d * (nblk - 1 - 2 * i)

    in_blk = lambda col: pl.BlockSpec((tb, HG_WIDTH), lambda d, i: (rows(d, i), col(d)))
    return pl.pallas_call(
        functools.partial(_hgrn_kernel, nchunk=tb // CHUNK, nblk=nblk,
                          seq_first_blk=tuple(first_blk), seq_last_blk=tuple(last_blk)),
        grid=(2, nblk),
        in_specs=[in_blk(lambda d: 0), in_blk(lambda d: 1 + d), in_blk(lambda d: 3),
                  pl.BlockSpec((1, 1, HG_WIDTH), lambda d, i: (d, 0, 0)),
                  pl.BlockSpec((1,) + sums.shape[1:], lambda d, i: (d, 0, 0)),
                  pl.BlockSpec((1,) + masks.shape[1:], lambda d, i: (d, 0, 0))],
        out_specs=pl.BlockSpec((1, tb, HG_WIDTH), lambda d, i: (d, rows(d, i), 0)),
        out_shape=jax.ShapeDtypeStruct((2, n, HG_WIDTH), F32),
        scratch_shapes=[pltpu.VMEM((HG_HEADS, HG_HEAD_DIM, HG_HEAD_DIM), F32)],
        compiler_params=_params(("parallel", "arbitrary"), 40 << 20),
        name="hgrn2",
    )(proj, proj, proj, lb.reshape(2, 1, HG_WIDTH), sums, masks)


def _out_router_kernel(xa_ref, xb_ref, of_ref, ob_ref, g_ref, att_ref, hw_ref, wo_ref, fw_ref, rw_ref, rb_ref,
                       h_ref, xn_ref, eidx_ref, gate_ref, cnt_ref, cnt_acc, *, n_first):
    @pl.when(pl.program_id(0) == 0)
    def _():
        cnt_acc[...] = jnp.zeros_like(cnt_acc)

    o = of_ref[0] + ob_ref[0]
    parts = []
    for h in range(HG_HEADS):
        sl = slice(h * HG_HEAD_DIM, (h + 1) * HG_HEAD_DIM)
        oh = o[:, sl]
        gh = g_ref[:, sl]
        parts.append(((oh * _rms_scale(oh) * hw_ref[...]) * (gh * jax.nn.sigmoid(gh))).astype(BF16))
    oa = jnp.concatenate(parts, axis=1)
    h1 = (jnp.where(pl.program_id(0) < n_first, xa_ref[...], xb_ref[...])
          + jnp.dot(oa, wo_ref[0:HG_WIDTH, :], preferred_element_type=F32)
          + jnp.dot(att_ref[...], wo_ref[HG_WIDTH:, :], preferred_element_type=F32))
    h_ref[...] = h1
    xn = h1 * _rms_scale(h1) * fw_ref[...]
    for cb in range(xn_ref.shape[1]):
        xn_ref[:, cb, :] = xn[:, cb * LANES:(cb + 1) * LANES]

    logits = jnp.dot(xn, rw_ref[...], precision=lax.Precision.HIGHEST,
                     preferred_element_type=F32) + rb_ref[...]
    tm = logits.shape[0]
    lane = lax.broadcasted_iota(jnp.int32, (tm, LANES), 1)
    work = logits
    vals, idxs = [], []
    for _ in range(TOP_K):
        m = jnp.max(work, axis=-1, keepdims=True)
        idx = jnp.min(jnp.where(work == m, lane, LANES), axis=-1, keepdims=True)
        vals.append(m)
        idxs.append(idx)
        work = jnp.where(lane == idx, -jnp.inf, work)
    exps = [jnp.exp(v - vals[0]) for v in vals]
    denom = exps[0] + exps[1] + exps[2] + exps[3]

    onehot = jnp.zeros((tm, LANES), F32)
    for idx in idxs:
        onehot = onehot + (lane == idx).astype(F32)
    eidx = jnp.zeros((tm, LANES), jnp.int32)
    gate = jnp.zeros((tm, LANES), F32)
    for kk in range(TOP_K):
        eidx = jnp.where(lane == kk, idxs[kk], eidx)
        gate = jnp.where(lane == kk, exps[kk] / denom, gate)
    eidx_ref[...] = eidx
    gate_ref[...] = gate
    cnt_acc[...] = cnt_acc[...] + jnp.sum(onehot, axis=0, keepdims=True)
    cnt_ref[...] = cnt_acc[...]


def _out_router(xa, xb, o_dirs, proj, att, hgrn_norm_w, w_out_bf16, ffn_norm_w, router_w, router_b):
    d = xa.shape[1]
    n = xa.shape[0] + xb.shape[0]
    tm = OUT_TM
    rw = jnp.zeros((d, LANES), F32).at[:, :N_EXPERTS].set(router_w.astype(F32))
    rb = jnp.full((1, LANES), -1e30, F32).at[0, :N_EXPERTS].set(router_b.astype(F32))
    g_blk = 4
    row = lambda i: (i, 0)
    const = lambda i: (0, 0)
    vmem = 2 * d * d * 2 + 12 * tm * d * 4 + (12 << 20)
    return pl.pallas_call(
        functools.partial(_out_router_kernel, n_first=xa.shape[0] // tm),
        grid=(n // tm,),
        in_specs=_pair_specs((tm, d), xa.shape[0] // tm) + [
                  pl.BlockSpec((1, tm, HG_WIDTH), lambda i: (0, i, 0)),
                  pl.BlockSpec((1, tm, HG_WIDTH), lambda i: (1, i, 0)),
                  pl.BlockSpec((tm, HG_WIDTH), lambda i: (i, g_blk)),
                  pl.BlockSpec((tm, Q_COLS), row),
                  pl.BlockSpec((1, HG_HEAD_DIM), const),
                  pl.BlockSpec((d, d), const),
                  pl.BlockSpec((1, d), const),
                  pl.BlockSpec((d, LANES), const),
                  pl.BlockSpec((1, LANES), const)],
        out_specs=[pl.BlockSpec((tm, d), row), pl.BlockSpec((tm, d // LANES, LANES), lambda i: (i, 0, 0)),
                   pl.BlockSpec((tm, LANES), row), pl.BlockSpec((tm, LANES), row),
                   pl.BlockSpec((1, LANES), const)],
        out_shape=[jax.ShapeDtypeStruct((n, d), F32), jax.ShapeDtypeStruct((n, d // LANES, LANES), F32),
                   jax.ShapeDtypeStruct((n, LANES), jnp.int32),
                   jax.ShapeDtypeStruct((n, LANES), F32), jax.ShapeDtypeStruct((1, LANES), F32)],
        scratch_shapes=[pltpu.VMEM((1, LANES), F32)],
        compiler_params=_params(("arbitrary",), vmem),
        name="out_router",
    )(xa, xb, o_dirs, o_dirs, proj, att, hgrn_norm_w.reshape(1, HG_HEAD_DIM), w_out_bf16,
      ffn_norm_w.reshape(1, d), rw, rb)


ROW_PITCH = 20
GATHER_UNROLL = 8


def _start_row_gather(idx_ref, n_rows, src_hbm, buf, sem):
    slabs = src_hbm.shape[1]

    def issue(g, carry):
        base = pl.multiple_of(g * GATHER_UNROLL, GATHER_UNROLL)
        for u in range(GATHER_UNROLL):
            r = base + u
            pltpu.make_async_copy(src_hbm.at[idx_ref[0, 0, r]], buf.at[pl.ds(r * ROW_PITCH, slabs), :], sem).start()
        return carry
    lax.fori_loop(0, n_rows // GATHER_UNROLL, issue, 0)


def _wait_row_gather(n_rows, slabs, buf, sem):
    done = buf.at[pl.ds(0, n_rows * slabs), :]
    pltpu.make_async_copy(done, done, sem).wait()


def _gathered_cols(buf, row0, n_rows, cb):
    return buf[pl.ds(row0 * ROW_PITCH + cb, n_rows, stride=ROW_PITCH), :]


def _start_row_scatter(idx_ref, src, dst_hbm, sem):
    def issue(g, carry):
        base = pl.multiple_of(g * GATHER_UNROLL, GATHER_UNROLL)
        for u in range(GATHER_UNROLL):
            r = base + u
            pltpu.make_async_copy(src.at[pl.ds(r, 1), :], dst_hbm.at[pl.ds(idx_ref[0, 0, r], 1), :], sem).start()
        return carry
    lax.fori_loop(0, src.shape[0] // GATHER_UNROLL, issue, 0)


def _wait_row_scatter(src, sem):
    pltpu.make_async_copy(src, src, sem).wait()


def _moe_kernel(vb_ref, ve_ref, vlo_ref, vhi_ref, nvis_ref,
                tok_ref, tok_next_ref, orow_ref, xn_hbm, wg_ref, wl_ref, bg_ref, bl_ref, wd_ref, bd_ref,
                y_hbm, raw, xb, acc, sem_in, sem_out, *, nblk):
    w = pl.program_id(0)
    j = pl.program_id(1)
    nf = pl.num_programs(1)
    r = xb.shape[0]
    slabs = xn_hbm.shape[1]
    valid = w < nvis_ref[0]
    blk, lo, hi = vb_ref[w], vlo_ref[w], vhi_ref[w]
    slot = lax.rem(blk, 2)
    acc_blk = acc.at[slot]

    @pl.when(jnp.logical_and(valid, jnp.logical_and(lo == 0, j == 0)))
    def _():
        @pl.when(w == 0)
        def _():
            _start_row_gather(tok_ref, r, xn_hbm, raw, sem_in)
        _wait_row_gather(r, slabs, raw, sem_in)
        for cb in range(slabs):
            xb[:, cb * LANES:(cb + 1) * LANES] = _gathered_cols(raw, 0, r, cb).astype(BF16)

        @pl.when(blk + 1 < nblk)
        def _():
            _start_row_gather(tok_next_ref, r, xn_hbm, raw, sem_in)

        @pl.when(blk >= 2)
        def _():
            _wait_row_scatter(acc_blk, sem_out.at[slot])
        acc_blk[...] = jnp.zeros(acc_blk.shape, F32)

    @pl.when(valid)
    def _():
        sub_lo = lax.shift_right_logical(lo, MOE_SUB.bit_length() - 1)
        sub_hi = lax.shift_right_logical(hi + (MOE_SUB - 1), MOE_SUB.bit_length() - 1)
        n_sub = r // MOE_SUB
        for s0 in range(n_sub):
            for s1 in range(s0 + 1, n_sub + 1):
                rows = slice(s0 * MOE_SUB, s1 * MOE_SUB)

                @pl.when(jnp.logical_and(sub_lo == s0, sub_hi == s1))
                def _(rows=rows):
                    m = rows.stop - rows.start
                    xs = xb[rows, :]
                    gl = jnp.dot(xs, wg_ref[0].astype(BF16), preferred_element_type=F32) + bg_ref[0]
                    lin = jnp.dot(xs, wl_ref[0].astype(BF16), preferred_element_type=F32) + bl_ref[0]
                    gl = jnp.minimum(gl, SWIGLU_LIMIT)
                    lin = jnp.clip(lin, -SWIGLU_LIMIT, SWIGLU_LIMIT)
                    act = (lin + 1.0) * (gl * jax.nn.sigmoid(SWIGLU_ALPHA * gl))
                    rid = rows.start + lax.broadcasted_iota(jnp.int32, (m, 1), 0)
                    mine = jnp.logical_and(rid >= lo, rid < hi).astype(F32)
                    acc_blk[rows, :] += jnp.dot((act * mine).astype(BF16), wd_ref[0].astype(BF16),
                                                preferred_element_type=F32)

                    @pl.when(j == 0)
                    def _():
                        acc_blk[rows, :] += mine * bd_ref[0]

        @pl.when(jnp.logical_and(hi == r, j == nf - 1))
        def _():
            _start_row_scatter(orow_ref, acc_blk, y_hbm, sem_out.at[slot])

    @pl.when(jnp.logical_and(w == pl.num_programs(0) - 1, j == nf - 1))
    def _():
        for s in range(min(2, nblk)):
            _wait_row_scatter(acc.at[s], sem_out.at[s])


def _moe_experts(vis_blk, vis_exp, vis_lo, vis_hi, n_vis, slot_tok, out_row, xn3,
                 w_gate_up, b_gate_up, w_down, b_down):
    r, tf = MOE_R, MOE_TF
    n, slabs, _ = xn3.shape
    d = slabs * LANES
    nblk = slot_tok.shape[0]
    nf = D_FF // tf

    def tile(w, j, nv):
        return jnp.where(w < nv[0], j, nf - 1)

    idx_spec = lambda f: pl.BlockSpec((1, 1, r), lambda w, j, vb, ve, lo, hi, nv: (f(vb[w]), 0, 0),
                                      memory_space=pltpu.SMEM)
    vmem = (r * ROW_PITCH * LANES * 4 + r * d * 2 + 2 * r * d * 4 + 2 * 3 * d * tf * 4 + 3 * d * tf * 2
            + 3 * MOE_SUB * d * 4 + (6 << 20))
    return pl.pallas_call(
        functools.partial(_moe_kernel, nblk=nblk),
        grid_spec=pltpu.PrefetchScalarGridSpec(
            num_scalar_prefetch=5,
            grid=(vis_blk.shape[0], nf),
            in_specs=[idx_spec(lambda b: b), idx_spec(lambda b: jnp.minimum(b + 1, nblk - 1)), idx_spec(lambda b: b),
                      pl.BlockSpec(memory_space=pl.ANY),
                      pl.BlockSpec((1, d, tf), lambda w, j, vb, ve, lo, hi, nv: (ve[w], 0, tile(w, j, nv))),
                      pl.BlockSpec((1, d, tf), lambda w, j, vb, ve, lo, hi, nv: (ve[w], 0, nf + tile(w, j, nv))),
                      pl.BlockSpec((1, 1, tf), lambda w, j, vb, ve, lo, hi, nv: (ve[w], 0, tile(w, j, nv))),
                      pl.BlockSpec((1, 1, tf), lambda w, j, vb, ve, lo, hi, nv: (ve[w], 0, nf + tile(w, j, nv))),
                      pl.BlockSpec((1, tf, d), lambda w, j, vb, ve, lo, hi, nv: (ve[w], tile(w, j, nv), 0)),
                      pl.BlockSpec((1, 1, d), lambda w, j, vb, ve, lo, hi, nv: (ve[w], 0, 0))],
            out_specs=pl.BlockSpec(memory_space=pl.ANY),
            scratch_shapes=[pltpu.VMEM((r * ROW_PITCH, LANES), F32),
                            pltpu.VMEM((r, d), BF16),
                            pltpu.VMEM((2, r, d), F32),
                            pltpu.SemaphoreType.DMA, pltpu.SemaphoreType.DMA((2,))]),
        out_shape=jax.ShapeDtypeStruct((TOP_K * n, d), F32),
        compiler_params=_params(("arbitrary", "arbitrary"), vmem),
        name="moe_experts",
    )(vis_blk, vis_exp, vis_lo, vis_hi, n_vis, slot_tok, slot_tok, out_row, xn3,
      w_gate_up, w_gate_up, b_gate_up.reshape(N_EXPERTS, 1, 2 * D_FF),
      b_gate_up.reshape(N_EXPERTS, 1, 2 * D_FF), w_down, b_down.reshape(N_EXPERTS, 1, d))


def _final_kernel(h_ref, gate_ref, y0_ref, y1_ref, y2_ref, y3_ref, pa_ref, pb_ref, pp_ref, pnw_ref, pg_ref,
                  oa_ref, ob_ref, *, n_first):
    first = pl.program_id(0) < n_first
    h = h_ref[...]
    for kk, y_ref in enumerate((y0_ref, y1_ref, y2_ref, y3_ref)):
        h = h + gate_ref[:, kk:kk + 1] * y_ref[...]
    gate = jax.nn.sigmoid(jnp.dot(h.astype(BF16), pg_ref[...], preferred_element_type=F32))
    p = jnp.where(first, pa_ref[...], pb_ref[...])
    pe = jnp.dot(p.astype(BF16), pp_ref[...], preferred_element_type=F32)
    pe = pe * _rms_scale(pe) * pnw_ref[...]
    out = h + gate * pe

    @pl.when(first)
    def _():
        oa_ref[...] = out

    @pl.when(jnp.logical_not(first))
    def _():
        ob_ref[...] = out


def _final(y4, h1, gates, pa, pb, ple_proj_bf16, ple_norm_w, ple_gate_bf16):
    n, d = h1.shape
    tm = FIN_TM
    nt = n // tm
    n_first = pa.shape[0] // tm
    row = lambda i: (i, 0)
    const = lambda i: (0, 0)
    vmem = 2 * d * d * 2 + 20 * tm * d * 4 + (8 << 20)
    return pl.pallas_call(
        functools.partial(_final_kernel, n_first=n_first),
        grid=(nt,),
        in_specs=[pl.BlockSpec((tm, d), row),
                  pl.BlockSpec((tm, LANES), row)]
        + [pl.BlockSpec((tm, d), lambda i, kk=kk: (kk * nt + i, 0)) for kk in range(TOP_K)]
        + _pair_specs((tm, PLE_DIM), n_first)
        + [pl.BlockSpec((PLE_DIM, d), const),
           pl.BlockSpec((1, d), const),
           pl.BlockSpec((d, d), const)],
        out_specs=_pair_specs((tm, d), n_first),
        out_shape=[jax.ShapeDtypeStruct((pa.shape[0], d), F32), jax.ShapeDtypeStruct((pb.shape[0], d), F32)],
        compiler_params=_params(("arbitrary",), vmem),
        name="combine_final",
    )(h1, gates, y4, y4, y4, y4, pa, pb, ple_proj_bf16, ple_norm_w.reshape(1, d), ple_gate_bf16)


def _routing(eidx, counts, n_tokens):
    r = MOE_R
    nblk = (n_tokens * TOP_K) // r
    nvis_max = nblk + N_EXPERTS - 1
    order = jnp.argsort(eidx[:, :TOP_K].reshape(-1), stable=True).astype(jnp.int32)
    slot_tok = order // TOP_K
    out_row = (order % TOP_K) * n_tokens + slot_tok
    cnt = counts[0, :N_EXPERTS].astype(jnp.int32)
    ends = jnp.cumsum(cnt)
    starts = ends - cnt
    first_blk = starts // r
    n_blk = jnp.where(cnt > 0, (ends - 1) // r - first_blk + 1, 0)
    vis_end = jnp.cumsum(n_blk)
    n_vis = vis_end[-1]
    w = jnp.arange(nvis_max, dtype=jnp.int32)
    valid = w < n_vis
    e = jnp.minimum(jnp.sum((vis_end[None, :] <= w[:, None]).astype(jnp.int32), axis=1), N_EXPERTS - 1)
    e = jnp.where(valid, e, jnp.max(jnp.where(valid, e, 0)))
    blk = jnp.where(valid, first_blk[e] + w - (vis_end[e] - n_blk[e]), nblk - 1)
    lo = jnp.where(valid, jnp.clip(starts[e] - blk * r, 0, r), 0)
    hi = jnp.where(valid, jnp.clip(ends[e] - blk * r, 0, r), 0)
    i32 = lambda a: a.astype(jnp.int32)
    return (i32(blk), i32(e), i32(lo), i32(hi), i32(n_vis).reshape(1),
            slot_tok.reshape(nblk, 1, r), out_row.reshape(nblk, 1, r))


def _layer(xa, xb, pa, pb, groups, lb, mix_norm_w, w_in, hgrn_norm_w, q_norm_w, k_norm_w, w_out, ffn_norm_w,
           router_w, router_b, w_gate_up, b_gate_up, w_down, b_down, ple_proj, ple_norm_w, ple_gate):
    n = xa.shape[0] + xb.shape[0]
    proj = _in_proj(xa, xb, mix_norm_w, w_in.astype(BF16))
    att, k, v = _qk_rope(proj, q_norm_w, k_norm_w, groups)
    sums, masks = _hgrn_constants()
    o_dirs = _hgrn(proj, lb, sums, masks, groups)
    tok0 = 0
    for n_tok, t in groups:
        att = _attention(att, k, v, tok0, n_tok // t, t)
        tok0 += n_tok
    h1, xn3, eidx, gates, counts = _out_router(
        xa, xb, o_dirs, proj, att, hgrn_norm_w, w_out.astype(BF16), ffn_norm_w, router_w, router_b)
    y4 = _moe_experts(*_routing(eidx, counts, n), xn3, w_gate_up, b_gate_up, w_down, b_down)
    return _final(y4, h1, gates, pa, pb, ple_proj.astype(BF16), ple_norm_w, ple_gate.astype(BF16))


def kernel(x_prompt, x_sample, p_prompt, p_sample, mix_norm_w, w_in, hgrn_lb, hgrn_norm_w, q_norm_w, k_norm_w, w_out, ffn_norm_w, router_w, router_b, w_gate_up, b_gate_up, w_down, b_down, ple_proj, ple_norm_w, ple_gate):
    depth = w_in.shape[0]
    groups = ((x_prompt.shape[0] * x_prompt.shape[1], x_prompt.shape[1]),
              (x_sample.shape[0] * x_sample.shape[1], x_sample.shape[1]))
    d = x_prompt.shape[-1]
    ha, hb = x_prompt.reshape(-1, d), x_sample.reshape(-1, d)
    lb_all = jnp.cumsum(jax.nn.softmax(hgrn_lb.astype(F32), axis=0), axis=0)
    for i in range(depth):
        ha, hb = _layer(ha, hb, p_prompt[i].reshape(-1, PLE_DIM), p_sample[i].reshape(-1, PLE_DIM), groups,
                        lb_all[i], mix_norm_w[i], w_in[i], hgrn_norm_w[i], q_norm_w[i], k_norm_w[i],
                        w_out[i], ffn_norm_w[i], router_w[i], router_b[i], w_gate_up[i], b_gate_up[i],
                        w_down[i], b_down[i], ple_proj[i], ple_norm_w[i], ple_gate[i])
    return ha.reshape(x_prompt.shape), hb.reshape(x_sample.shape)
```

```python
import functools

import numpy as np
import jax
import jax.numpy as jnp
from jax import lax
from jax.experimental import pallas as pl
from jax.experimental.pallas import tpu as pltpu

F32 = jnp.float32
BF16 = jnp.bfloat16

D_MODEL = 2048
GRID_W = 64
HG_WIDTH = 1024
HG_HEAD_DIM = 128
HG_HEADS = HG_WIDTH // HG_HEAD_DIM
CHUNK = 64
HEAD_DIM = 128
N_Q_HEADS = 8
N_KV_HEADS = 2
Q_PER_KV = N_Q_HEADS // N_KV_HEADS
ROPE_THETA = 10000.0
ROPE_AXIS_PAIRS = HEAD_DIM // 4
N_EXPERTS = 32
TOP_K = 4
D_FF = D_MODEL
SWIGLU_LIMIT = 7.0
SWIGLU_ALPHA = 1.702
PLE_DIM = 256
NORM_EPS = 1e-6
Q_COLS = N_Q_HEADS * HEAD_DIM
KV_COLS = N_KV_HEADS * HEAD_DIM
IN_COLS = 5 * HG_WIDTH + Q_COLS + 2 * KV_COLS

LANES = 128
V7X_VMEM_BUDGET_BYTES = 56 * 1024 * 1024

IN_TM, IN_TN = 1024, 512
QK_TM = 512
ATT_TQ = 256
ATT_TK = 1024
HG_TB = 256
OUT_TM = 256
MOE_R = 1024
MOE_SUB = 256
MOE_TF = 256
FIN_TM = 256
ROW_CHUNK = 128
N_LEVELS = 6


def _params(sem, vmem_bytes):
    return pltpu.CompilerParams(dimension_semantics=sem,
                                vmem_limit_bytes=min(int(vmem_bytes), V7X_VMEM_BUDGET_BYTES))


def _rms_scale(x):
    return lax.rsqrt(jnp.mean(x * x, axis=-1, keepdims=True) + NORM_EPS)


def _dot_nt(a, b):
    return lax.dot_general(a, b, (((1,), (1,)), ((), ())), preferred_element_type=F32)


def _pair_specs(block, n_first):
    zeros = (0,) * (len(block) - 1)
    return [pl.BlockSpec(block, lambda i, *_: (jnp.minimum(i, n_first - 1),) + zeros),
            pl.BlockSpec(block, lambda i, *_: (jnp.maximum(i - n_first, 0),) + zeros)]


def _in_proj_kernel(xa_ref, xb_ref, nw_ref, w_ref, o_ref, xn_ref, *, n_first):
    @pl.when(pl.program_id(1) == 0)
    def _():
        first = pl.program_id(0) < n_first

        def body(c, carry):
            rows = pl.ds(pl.multiple_of(c * ROW_CHUNK, ROW_CHUNK), ROW_CHUNK)
            x = jnp.where(first, xa_ref[rows, :], xb_ref[rows, :])
            xn_ref[rows, :] = (x * _rms_scale(x) * nw_ref[...]).astype(BF16)
            return carry
        lax.fori_loop(0, xa_ref.shape[0] // ROW_CHUNK, body, 0)

    o_ref[...] = jnp.dot(xn_ref[...], w_ref[...], preferred_element_type=F32)


def _in_proj(xa, xb, norm_w, w_bf16):
    d = xa.shape[1]
    n = xa.shape[0] + xb.shape[0]
    cols = w_bf16.shape[1]
    tm, tn = IN_TM, IN_TN
    vmem = 4 * tm * d * 4 + tm * d * 2 + 2 * d * tn * 2 + 2 * tm * tn * 4 + (8 << 20)
    return pl.pallas_call(
        functools.partial(_in_proj_kernel, n_first=xa.shape[0] // tm),
        grid=(n // tm, cols // tn),
        in_specs=_pair_specs((tm, d), xa.shape[0] // tm) + [
            pl.BlockSpec((1, d), lambda i, j: (0, 0)),
            pl.BlockSpec((d, tn), lambda i, j: (0, j))],
        out_specs=pl.BlockSpec((tm, tn), lambda i, j: (i, j)),
        out_shape=jax.ShapeDtypeStruct((n, cols), F32),
        scratch_shapes=[pltpu.VMEM((tm, d), BF16)],
        compiler_params=_params(("parallel", "arbitrary"), vmem),
        name="in_proj",
    )(xa, xb, norm_w.reshape(1, d), w_bf16)


def _rope_tables(t_max):
    rows = t_max // GRID_W
    row = jnp.repeat(jnp.arange(rows, dtype=F32), GRID_W)
    col = jnp.tile(jnp.arange(GRID_W, dtype=F32), rows)
    inv_freq = ROPE_THETA ** (-jnp.arange(ROPE_AXIS_PAIRS, dtype=F32) / ROPE_AXIS_PAIRS)
    ang = jnp.concatenate([row[:, None] * inv_freq, col[:, None] * inv_freq], axis=-1)
    cos = jnp.repeat(jnp.cos(ang), 2, axis=-1)
    sin = jnp.repeat(jnp.sin(ang), 2, axis=-1)
    even = (jnp.arange(HEAD_DIM) % 2 == 0)[None, :]
    s_next = jnp.where(even, -sin, 0.0)
    s_prev = jnp.where(even, 0.0, sin)
    return cos, s_next, s_prev


def _qk_rope_kernel(q_ref, k_ref, v_ref, c_ref, sn_ref, sp_ref, qw_ref, kw_ref,
                    qo_ref, ko_ref, vo_ref):
    c, sn, sp = c_ref[...], sn_ref[...], sp_ref[...]

    def norm_rope(x, w):
        y = x * _rms_scale(x) * w
        return y * c + pltpu.roll(y, HEAD_DIM - 1, 1) * sn + pltpu.roll(y, 1, 1) * sp

    scale = HEAD_DIM ** -0.5
    for h in range(N_Q_HEADS):
        sl = slice(h * HEAD_DIM, (h + 1) * HEAD_DIM)
        qo_ref[:, sl] = (norm_rope(q_ref[:, sl], qw_ref[...]) * scale).astype(BF16)
    for h in range(N_KV_HEADS):
        sl = slice(h * HEAD_DIM, (h + 1) * HEAD_DIM)
        ko_ref[:, sl] = norm_rope(k_ref[:, sl], kw_ref[...]).astype(BF16)
    vo_ref[...] = v_ref[...].astype(BF16)


def _qk_rope(proj, q_norm_w, k_norm_w, groups):
    n = proj.shape[0]
    tm = QK_TM
    t_max = max(t for _, t in groups)
    cos, s_next, s_prev = _rope_tables(t_max)

    def pos_block(i):
        blk = jnp.int32(0)
        start = 0
        for n_tok, t in groups:
            first = start // tm
            blk = jnp.where(i >= first, (i - first) % (t // tm), blk)
            start += n_tok
        return blk

    tab_spec = pl.BlockSpec((tm, HEAD_DIM), lambda i: (pos_block(i), 0))
    q_blk, k_blk, v_blk = (5 * HG_WIDTH) // Q_COLS, (5 * HG_WIDTH + Q_COLS) // KV_COLS, \
        (5 * HG_WIDTH + Q_COLS + KV_COLS) // KV_COLS
    return pl.pallas_call(
        _qk_rope_kernel,
        grid=(n // tm,),
        in_specs=[pl.BlockSpec((tm, Q_COLS), lambda i: (i, q_blk)),
                  pl.BlockSpec((tm, KV_COLS), lambda i: (i, k_blk)),
                  pl.BlockSpec((tm, KV_COLS), lambda i: (i, v_blk)),
                  tab_spec, tab_spec, tab_spec,
                  pl.BlockSpec((1, HEAD_DIM), lambda i: (0, 0)),
                  pl.BlockSpec((1, HEAD_DIM), lambda i: (0, 0))],
        out_specs=[pl.BlockSpec((tm, Q_COLS), lambda i: (i, 0)),
                   pl.BlockSpec((tm, KV_COLS), lambda i: (i, 0)),
                   pl.BlockSpec((tm, KV_COLS), lambda i: (i, 0))],
        out_shape=[jax.ShapeDtypeStruct((n, Q_COLS), BF16),
                   jax.ShapeDtypeStruct((n, KV_COLS), BF16),
                   jax.ShapeDtypeStruct((n, KV_COLS), BF16)],
        compiler_params=_params(("parallel",), 32 << 20),
        name="qk_rope",
    )(proj, proj, proj, cos, s_next, s_prev,
      q_norm_w.reshape(1, HEAD_DIM), k_norm_w.reshape(1, HEAD_DIM))


def _attn_kernel(q_ref, k_ref, v_ref, o_ref):
    n_kv = k_ref.shape[0] // ATT_TK
    items = [(h, c) for h in range(Q_PER_KV) for c in range(n_kv)]

    def scores(item):
        h, c = item
        return _dot_nt(q_ref[:, h * HEAD_DIM:(h + 1) * HEAD_DIM], k_ref[c * ATT_TK:(c + 1) * ATT_TK, :])

    s = scores(items[0])
    m = l = acc = None
    for i, (h, c) in enumerate(items):
        s_next = scores(items[i + 1]) if i + 1 < len(items) else None
        vc = v_ref[c * ATT_TK:(c + 1) * ATT_TK, :]
        mc = jnp.max(s, axis=-1, keepdims=True)
        if c == 0:
            m = mc
            p = jnp.exp(s - m)
            l = jnp.sum(p, axis=-1, keepdims=True)
            acc = jnp.dot(p.astype(BF16), vc, preferred_element_type=F32)
        else:
            m_new = jnp.maximum(m, mc)
            alpha = jnp.exp(m - m_new)
            p = jnp.exp(s - m_new)
            l = alpha * l + jnp.sum(p, axis=-1, keepdims=True)
            acc = alpha * acc + jnp.dot(p.astype(BF16), vc, preferred_element_type=F32)
            m = m_new
        if c == n_kv - 1:
            o_ref[:, h * HEAD_DIM:(h + 1) * HEAD_DIM] = (acc / l).astype(BF16)
        s = s_next


def _attention(q, k, v, tok0, batch, t):
    tq = ATT_TQ
    nq = t // tq
    row0, seq0 = tok0 // tq, tok0 // t
    gcols = Q_PER_KV * HEAD_DIM
    vmem = 4 * t * HEAD_DIM * 2 + 3 * tq * t * 4 + (8 << 20)
    q_spec = pl.BlockSpec((tq, gcols), lambda b, g, i: (row0 + b * nq + i, g))
    kv_spec = pl.BlockSpec((t, HEAD_DIM), lambda b, g, i: (seq0 + b, g))
    return pl.pallas_call(
        _attn_kernel,
        grid=(batch, N_KV_HEADS, nq),
        in_specs=[q_spec, kv_spec, kv_spec],
        out_specs=q_spec,
        out_shape=jax.ShapeDtypeStruct(q.shape, BF16),
        input_output_aliases={0: 0},
        compiler_params=_params(("parallel", "parallel", "arbitrary"), vmem),
        name="attention",
    )(q, k, v)


def _hgrn_constants():
    c = CHUNK
    t = np.arange(c)[:, None]
    u = np.arange(c)[None, :]
    sums = [(u <= t), (u > t)]
    masks = []
    for lvl in range(N_LEVELS):
        half = (c // 2) >> lvl
        bound = (t // (2 * half)) * (2 * half) + half - 1
        late = t > bound
        sums.append(np.where(late, (u > bound) & (u <= t), (u > t) & (u <= bound)))
        same = (t // (2 * half)) == (u // (2 * half))
        masks.append(same & late & (u <= bound))
    masks.append(t == u)
    sums = np.stack(sums).astype(np.float32)
    masks = np.stack(masks).astype(np.float32)
    both_s = np.stack([sums, sums[:, ::-1, ::-1]]).reshape(2, -1, c)
    both_m = np.stack([masks, masks[:, ::-1, ::-1]]).reshape(2, -1, c)
    return jnp.asarray(both_s, BF16), jnp.asarray(both_m, F32)


def _hgrn_kernel(q_ref, f_ref, v_ref, lb_ref, sums_ref, masks_ref, o_ref, st_ref, *, nchunk, nblk,
                 seq_first_blk, seq_last_blk):
    d = pl.program_id(0)
    i = pl.program_id(1)
    blk = i + d * (nblk - 1 - 2 * i)
    fresh = functools.reduce(
        jnp.logical_or,
        [jnp.logical_or(jnp.logical_and(d == 0, blk == f), jnp.logical_and(d == 1, blk == l))
         for f, l in zip(seq_first_blk, seq_last_blk)])

    @pl.when(fresh)
    def _():
        st_ref[...] = jnp.zeros_like(st_ref)

    lb = lb_ref[0]
    sums = sums_ref[0]
    c = CHUNK

    def chunk_body(j, carry):
        jj = j + d * (nchunk - 1 - 2 * j)
        rows = pl.ds(pl.multiple_of(jj * c, c), c)
        f = lb + (1.0 - lb) * jax.nn.sigmoid(f_ref[rows, :])
        k = 1.0 - f
        lf = jnp.log(f)
        qpre = q_ref[rows, :]
        q = qpre * jax.nn.sigmoid(qpre)
        vb = v_ref[rows, :].astype(BF16)
        lf_hi = lf.astype(BF16)
        lf_lo = (lf - lf_hi.astype(F32)).astype(BF16)
        e = jnp.exp(jnp.dot(sums, lf_hi, preferred_element_type=F32)
                    + jnp.dot(sums, lf_lo, preferred_element_type=F32))
        g = jnp.exp(jnp.sum(lf, axis=0, keepdims=True))
        qt = (q * e[0:c]).astype(BF16)
        kd = (k * e[c:2 * c]).astype(BF16)
        ql = [q.astype(BF16)] + [(q * e[(2 + lvl) * c:(3 + lvl) * c]).astype(BF16) for lvl in range(N_LEVELS)]
        kl = [k.astype(BF16)] + [(k * e[(2 + lvl) * c:(3 + lvl) * c]).astype(BF16) for lvl in range(N_LEVELS)]
        mk = [masks_ref[0, N_LEVELS * c:(N_LEVELS + 1) * c, :]] + [
            masks_ref[0, lvl * c:(lvl + 1) * c, :] for lvl in range(N_LEVELS)]
        heads = [slice(h * HG_HEAD_DIM, (h + 1) * HG_HEAD_DIM) for h in range(HG_HEADS)]
        scores = []
        for sl in heads:
            a = mk[0] * _dot_nt(ql[0][:, sl], kl[0][:, sl])
            for lvl in range(1, N_LEVELS + 1):
                a = a + mk[lvl] * _dot_nt(ql[lvl][:, sl], kl[lvl][:, sl])
            scores.append(a.astype(BF16))
        for h, sl in enumerate(heads):
            st = st_ref[h]
            o = _dot_nt(qt[:, sl], st.astype(BF16)) + jnp.dot(scores[h], vb[:, sl], preferred_element_type=F32)
            ut = lax.dot_general(vb[:, sl], kd[:, sl], (((0,), (0,)), ((), ())), preferred_element_type=F32)
            st_ref[h] = st * g[:, sl] + ut
            o_ref[0, rows, sl] = o
        return carry

    lax.fori_loop(0, nchunk, chunk_body, 0, unroll=True)


def _hgrn(proj, lb, sums, masks, groups):
    n = proj.shape[0]
    tb = HG_TB
    nblk = n // tb
    first_blk, last_blk, tok = [], [], 0
    for n_tok, t in groups:
        for s in range(n_tok // t):
            first_blk.append((tok + s * t) // tb)
            last_blk.append((tok + (s + 1) * t) // tb - 1)
        tok += n_tok

    def rows(d, i):
        return i + d * (nblk - 1 - 2 * i)

    in_blk = lambda col: pl.BlockSpec((tb, HG_WIDTH), lambda d, i: (rows(d, i), col(d)))
    return pl.pallas_call(
        functools.partial(_hgrn_kernel, nchunk=tb // CHUNK, nblk=nblk,
                          seq_first_blk=tuple(first_blk), seq_last_blk=tuple(last_blk)),
        grid=(2, nblk),
        in_specs=[in_blk(lambda d: 0), in_blk(lambda d: 1 + d), in_blk(lambda d: 3),
                  pl.BlockSpec((1, 1, HG_WIDTH), lambda d, i: (d, 0, 0)),
                  pl.BlockSpec((1,) + sums.shape[1:], lambda d, i: (d, 0, 0)),
                  pl.BlockSpec((1,) + masks.shape[1:], lambda d, i: (d, 0, 0))],
        out_specs=pl.BlockSpec((1, tb, HG_WIDTH), lambda d, i: (d, rows(d, i), 0)),
        out_shape=jax.ShapeDtypeStruct((2, n, HG_WIDTH), F32),
        scratch_shapes=[pltpu.VMEM((HG_HEADS, HG_HEAD_DIM, HG_HEAD_DIM), F32)],
        compiler_params=_params(("parallel", "arbitrary"), 40 << 20),
        name="hgrn2",
    )(proj, proj, proj, lb.reshape(2, 1, HG_WIDTH), sums, masks)


def _out_router_kernel(xa_ref, xb_ref, of_ref, ob_ref, g_ref, att_ref, hw_ref, wo_ref, fw_ref, rw_ref, rb_ref,
                       h_ref, xn_ref, eidx_ref, gate_ref, cnt_ref, cnt_acc, *, n_first):
    @pl.when(pl.program_id(0) == 0)
    def _():
        cnt_acc[...] = jnp.zeros_like(cnt_acc)

    o = of_ref[0] + ob_ref[0]
    parts = []
    for h in range(HG_HEADS):
        sl = slice(h * HG_HEAD_DIM, (h + 1) * HG_HEAD_DIM)
        oh = o[:, sl]
        gh = g_ref[:, sl]
        parts.append(((oh * _rms_scale(oh) * hw_ref[...]) * (gh * jax.nn.sigmoid(gh))).astype(BF16))
    oa = jnp.concatenate(parts, axis=1)
    h1 = (jnp.where(pl.program_id(0) < n_first, xa_ref[...], xb_ref[...])
          + jnp.dot(oa, wo_ref[0:HG_WIDTH, :], preferred_element_type=F32)
          + jnp.dot(att_ref[...], wo_ref[HG_WIDTH:, :], preferred_element_type=F32))
    h_ref[...] = h1
    xn = h1 * _rms_scale(h1) * fw_ref[...]
    for cb in range(xn_ref.shape[1]):
        xn_ref[:, cb, :] = xn[:, cb * LANES:(cb + 1) * LANES]

    xn_hi = xn.astype(BF16)
    xn_lo = (xn - xn_hi.astype(F32)).astype(BF16)
    logits = (jnp.dot(xn_hi, rw_ref[0], preferred_element_type=F32)
              + (jnp.dot(xn_lo, rw_ref[0], preferred_element_type=F32)
                 + jnp.dot(xn_hi, rw_ref[1], preferred_element_type=F32))) + rb_ref[...]
    tm = logits.shape[0]
    lane = lax.broadcasted_iota(jnp.int32, (tm, LANES), 1)
    work = logits
    vals, idxs = [], []
    for _ in range(TOP_K):
        m = jnp.max(work, axis=-1, keepdims=True)
        idx = jnp.min(jnp.where(work == m, lane, LANES), axis=-1, keepdims=True)
        vals.append(m)
        idxs.append(idx)
        work = jnp.where(lane == idx, -jnp.inf, work)
    exps = [jnp.exp(v - vals[0]) for v in vals]
    denom = exps[0] + exps[1] + exps[2] + exps[3]

    onehot = jnp.zeros((tm, LANES), F32)
    for idx in idxs:
        onehot = onehot + (lane == idx).astype(F32)
    eidx = jnp.zeros((tm, LANES), jnp.int32)
    gate = jnp.zeros((tm, LANES), F32)
    for kk in range(TOP_K):
        eidx = jnp.where(lane == kk, idxs[kk], eidx)
        gate = jnp.where(lane == kk, exps[kk] / denom, gate)
    eidx_ref[...] = eidx
    gate_ref[...] = gate
    cnt_acc[...] = cnt_acc[...] + jnp.sum(onehot, axis=0, keepdims=True)
    cnt_ref[...] = cnt_acc[...]


def _out_router(xa, xb, o_dirs, proj, att, hgrn_norm_w, w_out_bf16, ffn_norm_w, router_w, router_b):
    d = xa.shape[1]
    n = xa.shape[0] + xb.shape[0]
    tm = OUT_TM
    rw = jnp.zeros((d, LANES), F32).at[:, :N_EXPERTS].set(router_w.astype(F32))
    rw_hi = rw.astype(BF16)
    rw = jnp.stack([rw_hi, (rw - rw_hi.astype(F32)).astype(BF16)])
    rb = jnp.full((1, LANES), -1e30, F32).at[0, :N_EXPERTS].set(router_b.astype(F32))
    g_blk = 4
    row = lambda i: (i, 0)
    const = lambda i: (0, 0)
    vmem = 2 * d * d * 2 + 12 * tm * d * 4 + (12 << 20)
    return pl.pallas_call(
        functools.partial(_out_router_kernel, n_first=xa.shape[0] // tm),
        grid=(n // tm,),
        in_specs=_pair_specs((tm, d), xa.shape[0] // tm) + [
                  pl.BlockSpec((1, tm, HG_WIDTH), lambda i: (0, i, 0)),
                  pl.BlockSpec((1, tm, HG_WIDTH), lambda i: (1, i, 0)),
                  pl.BlockSpec((tm, HG_WIDTH), lambda i: (i, g_blk)),
                  pl.BlockSpec((tm, Q_COLS), row),
                  pl.BlockSpec((1, HG_HEAD_DIM), const),
                  pl.BlockSpec((d, d), const),
                  pl.BlockSpec((1, d), const),
                  pl.BlockSpec((2, d, LANES), lambda i: (0, 0, 0)),
                  pl.BlockSpec((1, LANES), const)],
        out_specs=[pl.BlockSpec((tm, d), row), pl.BlockSpec((tm, d // LANES, LANES), lambda i: (i, 0, 0)),
                   pl.BlockSpec((tm, LANES), row), pl.BlockSpec((tm, LANES), row),
                   pl.BlockSpec((1, LANES), const)],
        out_shape=[jax.ShapeDtypeStruct((n, d), F32), jax.ShapeDtypeStruct((n, d // LANES, LANES), F32),
                   jax.ShapeDtypeStruct((n, LANES), jnp.int32),
                   jax.ShapeDtypeStruct((n, LANES), F32), jax.ShapeDtypeStruct((1, LANES), F32)],
        scratch_shapes=[pltpu.VMEM((1, LANES), F32)],
        compiler_params=_params(("arbitrary",), vmem),
        name="out_router",
    )(xa, xb, o_dirs, o_dirs, proj, att, hgrn_norm_w.reshape(1, HG_HEAD_DIM), w_out_bf16,
      ffn_norm_w.reshape(1, d), rw, rb)


ROW_PITCH = 20
GATHER_UNROLL = 8


def _start_row_gather(idx_ref, n_rows, src_hbm, buf, sem):
    slabs = src_hbm.shape[1]

    def issue(g, carry):
        base = pl.multiple_of(g * GATHER_UNROLL, GATHER_UNROLL)
        for u in range(GATHER_UNROLL):
            r = base + u
            pltpu.make_async_copy(src_hbm.at[idx_ref[0, 0, r]], buf.at[pl.ds(r * ROW_PITCH, slabs), :], sem).start()
        return carry
    lax.fori_loop(0, n_rows // GATHER_UNROLL, issue, 0)


def _wait_row_gather(n_rows, slabs, buf, sem):
    done = buf.at[pl.ds(0, n_rows * slabs), :]
    pltpu.make_async_copy(done, done, sem).wait()


def _gathered_cols(buf, row0, n_rows, cb):
    return buf[pl.ds(row0 * ROW_PITCH + cb, n_rows, stride=ROW_PITCH), :]


def _start_row_scatter(idx_ref, src, dst_hbm, sem):
    def issue(g, carry):
        base = pl.multiple_of(g * GATHER_UNROLL, GATHER_UNROLL)
        for u in range(GATHER_UNROLL):
            r = base + u
            pltpu.make_async_copy(src.at[pl.ds(r, 1), :], dst_hbm.at[pl.ds(idx_ref[0, 0, r], 1), :], sem).start()
        return carry
    lax.fori_loop(0, src.shape[0] // GATHER_UNROLL, issue, 0)


def _wait_row_scatter(src, sem):
    pltpu.make_async_copy(src, src, sem).wait()


def _moe_kernel(vb_ref, ve_ref, vlo_ref, vhi_ref, nvis_ref,
                tok_ref, tok_next_ref, orow_ref, xn_hbm, wg_ref, wl_ref, bg_ref, bl_ref, wd_ref, bd_ref,
                y_hbm, raw, xb, acc, sem_in, sem_out, *, nblk):
    w = pl.program_id(0)
    j = pl.program_id(1)
    nf = pl.num_programs(1)
    r = xb.shape[0]
    slabs = xn_hbm.shape[1]
    valid = w < nvis_ref[0]
    blk, lo, hi = vb_ref[w], vlo_ref[w], vhi_ref[w]
    slot = lax.rem(blk, 2)
    acc_blk = acc.at[slot]

    @pl.when(jnp.logical_and(valid, jnp.logical_and(lo == 0, j == 0)))
    def _():
        @pl.when(w == 0)
        def _():
            _start_row_gather(tok_ref, r, xn_hbm, raw, sem_in)
        _wait_row_gather(r, slabs, raw, sem_in)
        for cb in range(slabs):
            xb[:, cb * LANES:(cb + 1) * LANES] = _gathered_cols(raw, 0, r, cb).astype(BF16)

        @pl.when(blk + 1 < nblk)
        def _():
            _start_row_gather(tok_next_ref, r, xn_hbm, raw, sem_in)

        @pl.when(blk >= 2)
        def _():
            _wait_row_scatter(acc_blk, sem_out.at[slot])
        acc_blk[...] = jnp.zeros(acc_blk.shape, F32)

    @pl.when(valid)
    def _():
        sub_lo = lax.shift_right_logical(lo, MOE_SUB.bit_length() - 1)
        sub_hi = lax.shift_right_logical(hi + (MOE_SUB - 1), MOE_SUB.bit_length() - 1)
        n_sub = r // MOE_SUB
        for s0 in range(n_sub):
            for s1 in range(s0 + 1, n_sub + 1):
                rows = slice(s0 * MOE_SUB, s1 * MOE_SUB)

                @pl.when(jnp.logical_and(sub_lo == s0, sub_hi == s1))
                def _(rows=rows):
                    m = rows.stop - rows.start
                    xs = xb[rows, :]
                    gl = jnp.dot(xs, wg_ref[0].astype(BF16), preferred_element_type=F32) + bg_ref[0]
                    lin = jnp.dot(xs, wl_ref[0].astype(BF16), preferred_element_type=F32) + bl_ref[0]
                    gl = jnp.minimum(gl, SWIGLU_LIMIT)
                    lin = jnp.clip(lin, -SWIGLU_LIMIT, SWIGLU_LIMIT)
                    act = (lin + 1.0) * (gl * jax.nn.sigmoid(SWIGLU_ALPHA * gl))
                    rid = rows.start + lax.broadcasted_iota(jnp.int32, (m, 1), 0)
                    mine = jnp.logical_and(rid >= lo, rid < hi).astype(F32)
                    acc_blk[rows, :] += jnp.dot((act * mine).astype(BF16), wd_ref[0].astype(BF16),
                                                preferred_element_type=F32)

                    @pl.when(j == 0)
                    def _():
                        acc_blk[rows, :] += mine * bd_ref[0]

        @pl.when(jnp.logical_and(hi == r, j == nf - 1))
        def _():
            _start_row_scatter(orow_ref, acc_blk, y_hbm, sem_out.at[slot])

    @pl.when(jnp.logical_and(w == pl.num_programs(0) - 1, j == nf - 1))
    def _():
        for s in range(min(2, nblk)):
            _wait_row_scatter(acc.at[s], sem_out.at[s])


def _moe_experts(vis_blk, vis_exp, vis_lo, vis_hi, n_vis, slot_tok, out_row, xn3,
                 w_gate_up, b_gate_up, w_down, b_down):
    r, tf = MOE_R, MOE_TF
    n, slabs, _ = xn3.shape
    d = slabs * LANES
    nblk = slot_tok.shape[0]
    nf = D_FF // tf

    def tile(w, j, nv):
        return jnp.where(w < nv[0], j, nf - 1)

    idx_spec = lambda f: pl.BlockSpec((1, 1, r), lambda w, j, vb, ve, lo, hi, nv: (f(vb[w]), 0, 0),
                                      memory_space=pltpu.SMEM)
    vmem = (r * ROW_PITCH * LANES * 4 + r * d * 2 + 2 * r * d * 4 + 2 * 3 * d * tf * 4 + 3 * d * tf * 2
            + 3 * MOE_SUB * d * 4 + (6 << 20))
    return pl.pallas_call(
        functools.partial(_moe_kernel, nblk=nblk),
        grid_spec=pltpu.PrefetchScalarGridSpec(
            num_scalar_prefetch=5,
            grid=(vis_blk.shape[0], nf),
            in_specs=[idx_spec(lambda b: b), idx_spec(lambda b: jnp.minimum(b + 1, nblk - 1)), idx_spec(lambda b: b),
                      pl.BlockSpec(memory_space=pl.ANY),
                      pl.BlockSpec((1, d, tf), lambda w, j, vb, ve, lo, hi, nv: (ve[w], 0, tile(w, j, nv))),
                      pl.BlockSpec((1, d, tf), lambda w, j, vb, ve, lo, hi, nv: (ve[w], 0, nf + tile(w, j, nv))),
                      pl.BlockSpec((1, 1, tf), lambda w, j, vb, ve, lo, hi, nv: (ve[w], 0, tile(w, j, nv))),
                      pl.BlockSpec((1, 1, tf), lambda w, j, vb, ve, lo, hi, nv: (ve[w], 0, nf + tile(w, j, nv))),
                      pl.BlockSpec((1, tf, d), lambda w, j, vb, ve, lo, hi, nv: (ve[w], tile(w, j, nv), 0)),
                      pl.BlockSpec((1, 1, d), lambda w, j, vb, ve, lo, hi, nv: (ve[w], 0, 0))],
            out_specs=pl.BlockSpec(memory_space=pl.ANY),
            scratch_shapes=[pltpu.VMEM((r * ROW_PITCH, LANES), F32),
                            pltpu.VMEM((r, d), BF16),
                            pltpu.VMEM((2, r, d), F32),
                            pltpu.SemaphoreType.DMA, pltpu.SemaphoreType.DMA((2,))]),
        out_shape=jax.ShapeDtypeStruct((TOP_K * n, d), F32),
        compiler_params=_params(("arbitrary", "arbitrary"), vmem),
        name="moe_experts",
    )(vis_blk, vis_exp, vis_lo, vis_hi, n_vis, slot_tok, slot_tok, out_row, xn3,
      w_gate_up, w_gate_up, b_gate_up.reshape(N_EXPERTS, 1, 2 * D_FF),
      b_gate_up.reshape(N_EXPERTS, 1, 2 * D_FF), w_down, b_down.reshape(N_EXPERTS, 1, d))


def _final_kernel(h_ref, gate_ref, y0_ref, y1_ref, y2_ref, y3_ref, pa_ref, pb_ref, pp_ref, pnw_ref, pg_ref,
                  oa_ref, ob_ref, *, n_first):
    first = pl.program_id(0) < n_first
    h = h_ref[...]
    for kk, y_ref in enumerate((y0_ref, y1_ref, y2_ref, y3_ref)):
        h = h + gate_ref[:, kk:kk + 1] * y_ref[...]
    gate = jax.nn.sigmoid(jnp.dot(h.astype(BF16), pg_ref[...], preferred_element_type=F32))
    p = jnp.where(first, pa_ref[...], pb_ref[...])
    pe = jnp.dot(p.astype(BF16), pp_ref[...], preferred_element_type=F32)
    pe = pe * _rms_scale(pe) * pnw_ref[...]
    out = h + gate * pe

    @pl.when(first)
    def _():
        oa_ref[...] = out

    @pl.when(jnp.logical_not(first))
    def _():
        ob_ref[...] = out


def _final(y4, h1, gates, pa, pb, ple_proj_bf16, ple_norm_w, ple_gate_bf16):
    n, d = h1.shape
    tm = FIN_TM
    nt = n // tm
    n_first = pa.shape[0] // tm
    row = lambda i: (i, 0)
    const = lambda i: (0, 0)
    vmem = 2 * d * d * 2 + 20 * tm * d * 4 + (8 << 20)
    return pl.pallas_call(
        functools.partial(_final_kernel, n_first=n_first),
        grid=(nt,),
        in_specs=[pl.BlockSpec((tm, d), row),
                  pl.BlockSpec((tm, LANES), row)]
        + [pl.BlockSpec((tm, d), lambda i, kk=kk: (kk * nt + i, 0)) for kk in range(TOP_K)]
        + _pair_specs((tm, PLE_DIM), n_first)
        + [pl.BlockSpec((PLE_DIM, d), const),
           pl.BlockSpec((1, d), const),
           pl.BlockSpec((d, d), const)],
        out_specs=_pair_specs((tm, d), n_first),
        out_shape=[jax.ShapeDtypeStruct((pa.shape[0], d), F32), jax.ShapeDtypeStruct((pb.shape[0], d), F32)],
        compiler_params=_params(("arbitrary",), vmem),
        name="combine_final",
    )(h1, gates, y4, y4, y4, y4, pa, pb, ple_proj_bf16, ple_norm_w.reshape(1, d), ple_gate_bf16)


def _routing(eidx, counts, n_tokens):
    r = MOE_R
    nblk = (n_tokens * TOP_K) // r
    nvis_max = nblk + N_EXPERTS - 1
    order = jnp.argsort(eidx[:, :TOP_K].reshape(-1), stable=True).astype(jnp.int32)
    slot_tok = order // TOP_K
    out_row = (order % TOP_K) * n_tokens + slot_tok
    cnt = counts[0, :N_EXPERTS].astype(jnp.int32)
    ends = jnp.cumsum(cnt)
    starts = ends - cnt
    first_blk = starts // r
    n_blk = jnp.where(cnt > 0, (ends - 1) // r - first_blk + 1, 0)
    vis_end = jnp.cumsum(n_blk)
    n_vis = vis_end[-1]
    w = jnp.arange(nvis_max, dtype=jnp.int32)
    valid = w < n_vis
    e = jnp.minimum(jnp.sum((vis_end[None, :] <= w[:, None]).astype(jnp.int32), axis=1), N_EXPERTS - 1)
    e = jnp.where(valid, e, jnp.max(jnp.where(valid, e, 0)))
    blk = jnp.where(valid, first_blk[e] + w - (vis_end[e] - n_blk[e]), nblk - 1)
    lo = jnp.where(valid, jnp.clip(starts[e] - blk * r, 0, r), 0)
    hi = jnp.where(valid, jnp.clip(ends[e] - blk * r, 0, r), 0)
    i32 = lambda a: a.astype(jnp.int32)
    return (i32(blk), i32(e), i32(lo), i32(hi), i32(n_vis).reshape(1),
            slot_tok.reshape(nblk, 1, r), out_row.reshape(nblk, 1, r))


def _layer(xa, xb, pa, pb, groups, lb, mix_norm_w, w_in, hgrn_norm_w, q_norm_w, k_norm_w, w_out, ffn_norm_w,
           router_w, router_b, w_gate_up, b_gate_up, w_down, b_down, ple_proj, ple_norm_w, ple_gate):
    n = xa.shape[0] + xb.shape[0]
    proj = _in_proj(xa, xb, mix_norm_w, w_in.astype(BF16))
    att, k, v = _qk_rope(proj, q_norm_w, k_norm_w, groups)
    sums, masks = _hgrn_constants()
    o_dirs = _hgrn(proj, lb, sums, masks, groups)
    tok0 = 0
    for n_tok, t in groups:
        att = _attention(att, k, v, tok0, n_tok // t, t)
        tok0 += n_tok
    h1, xn3, eidx, gates, counts = _out_router(
        xa, xb, o_dirs, proj, att, hgrn_norm_w, w_out.astype(BF16), ffn_norm_w, router_w, router_b)
    y4 = _moe_experts(*_routing(eidx, counts, n), xn3, w_gate_up, b_gate_up, w_down, b_down)
    return _final(y4, h1, gates, pa, pb, ple_proj.astype(BF16), ple_norm_w, ple_gate.astype(BF16))


def kernel(x_prompt, x_sample, p_prompt, p_sample, mix_norm_w, w_in, hgrn_lb, hgrn_norm_w, q_norm_w, k_norm_w, w_out, ffn_norm_w, router_w, router_b, w_gate_up, b_gate_up, w_down, b_down, ple_proj, ple_norm_w, ple_gate):
    depth = w_in.shape[0]
    groups = ((x_prompt.shape[0] * x_prompt.shape[1], x_prompt.shape[1]),
              (x_sample.shape[0] * x_sample.shape[1], x_sample.shape[1]))
    d = x_prompt.shape[-1]
    ha, hb = x_prompt.reshape(-1, d), x_sample.reshape(-1, d)
    lb_all = jnp.cumsum(jax.nn.softmax(hgrn_lb.astype(F32), axis=0), axis=0)
    for i in range(depth):
        ha, hb = _layer(ha, hb, p_prompt[i].reshape(-1, PLE_DIM), p_sample[i].reshape(-1, PLE_DIM), groups,
                        lb_all[i], mix_norm_w[i], w_in[i], hgrn_norm_w[i], q_norm_w[i], k_norm_w[i],
                        w_out[i], ffn_norm_w[i], router_w[i], router_b[i], w_gate_up[i], b_gate_up[i],
                        w_down[i], b_down[i], ple_proj[i], ple_norm_w[i], ple_gate[i])
    return ha.reshape(x_prompt.shape), hb.reshape(x_sample.shape)
```

```python
import functools

import numpy as np
import jax
import jax.numpy as jnp
from jax import lax
from jax.experimental import pallas as pl
from jax.experimental.pallas import tpu as pltpu

F32 = jnp.float32
BF16 = jnp.bfloat16

D_MODEL = 2048
GRID_W = 64
HG_WIDTH = 1024
HG_HEAD_DIM = 128
HG_HEADS = HG_WIDTH // HG_HEAD_DIM
CHUNK = 64
HEAD_DIM = 128
N_Q_HEADS = 8
N_KV_HEADS = 2
Q_PER_KV = N_Q_HEADS // N_KV_HEADS
ROPE_THETA = 10000.0
ROPE_AXIS_PAIRS = HEAD_DIM // 4
N_EXPERTS = 32
TOP_K = 4
D_FF = D_MODEL
SWIGLU_LIMIT = 7.0
SWIGLU_ALPHA = 1.702
PLE_DIM = 256
NORM_EPS = 1e-6
Q_COLS = N_Q_HEADS * HEAD_DIM
KV_COLS = N_KV_HEADS * HEAD_DIM
IN_COLS = 5 * HG_WIDTH + Q_COLS + 2 * KV_COLS

LANES = 128
V7X_VMEM_BUDGET_BYTES = 56 * 1024 * 1024

IN_TM, IN_TN = 1024, 512
QK_TM = 512
ATT_TQ = 256
ATT_TK = 1024
HG_TB = 512
HG_UNROLL = 4
OUT_TM = 256
MOE_R = 1024
MOE_SUB = 256
MOE_TF = 256
FIN_TM = 256
ROW_CHUNK = 128
N_LEVELS = 6


def _params(sem, vmem_bytes):
    return pltpu.CompilerParams(dimension_semantics=sem,
                                vmem_limit_bytes=min(int(vmem_bytes), V7X_VMEM_BUDGET_BYTES))


def _rms_scale(x):
    return lax.rsqrt(jnp.mean(x * x, axis=-1, keepdims=True) + NORM_EPS)


def _dot_nt(a, b):
    return lax.dot_general(a, b, (((1,), (1,)), ((), ())), preferred_element_type=F32)


def _pair_specs(block, n_first):
    zeros = (0,) * (len(block) - 1)
    return [pl.BlockSpec(block, lambda i, *_: (jnp.minimum(i, n_first - 1),) + zeros),
            pl.BlockSpec(block, lambda i, *_: (jnp.maximum(i - n_first, 0),) + zeros)]


def _in_proj_kernel(xa_ref, xb_ref, nw_ref, w_ref, o_ref, xn_ref, *, n_first):
    @pl.when(pl.program_id(1) == 0)
    def _():
        first = pl.program_id(0) < n_first

        def body(c, carry):
            rows = pl.ds(pl.multiple_of(c * ROW_CHUNK, ROW_CHUNK), ROW_CHUNK)
            x = jnp.where(first, xa_ref[rows, :], xb_ref[rows, :])
            xn_ref[rows, :] = (x * _rms_scale(x) * nw_ref[...]).astype(BF16)
            return carry
        lax.fori_loop(0, xa_ref.shape[0] // ROW_CHUNK, body, 0)

    o_ref[...] = jnp.dot(xn_ref[...], w_ref[...], preferred_element_type=F32)


def _in_proj(xa, xb, norm_w, w_bf16):
    d = xa.shape[1]
    n = xa.shape[0] + xb.shape[0]
    cols = w_bf16.shape[1]
    tm, tn = IN_TM, IN_TN
    vmem = 4 * tm * d * 4 + tm * d * 2 + 2 * d * tn * 2 + 2 * tm * tn * 4 + (8 << 20)
    return pl.pallas_call(
        functools.partial(_in_proj_kernel, n_first=xa.shape[0] // tm),
        grid=(n // tm, cols // tn),
        in_specs=_pair_specs((tm, d), xa.shape[0] // tm) + [
            pl.BlockSpec((1, d), lambda i, j: (0, 0)),
            pl.BlockSpec((d, tn), lambda i, j: (0, j))],
        out_specs=pl.BlockSpec((tm, tn), lambda i, j: (i, j)),
        out_shape=jax.ShapeDtypeStruct((n, cols), F32),
        scratch_shapes=[pltpu.VMEM((tm, d), BF16)],
        compiler_params=_params(("parallel", "arbitrary"), vmem),
        name="in_proj",
    )(xa, xb, norm_w.reshape(1, d), w_bf16)


def _rope_tables(t_max):
    rows = t_max // GRID_W
    row = jnp.repeat(jnp.arange(rows, dtype=F32), GRID_W)
    col = jnp.tile(jnp.arange(GRID_W, dtype=F32), rows)
    inv_freq = ROPE_THETA ** (-jnp.arange(ROPE_AXIS_PAIRS, dtype=F32) / ROPE_AXIS_PAIRS)
    ang = jnp.concatenate([row[:, None] * inv_freq, col[:, None] * inv_freq], axis=-1)
    cos = jnp.repeat(jnp.cos(ang), 2, axis=-1)
    sin = jnp.repeat(jnp.sin(ang), 2, axis=-1)
    even = (jnp.arange(HEAD_DIM) % 2 == 0)[None, :]
    s_next = jnp.where(even, -sin, 0.0)
    s_prev = jnp.where(even, 0.0, sin)
    return cos, s_next, s_prev


def _qk_rope_kernel(q_ref, k_ref, v_ref, c_ref, sn_ref, sp_ref, qw_ref, kw_ref,
                    qo_ref, ko_ref, vo_ref):
    c, sn, sp = c_ref[...], sn_ref[...], sp_ref[...]

    def norm_rope(x, w):
        y = x * _rms_scale(x) * w
        return y * c + pltpu.roll(y, HEAD_DIM - 1, 1) * sn + pltpu.roll(y, 1, 1) * sp

    scale = HEAD_DIM ** -0.5
    for h in range(N_Q_HEADS):
        sl = slice(h * HEAD_DIM, (h + 1) * HEAD_DIM)
        qo_ref[:, sl] = (norm_rope(q_ref[:, sl], qw_ref[...]) * scale).astype(BF16)
    for h in range(N_KV_HEADS):
        sl = slice(h * HEAD_DIM, (h + 1) * HEAD_DIM)
        ko_ref[:, sl] = norm_rope(k_ref[:, sl], kw_ref[...]).astype(BF16)
    vo_ref[...] = v_ref[...].astype(BF16)


def _qk_rope(proj, q_norm_w, k_norm_w, groups):
    n = proj.shape[0]
    tm = QK_TM
    t_max = max(t for _, t in groups)
    cos, s_next, s_prev = _rope_tables(t_max)

    def pos_block(i):
        blk = jnp.int32(0)
        start = 0
        for n_tok, t in groups:
            first = start // tm
            blk = jnp.where(i >= first, (i - first) % (t // tm), blk)
            start += n_tok
        return blk

    tab_spec = pl.BlockSpec((tm, HEAD_DIM), lambda i: (pos_block(i), 0))
    q_blk, k_blk, v_blk = (5 * HG_WIDTH) // Q_COLS, (5 * HG_WIDTH + Q_COLS) // KV_COLS, \
        (5 * HG_WIDTH + Q_COLS + KV_COLS) // KV_COLS
    return pl.pallas_call(
        _qk_rope_kernel,
        grid=(n // tm,),
        in_specs=[pl.BlockSpec((tm, Q_COLS), lambda i: (i, q_blk)),
                  pl.BlockSpec((tm, KV_COLS), lambda i: (i, k_blk)),
                  pl.BlockSpec((tm, KV_COLS), lambda i: (i, v_blk)),
                  tab_spec, tab_spec, tab_spec,
                  pl.BlockSpec((1, HEAD_DIM), lambda i: (0, 0)),
                  pl.BlockSpec((1, HEAD_DIM), lambda i: (0, 0))],
        out_specs=[pl.BlockSpec((tm, Q_COLS), lambda i: (i, 0)),
                   pl.BlockSpec((tm, KV_COLS), lambda i: (i, 0)),
                   pl.BlockSpec((tm, KV_COLS), lambda i: (i, 0))],
        out_shape=[jax.ShapeDtypeStruct((n, Q_COLS), BF16),
                   jax.ShapeDtypeStruct((n, KV_COLS), BF16),
                   jax.ShapeDtypeStruct((n, KV_COLS), BF16)],
        compiler_params=_params(("parallel",), 32 << 20),
        name="qk_rope",
    )(proj, proj, proj, cos, s_next, s_prev,
      q_norm_w.reshape(1, HEAD_DIM), k_norm_w.reshape(1, HEAD_DIM))


def _attn_kernel(q_ref, k_ref, v_ref, o_ref):
    n_kv = k_ref.shape[0] // ATT_TK
    items = [(h, c) for h in range(Q_PER_KV) for c in range(n_kv)]

    def scores(item):
        h, c = item
        return _dot_nt(q_ref[:, h * HEAD_DIM:(h + 1) * HEAD_DIM], k_ref[c * ATT_TK:(c + 1) * ATT_TK, :])

    s = scores(items[0])
    m = l = acc = None
    for i, (h, c) in enumerate(items):
        s_next = scores(items[i + 1]) if i + 1 < len(items) else None
        vc = v_ref[c * ATT_TK:(c + 1) * ATT_TK, :]
        mc = jnp.max(s, axis=-1, keepdims=True)
        if c == 0:
            m = mc
            p = jnp.exp(s - m)
            l = jnp.sum(p, axis=-1, keepdims=True)
            acc = jnp.dot(p.astype(BF16), vc, preferred_element_type=F32)
        else:
            m_new = jnp.maximum(m, mc)
            alpha = jnp.exp(m - m_new)
            p = jnp.exp(s - m_new)
            l = alpha * l + jnp.sum(p, axis=-1, keepdims=True)
            acc = alpha * acc + jnp.dot(p.astype(BF16), vc, preferred_element_type=F32)
            m = m_new
        if c == n_kv - 1:
            o_ref[:, h * HEAD_DIM:(h + 1) * HEAD_DIM] = (acc / l).astype(BF16)
        s = s_next


def _attention(q, k, v, tok0, batch, t):
    tq = ATT_TQ
    nq = t // tq
    row0, seq0 = tok0 // tq, tok0 // t
    gcols = Q_PER_KV * HEAD_DIM
    vmem = 4 * t * HEAD_DIM * 2 + 3 * tq * t * 4 + (8 << 20)
    q_spec = pl.BlockSpec((tq, gcols), lambda b, g, i: (row0 + b * nq + i, g))
    kv_spec = pl.BlockSpec((t, HEAD_DIM), lambda b, g, i: (seq0 + b, g))
    return pl.pallas_call(
        _attn_kernel,
        grid=(batch, N_KV_HEADS, nq),
        in_specs=[q_spec, kv_spec, kv_spec],
        out_specs=q_spec,
        out_shape=jax.ShapeDtypeStruct(q.shape, BF16),
        input_output_aliases={0: 0},
        compiler_params=_params(("parallel", "parallel", "arbitrary"), vmem),
        name="attention",
    )(q, k, v)


def _hgrn_constants():
    c = CHUNK
    t = np.arange(c)[:, None]
    u = np.arange(c)[None, :]
    sums = [(u <= t), (u > t)]
    masks = []
    for lvl in range(N_LEVELS):
        half = (c // 2) >> lvl
        bound = (t // (2 * half)) * (2 * half) + half - 1
        late = t > bound
        sums.append(np.where(late, (u > bound) & (u <= t), (u > t) & (u <= bound)))
        same = (t // (2 * half)) == (u // (2 * half))
        masks.append(same & late & (u <= bound))
    masks.append(t == u)
    sums = np.stack(sums).astype(np.float32)
    masks = np.stack(masks).astype(np.float32)
    both_s = np.stack([sums, sums[:, ::-1, ::-1]]).reshape(2, -1, c)
    both_m = np.stack([masks, masks[:, ::-1, ::-1]]).reshape(2, -1, c)
    return jnp.asarray(both_s, BF16), jnp.asarray(both_m, F32)


def _hgrn_kernel(q_ref, f_ref, v_ref, lb_ref, sums_ref, masks_ref, o_ref, st_ref, *, nchunk, nblk,
                 seq_first_blk, seq_last_blk):
    d = pl.program_id(0)
    i = pl.program_id(1)
    blk = i + d * (nblk - 1 - 2 * i)
    fresh = functools.reduce(
        jnp.logical_or,
        [jnp.logical_or(jnp.logical_and(d == 0, blk == f), jnp.logical_and(d == 1, blk == l))
         for f, l in zip(seq_first_blk, seq_last_blk)])

    @pl.when(fresh)
    def _():
        st_ref[...] = jnp.zeros_like(st_ref)

    lb = lb_ref[0]
    sums = sums_ref[0]
    c = CHUNK

    def chunk_body(j, carry):
        jj = j + d * (nchunk - 1 - 2 * j)
        rows = pl.ds(pl.multiple_of(jj * c, c), c)
        f = lb + (1.0 - lb) * jax.nn.sigmoid(f_ref[rows, :])
        k = 1.0 - f
        lf = jnp.log(f)
        qpre = q_ref[rows, :]
        q = qpre * jax.nn.sigmoid(qpre)
        vb = v_ref[rows, :].astype(BF16)
        lf_hi = lf.astype(BF16)
        lf_lo = (lf - lf_hi.astype(F32)).astype(BF16)
        e = jnp.exp(jnp.dot(sums, lf_hi, preferred_element_type=F32)
                    + jnp.dot(sums, lf_lo, preferred_element_type=F32))
        g = jnp.exp(jnp.sum(lf, axis=0, keepdims=True))
        qt = (q * e[0:c]).astype(BF16)
        kd = (k * e[c:2 * c]).astype(BF16)
        ql = [q.astype(BF16)] + [(q * e[(2 + lvl) * c:(3 + lvl) * c]).astype(BF16) for lvl in range(N_LEVELS)]
        kl = [k.astype(BF16)] + [(k * e[(2 + lvl) * c:(3 + lvl) * c]).astype(BF16) for lvl in range(N_LEVELS)]
        mk = [masks_ref[0, N_LEVELS * c:(N_LEVELS + 1) * c, :]] + [
            masks_ref[0, lvl * c:(lvl + 1) * c, :] for lvl in range(N_LEVELS)]
        heads = [slice(h * HG_HEAD_DIM, (h + 1) * HG_HEAD_DIM) for h in range(HG_HEADS)]
        scores = []
        for sl in heads:
            a = mk[0] * _dot_nt(ql[0][:, sl], kl[0][:, sl])
            for lvl in range(1, N_LEVELS + 1):
                a = a + mk[lvl] * _dot_nt(ql[lvl][:, sl], kl[lvl][:, sl])
            scores.append(a.astype(BF16))
        for h, sl in enumerate(heads):
            st = st_ref[h]
            o = _dot_nt(qt[:, sl], st.astype(BF16)) + jnp.dot(scores[h], vb[:, sl], preferred_element_type=F32)
            ut = lax.dot_general(vb[:, sl], kd[:, sl], (((0,), (0,)), ((), ())), preferred_element_type=F32)
            st_ref[h] = st * g[:, sl] + ut
            o_ref[0, rows, sl] = o
        return carry

    lax.fori_loop(0, nchunk, chunk_body, 0, unroll=HG_UNROLL)


def _hgrn(proj, lb, sums, masks, groups):
    n = proj.shape[0]
    tb = HG_TB
    nblk = n // tb
    first_blk, last_blk, tok = [], [], 0
    for n_tok, t in groups:
        for s in range(n_tok // t):
            first_blk.append((tok + s * t) // tb)
            last_blk.append((tok + (s + 1) * t) // tb - 1)
        tok += n_tok

    def rows(d, i):
        return i + d * (nblk - 1 - 2 * i)

    in_blk = lambda col: pl.BlockSpec((tb, HG_WIDTH), lambda d, i: (rows(d, i), col(d)))
    return pl.pallas_call(
        functools.partial(_hgrn_kernel, nchunk=tb // CHUNK, nblk=nblk,
                          seq_first_blk=tuple(first_blk), seq_last_blk=tuple(last_blk)),
        grid=(2, nblk),
        in_specs=[in_blk(lambda d: 0), in_blk(lambda d: 1 + d), in_blk(lambda d: 3),
                  pl.BlockSpec((1, 1, HG_WIDTH), lambda d, i: (d, 0, 0)),
                  pl.BlockSpec((1,) + sums.shape[1:], lambda d, i: (d, 0, 0)),
                  pl.BlockSpec((1,) + masks.shape[1:], lambda d, i: (d, 0, 0))],
        out_specs=pl.BlockSpec((1, tb, HG_WIDTH), lambda d, i: (d, rows(d, i), 0)),
        out_shape=jax.ShapeDtypeStruct((2, n, HG_WIDTH), F32),
        scratch_shapes=[pltpu.VMEM((HG_HEADS, HG_HEAD_DIM, HG_HEAD_DIM), F32)],
        compiler_params=_params(("parallel", "arbitrary"), 40 << 20),
        name="hgrn2",
    )(proj, proj, proj, lb.reshape(2, 1, HG_WIDTH), sums, masks)


def _out_router_kernel(xa_ref, xb_ref, of_ref, ob_ref, g_ref, att_ref, hw_ref, wo_ref, fw_ref, rw_ref, rb_ref,
                       h_ref, xn_ref, eidx_ref, gate_ref, cnt_ref, cnt_acc, *, n_first):
    @pl.when(pl.program_id(0) == 0)
    def _():
        cnt_acc[...] = jnp.zeros_like(cnt_acc)

    o = of_ref[0] + ob_ref[0]
    parts = []
    for h in range(HG_HEADS):
        sl = slice(h * HG_HEAD_DIM, (h + 1) * HG_HEAD_DIM)
        oh = o[:, sl]
        gh = g_ref[:, sl]
        parts.append(((oh * _rms_scale(oh) * hw_ref[...]) * (gh * jax.nn.sigmoid(gh))).astype(BF16))
    oa = jnp.concatenate(parts, axis=1)
    h1 = (jnp.where(pl.program_id(0) < n_first, xa_ref[...], xb_ref[...])
          + jnp.dot(oa, wo_ref[0:HG_WIDTH, :], preferred_element_type=F32)
          + jnp.dot(att_ref[...], wo_ref[HG_WIDTH:, :], preferred_element_type=F32))
    h_ref[...] = h1
    xn = h1 * _rms_scale(h1) * fw_ref[...]
    for cb in range(xn_ref.shape[1]):
        xn_ref[:, cb, :] = xn[:, cb * LANES:(cb + 1) * LANES]

    xn_hi = xn.astype(BF16)
    xn_lo = (xn - xn_hi.astype(F32)).astype(BF16)
    logits = (jnp.dot(xn_hi, rw_ref[0], preferred_element_type=F32)
              + (jnp.dot(xn_lo, rw_ref[0], preferred_element_type=F32)
                 + jnp.dot(xn_hi, rw_ref[1], preferred_element_type=F32))) + rb_ref[...]
    tm = logits.shape[0]
    lane = lax.broadcasted_iota(jnp.int32, (tm, LANES), 1)
    work = logits
    vals, idxs = [], []
    for _ in range(TOP_K):
        m = jnp.max(work, axis=-1, keepdims=True)
        idx = jnp.min(jnp.where(work == m, lane, LANES), axis=-1, keepdims=True)
        vals.append(m)
        idxs.append(idx)
        work = jnp.where(lane == idx, -jnp.inf, work)
    exps = [jnp.exp(v - vals[0]) for v in vals]
    denom = exps[0] + exps[1] + exps[2] + exps[3]

    onehot = jnp.zeros((tm, LANES), F32)
    for idx in idxs:
        onehot = onehot + (lane == idx).astype(F32)
    eidx = jnp.zeros((tm, LANES), jnp.int32)
    gate = jnp.zeros((tm, LANES), F32)
    for kk in range(TOP_K):
        eidx = jnp.where(lane == kk, idxs[kk], eidx)
        gate = jnp.where(lane == kk, exps[kk] / denom, gate)
    eidx_ref[...] = eidx
    gate_ref[...] = gate
    cnt_acc[...] = cnt_acc[...] + jnp.sum(onehot, axis=0, keepdims=True)
    cnt_ref[...] = cnt_acc[...]


def _out_router(xa, xb, o_dirs, proj, att, hgrn_norm_w, w_out_bf16, ffn_norm_w, router_w, router_b):
    d = xa.shape[1]
    n = xa.shape[0] + xb.shape[0]
    tm = OUT_TM
    rw = jnp.zeros((d, LANES), F32).at[:, :N_EXPERTS].set(router_w.astype(F32))
    rw_hi = rw.astype(BF16)
    rw = jnp.stack([rw_hi, (rw - rw_hi.astype(F32)).astype(BF16)])
    rb = jnp.full((1, LANES), -1e30, F32).at[0, :N_EXPERTS].set(router_b.astype(F32))
    g_blk = 4
    row = lambda i: (i, 0)
    const = lambda i: (0, 0)
    vmem = 2 * d * d * 2 + 12 * tm * d * 4 + (12 << 20)
    return pl.pallas_call(
        functools.partial(_out_router_kernel, n_first=xa.shape[0] // tm),
        grid=(n // tm,),
        in_specs=_pair_specs((tm, d), xa.shape[0] // tm) + [
                  pl.BlockSpec((1, tm, HG_WIDTH), lambda i: (0, i, 0)),
                  pl.BlockSpec((1, tm, HG_WIDTH), lambda i: (1, i, 0)),
                  pl.BlockSpec((tm, HG_WIDTH), lambda i: (i, g_blk)),
                  pl.BlockSpec((tm, Q_COLS), row),
                  pl.BlockSpec((1, HG_HEAD_DIM), const),
                  pl.BlockSpec((d, d), const),
                  pl.BlockSpec((1, d), const),
                  pl.BlockSpec((2, d, LANES), lambda i: (0, 0, 0)),
                  pl.BlockSpec((1, LANES), const)],
        out_specs=[pl.BlockSpec((tm, d), row), pl.BlockSpec((tm, d // LANES, LANES), lambda i: (i, 0, 0)),
                   pl.BlockSpec((tm, LANES), row), pl.BlockSpec((tm, LANES), row),
                   pl.BlockSpec((1, LANES), const)],
        out_shape=[jax.ShapeDtypeStruct((n, d), F32), jax.ShapeDtypeStruct((n, d // LANES, LANES), F32),
                   jax.ShapeDtypeStruct((n, LANES), jnp.int32),
                   jax.ShapeDtypeStruct((n, LANES), F32), jax.ShapeDtypeStruct((1, LANES), F32)],
        scratch_shapes=[pltpu.VMEM((1, LANES), F32)],
        compiler_params=_params(("arbitrary",), vmem),
        name="out_router",
    )(xa, xb, o_dirs, o_dirs, proj, att, hgrn_norm_w.reshape(1, HG_HEAD_DIM), w_out_bf16,
      ffn_norm_w.reshape(1, d), rw, rb)


ROW_PITCH = 20
GATHER_UNROLL = 8
ROW_COPY_PRIORITY = 1


def _start_row_gather(idx_ref, n_rows, src_hbm, buf, sem):
    slabs = src_hbm.shape[1]

    def issue(g, carry):
        base = pl.multiple_of(g * GATHER_UNROLL, GATHER_UNROLL)
        for u in range(GATHER_UNROLL):
            r = base + u
            pltpu.make_async_copy(src_hbm.at[idx_ref[0, 0, r]], buf.at[pl.ds(r * ROW_PITCH, slabs), :],
                                  sem).start(priority=ROW_COPY_PRIORITY)
        return carry
    lax.fori_loop(0, n_rows // GATHER_UNROLL, issue, 0)


def _wait_row_gather(n_rows, slabs, buf, sem):
    done = buf.at[pl.ds(0, n_rows * slabs), :]
    pltpu.make_async_copy(done, done, sem).wait()


def _gathered_cols(buf, row0, n_rows, cb):
    return buf[pl.ds(row0 * ROW_PITCH + cb, n_rows, stride=ROW_PITCH), :]


def _start_row_scatter(idx_ref, src, dst_hbm, sem):
    def issue(g, carry):
        base = pl.multiple_of(g * GATHER_UNROLL, GATHER_UNROLL)
        for u in range(GATHER_UNROLL):
            r = base + u
            pltpu.make_async_copy(src.at[pl.ds(r, 1), :], dst_hbm.at[pl.ds(idx_ref[0, 0, r], 1), :],
                                  sem).start(priority=ROW_COPY_PRIORITY)
        return carry
    lax.fori_loop(0, src.shape[0] // GATHER_UNROLL, issue, 0)


def _wait_row_scatter(src, sem):
    pltpu.make_async_copy(src, src, sem).wait()


def _moe_kernel(vb_ref, ve_ref, vlo_ref, vhi_ref, nvis_ref,
                tok_ref, tok_next_ref, orow_ref, xn_hbm, wg_ref, wl_ref, bg_ref, bl_ref, wd_ref, bd_ref,
                y_hbm, raw, xb, acc, sem_in, sem_out, *, nblk):
    w = pl.program_id(0)
    j = pl.program_id(1)
    nf = pl.num_programs(1)
    r = xb.shape[0]
    slabs = xn_hbm.shape[1]
    valid = w < nvis_ref[0]
    blk, lo, hi = vb_ref[w], vlo_ref[w], vhi_ref[w]
    slot = lax.rem(blk, 2)
    acc_blk = acc.at[slot]

    @pl.when(jnp.logical_and(valid, jnp.logical_and(lo == 0, j == 0)))
    def _():
        @pl.when(w == 0)
        def _():
            _start_row_gather(tok_ref, r, xn_hbm, raw, sem_in)
        _wait_row_gather(r, slabs, raw, sem_in)
        for cb in range(slabs):
            xb[:, cb * LANES:(cb + 1) * LANES] = _gathered_cols(raw, 0, r, cb).astype(BF16)

        @pl.when(blk + 1 < nblk)
        def _():
            _start_row_gather(tok_next_ref, r, xn_hbm, raw, sem_in)

        @pl.when(blk >= 2)
        def _():
            _wait_row_scatter(acc_blk, sem_out.at[slot])
        acc_blk[...] = jnp.zeros(acc_blk.shape, F32)

    @pl.when(valid)
    def _():
        sub_lo = lax.shift_right_logical(lo, MOE_SUB.bit_length() - 1)
        sub_hi = lax.shift_right_logical(hi + (MOE_SUB - 1), MOE_SUB.bit_length() - 1)
        n_sub = r // MOE_SUB
        for s0 in range(n_sub):
            for s1 in range(s0 + 1, n_sub + 1):
                rows = slice(s0 * MOE_SUB, s1 * MOE_SUB)

                @pl.when(jnp.logical_and(sub_lo == s0, sub_hi == s1))
                def _(rows=rows):
                    m = rows.stop - rows.start
                    xs = xb[rows, :]
                    gl = jnp.dot(xs, wg_ref[0].astype(BF16), preferred_element_type=F32) + bg_ref[0]
                    lin = jnp.dot(xs, wl_ref[0].astype(BF16), preferred_element_type=F32) + bl_ref[0]
                    gl = jnp.minimum(gl, SWIGLU_LIMIT)
                    lin = jnp.clip(lin, -SWIGLU_LIMIT, SWIGLU_LIMIT)
                    act = (lin + 1.0) * (gl * jax.nn.sigmoid(SWIGLU_ALPHA * gl))
                    rid = rows.start + lax.broadcasted_iota(jnp.int32, (m, 1), 0)
                    mine = jnp.logical_and(rid >= lo, rid < hi).astype(F32)
                    acc_blk[rows, :] += jnp.dot((act * mine).astype(BF16), wd_ref[0].astype(BF16),
                                                preferred_element_type=F32)

                    @pl.when(j == 0)
                    def _():
                        acc_blk[rows, :] += mine * bd_ref[0]

        @pl.when(jnp.logical_and(hi == r, j == nf - 1))
        def _():
            _start_row_scatter(orow_ref, acc_blk, y_hbm, sem_out.at[slot])

    @pl.when(jnp.logical_and(w == pl.num_programs(0) - 1, j == nf - 1))
    def _():
        for s in range(min(2, nblk)):
            _wait_row_scatter(acc.at[s], sem_out.at[s])


def _moe_experts(vis_blk, vis_exp, vis_lo, vis_hi, n_vis, slot_tok, out_row, xn3,
                 w_gate_up, b_gate_up, w_down, b_down):
    r, tf = MOE_R, MOE_TF
    n, slabs, _ = xn3.shape
    d = slabs * LANES
    nblk = slot_tok.shape[0]
    nf = D_FF // tf

    def tile(w, j, nv):
        return jnp.where(w < nv[0], j, nf - 1)

    idx_spec = lambda f: pl.BlockSpec((1, 1, r), lambda w, j, vb, ve, lo, hi, nv: (f(vb[w]), 0, 0),
                                      memory_space=pltpu.SMEM)
    vmem = (r * ROW_PITCH * LANES * 4 + r * d * 2 + 2 * r * d * 4 + 2 * 3 * d * tf * 4 + 3 * d * tf * 2
            + 3 * MOE_SUB * d * 4 + (6 << 20))
    return pl.pallas_call(
        functools.partial(_moe_kernel, nblk=nblk),
        grid_spec=pltpu.PrefetchScalarGridSpec(
            num_scalar_prefetch=5,
            grid=(vis_blk.shape[0], nf),
            in_specs=[idx_spec(lambda b: b), idx_spec(lambda b: jnp.minimum(b + 1, nblk - 1)), idx_spec(lambda b: b),
                      pl.BlockSpec(memory_space=pl.ANY),
                      pl.BlockSpec((1, d, tf), lambda w, j, vb, ve, lo, hi, nv: (ve[w], 0, tile(w, j, nv))),
                      pl.BlockSpec((1, d, tf), lambda w, j, vb, ve, lo, hi, nv: (ve[w], 0, nf + tile(w, j, nv))),
                      pl.BlockSpec((1, 1, tf), lambda w, j, vb, ve, lo, hi, nv: (ve[w], 0, tile(w, j, nv))),
                      pl.BlockSpec((1, 1, tf), lambda w, j, vb, ve, lo, hi, nv: (ve[w], 0, nf + tile(w, j, nv))),
                      pl.BlockSpec((1, tf, d), lambda w, j, vb, ve, lo, hi, nv: (ve[w], tile(w, j, nv), 0)),
                      pl.BlockSpec((1, 1, d), lambda w, j, vb, ve, lo, hi, nv: (ve[w], 0, 0))],
            out_specs=pl.BlockSpec(memory_space=pl.ANY),
            scratch_shapes=[pltpu.VMEM((r * ROW_PITCH, LANES), F32),
                            pltpu.VMEM((r, d), BF16),
                            pltpu.VMEM((2, r, d), F32),
                            pltpu.SemaphoreType.DMA, pltpu.SemaphoreType.DMA((2,))]),
        out_shape=jax.ShapeDtypeStruct((TOP_K * n, d), F32),
        compiler_params=_params(("arbitrary", "arbitrary"), vmem),
        name="moe_experts",
    )(vis_blk, vis_exp, vis_lo, vis_hi, n_vis, slot_tok, slot_tok, out_row, xn3,
      w_gate_up, w_gate_up, b_gate_up.reshape(N_EXPERTS, 1, 2 * D_FF),
      b_gate_up.reshape(N_EXPERTS, 1, 2 * D_FF), w_down, b_down.reshape(N_EXPERTS, 1, d))


def _final_kernel(h_ref, gate_ref, y0_ref, y1_ref, y2_ref, y3_ref, pa_ref, pb_ref, pp_ref, pnw_ref, pg_ref,
                  oa_ref, ob_ref, *, n_first):
    first = pl.program_id(0) < n_first
    h = h_ref[...]
    for kk, y_ref in enumerate((y0_ref, y1_ref, y2_ref, y3_ref)):
        h = h + gate_ref[:, kk:kk + 1] * y_ref[...]
    gate = jax.nn.sigmoid(jnp.dot(h.astype(BF16), pg_ref[...], preferred_element_type=F32))
    p = jnp.where(first, pa_ref[...], pb_ref[...])
    pe = jnp.dot(p.astype(BF16), pp_ref[...], preferred_element_type=F32)
    pe = pe * _rms_scale(pe) * pnw_ref[...]
    out = h + gate * pe

    @pl.when(first)
    def _():
        oa_ref[...] = out

    @pl.when(jnp.logical_not(first))
    def _():
        ob_ref[...] = out


def _final(y4, h1, gates, pa, pb, ple_proj_bf16, ple_norm_w, ple_gate_bf16):
    n, d = h1.shape
    tm = FIN_TM
    nt = n // tm
    n_first = pa.shape[0] // tm
    row = lambda i: (i, 0)
    const = lambda i: (0, 0)
    vmem = 2 * d * d * 2 + 20 * tm * d * 4 + (8 << 20)
    return pl.pallas_call(
        functools.partial(_final_kernel, n_first=n_first),
        grid=(nt,),
        in_specs=[pl.BlockSpec((tm, d), row),
                  pl.BlockSpec((tm, LANES), row)]
        + [pl.BlockSpec((tm, d), lambda i, kk=kk: (kk * nt + i, 0)) for kk in range(TOP_K)]
        + _pair_specs((tm, PLE_DIM), n_first)
        + [pl.BlockSpec((PLE_DIM, d), const),
           pl.BlockSpec((1, d), const),
           pl.BlockSpec((d, d), const)],
        out_specs=_pair_specs((tm, d), n_first),
        out_shape=[jax.ShapeDtypeStruct((pa.shape[0], d), F32), jax.ShapeDtypeStruct((pb.shape[0], d), F32)],
        compiler_params=_params(("arbitrary",), vmem),
        name="combine_final",
    )(h1, gates, y4, y4, y4, y4, pa, pb, ple_proj_bf16, ple_norm_w.reshape(1, d), ple_gate_bf16)


def _routing(eidx, counts, n_tokens):
    r = MOE_R
    nblk = (n_tokens * TOP_K) // r
    nvis_max = nblk + N_EXPERTS - 1
    order = jnp.argsort(eidx[:, :TOP_K].reshape(-1), stable=True).astype(jnp.int32)
    slot_tok = order // TOP_K
    out_row = (order % TOP_K) * n_tokens + slot_tok
    cnt = counts[0, :N_EXPERTS].astype(jnp.int32)
    ends = jnp.cumsum(cnt)
    starts = ends - cnt
    first_blk = starts // r
    n_blk = jnp.where(cnt > 0, (ends - 1) // r - first_blk + 1, 0)
    vis_end = jnp.cumsum(n_blk)
    n_vis = vis_end[-1]
    w = jnp.arange(nvis_max, dtype=jnp.int32)
    valid = w < n_vis
    e = jnp.minimum(jnp.sum((vis_end[None, :] <= w[:, None]).astype(jnp.int32), axis=1), N_EXPERTS - 1)
    e = jnp.where(valid, e, jnp.max(jnp.where(valid, e, 0)))
    blk = jnp.where(valid, first_blk[e] + w - (vis_end[e] - n_blk[e]), nblk - 1)
    lo = jnp.where(valid, jnp.clip(starts[e] - blk * r, 0, r), 0)
    hi = jnp.where(valid, jnp.clip(ends[e] - blk * r, 0, r), 0)
    i32 = lambda a: a.astype(jnp.int32)
    return (i32(blk), i32(e), i32(lo), i32(hi), i32(n_vis).reshape(1),
            slot_tok.reshape(nblk, 1, r), out_row.reshape(nblk, 1, r))


def _layer(xa, xb, pa, pb, groups, lb, mix_norm_w, w_in, hgrn_norm_w, q_norm_w, k_norm_w, w_out, ffn_norm_w,
           router_w, router_b, w_gate_up, b_gate_up, w_down, b_down, ple_proj, ple_norm_w, ple_gate):
    n = xa.shape[0] + xb.shape[0]
    proj = _in_proj(xa, xb, mix_norm_w, w_in.astype(BF16))
    att, k, v = _qk_rope(proj, q_norm_w, k_norm_w, groups)
    sums, masks = _hgrn_constants()
    o_dirs = _hgrn(proj, lb, sums, masks, groups)
    tok0 = 0
    for n_tok, t in groups:
        att = _attention(att, k, v, tok0, n_tok // t, t)
        tok0 += n_tok
    h1, xn3, eidx, gates, counts = _out_router(
        xa, xb, o_dirs, proj, att, hgrn_norm_w, w_out.astype(BF16), ffn_norm_w, router_w, router_b)
    y4 = _moe_experts(*_routing(eidx, counts, n), xn3, w_gate_up, b_gate_up, w_down, b_down)
    return _final(y4, h1, gates, pa, pb, ple_proj.astype(BF16), ple_norm_w, ple_gate.astype(BF16))


def kernel(x_prompt, x_sample, p_prompt, p_sample, mix_norm_w, w_in, hgrn_lb, hgrn_norm_w, q_norm_w, k_norm_w, w_out, ffn_norm_w, router_w, router_b, w_gate_up, b_gate_up, w_down, b_down, ple_proj, ple_norm_w, ple_gate):
    depth = w_in.shape[0]
    groups = ((x_prompt.shape[0] * x_prompt.shape[1], x_prompt.shape[1]),
              (x_sample.shape[0] * x_sample.shape[1], x_sample.shape[1]))
    d = x_prompt.shape[-1]
    ha, hb = x_prompt.reshape(-1, d), x_sample.reshape(-1, d)
    lb_all = jnp.cumsum(jax.nn.softmax(hgrn_lb.astype(F32), axis=0), axis=0)
    for i in range(depth):
        ha, hb = _layer(ha, hb, p_prompt[i].reshape(-1, PLE_DIM), p_sample[i].reshape(-1, PLE_DIM), groups,
                        lb_all[i], mix_norm_w[i], w_in[i], hgrn_norm_w[i], q_norm_w[i], k_norm_w[i],
                        w_out[i], ffn_norm_w[i], router_w[i], router_b[i], w_gate_up[i], b_gate_up[i],
                        w_down[i], b_down[i], ple_proj[i], ple_norm_w[i], ple_gate[i])
    return ha.reshape(x_prompt.shape), hb.reshape(x_sample.shape)
```

```python
import functools

import numpy as np
import jax
import jax.numpy as jnp
from jax import lax
from jax.experimental import pallas as pl
from jax.experimental.pallas import tpu as pltpu

F32 = jnp.float32
BF16 = jnp.bfloat16

D_MODEL = 2048
GRID_W = 64
HG_WIDTH = 1024
HG_HEAD_DIM = 128
HG_HEADS = HG_WIDTH // HG_HEAD_DIM
CHUNK = 64
HEAD_DIM = 128
N_Q_HEADS = 8
N_KV_HEADS = 2
Q_PER_KV = N_Q_HEADS // N_KV_HEADS
ROPE_THETA = 10000.0
ROPE_AXIS_PAIRS = HEAD_DIM // 4
N_EXPERTS = 32
TOP_K = 4
D_FF = D_MODEL
SWIGLU_LIMIT = 7.0
SWIGLU_ALPHA = 1.702
PLE_DIM = 256
NORM_EPS = 1e-6
Q_COLS = N_Q_HEADS * HEAD_DIM
KV_COLS = N_KV_HEADS * HEAD_DIM
IN_COLS = 5 * HG_WIDTH + Q_COLS + 2 * KV_COLS

LANES = 128
V7X_VMEM_BUDGET_BYTES = 56 * 1024 * 1024

IN_TM, IN_TN = 1024, 512
QK_TM = 512
ATT_TQ = 256
ATT_TK = 1024
HG_TB = 512
HG_UNROLL = 4
OUT_TM = 256
MOE_R = 512
MOE_SUB = 128
MOE_TF = 512
FIN_TM = 256
ROW_CHUNK = 128
N_LEVELS = 6


def _params(sem, vmem_bytes):
    return pltpu.CompilerParams(dimension_semantics=sem,
                                vmem_limit_bytes=min(int(vmem_bytes), V7X_VMEM_BUDGET_BYTES))


def _rms_scale(x):
    return lax.rsqrt(jnp.mean(x * x, axis=-1, keepdims=True) + NORM_EPS)


def _dot_nt(a, b):
    return lax.dot_general(a, b, (((1,), (1,)), ((), ())), preferred_element_type=F32)


def _pair_specs(block, n_first):
    zeros = (0,) * (len(block) - 1)
    return [pl.BlockSpec(block, lambda i, *_: (jnp.minimum(i, n_first - 1),) + zeros),
            pl.BlockSpec(block, lambda i, *_: (jnp.maximum(i - n_first, 0),) + zeros)]


def _in_proj_kernel(xa_ref, xb_ref, nw_ref, w_ref, o_ref, xn_ref, *, n_first):
    @pl.when(pl.program_id(1) == 0)
    def _():
        first = pl.program_id(0) < n_first

        def body(c, carry):
            rows = pl.ds(pl.multiple_of(c * ROW_CHUNK, ROW_CHUNK), ROW_CHUNK)
            x = jnp.where(first, xa_ref[rows, :], xb_ref[rows, :])
            xn_ref[rows, :] = (x * _rms_scale(x) * nw_ref[...]).astype(BF16)
            return carry
        lax.fori_loop(0, xa_ref.shape[0] // ROW_CHUNK, body, 0)

    o_ref[...] = jnp.dot(xn_ref[...], w_ref[...], preferred_element_type=F32)


def _in_proj(xa, xb, norm_w, w_bf16):
    d = xa.shape[1]
    n = xa.shape[0] + xb.shape[0]
    cols = w_bf16.shape[1]
    tm, tn = IN_TM, IN_TN
    vmem = 4 * tm * d * 4 + tm * d * 2 + 2 * d * tn * 2 + 2 * tm * tn * 4 + (8 << 20)
    return pl.pallas_call(
        functools.partial(_in_proj_kernel, n_first=xa.shape[0] // tm),
        grid=(n // tm, cols // tn),
        in_specs=_pair_specs((tm, d), xa.shape[0] // tm) + [
            pl.BlockSpec((1, d), lambda i, j: (0, 0)),
            pl.BlockSpec((d, tn), lambda i, j: (0, j))],
        out_specs=pl.BlockSpec((tm, tn), lambda i, j: (i, j)),
        out_shape=jax.ShapeDtypeStruct((n, cols), F32),
        scratch_shapes=[pltpu.VMEM((tm, d), BF16)],
        compiler_params=_params(("parallel", "arbitrary"), vmem),
        name="in_proj",
    )(xa, xb, norm_w.reshape(1, d), w_bf16)


def _rope_tables(t_max):
    rows = t_max // GRID_W
    row = jnp.repeat(jnp.arange(rows, dtype=F32), GRID_W)
    col = jnp.tile(jnp.arange(GRID_W, dtype=F32), rows)
    inv_freq = ROPE_THETA ** (-jnp.arange(ROPE_AXIS_PAIRS, dtype=F32) / ROPE_AXIS_PAIRS)
    ang = jnp.concatenate([row[:, None] * inv_freq, col[:, None] * inv_freq], axis=-1)
    cos = jnp.repeat(jnp.cos(ang), 2, axis=-1)
    sin = jnp.repeat(jnp.sin(ang), 2, axis=-1)
    even = (jnp.arange(HEAD_DIM) % 2 == 0)[None, :]
    s_next = jnp.where(even, -sin, 0.0)
    s_prev = jnp.where(even, 0.0, sin)
    return cos, s_next, s_prev


def _qk_rope_kernel(q_ref, k_ref, v_ref, c_ref, sn_ref, sp_ref, qw_ref, kw_ref,
                    qo_ref, ko_ref, vo_ref):
    c, sn, sp = c_ref[...], sn_ref[...], sp_ref[...]

    def norm_rope(x, w):
        y = x * _rms_scale(x) * w
        return y * c + pltpu.roll(y, HEAD_DIM - 1, 1) * sn + pltpu.roll(y, 1, 1) * sp

    scale = HEAD_DIM ** -0.5
    for h in range(N_Q_HEADS):
        sl = slice(h * HEAD_DIM, (h + 1) * HEAD_DIM)
        qo_ref[:, sl] = (norm_rope(q_ref[:, sl], qw_ref[...]) * scale).astype(BF16)
    for h in range(N_KV_HEADS):
        sl = slice(h * HEAD_DIM, (h + 1) * HEAD_DIM)
        ko_ref[:, sl] = norm_rope(k_ref[:, sl], kw_ref[...]).astype(BF16)
    vo_ref[...] = v_ref[...].astype(BF16)


def _qk_rope(proj, q_norm_w, k_norm_w, groups):
    n = proj.shape[0]
    tm = QK_TM
    t_max = max(t for _, t in groups)
    cos, s_next, s_prev = _rope_tables(t_max)

    def pos_block(i):
        blk = jnp.int32(0)
        start = 0
        for n_tok, t in groups:
            first = start // tm
            blk = jnp.where(i >= first, (i - first) % (t // tm), blk)
            start += n_tok
        return blk

    tab_spec = pl.BlockSpec((tm, HEAD_DIM), lambda i: (pos_block(i), 0))
    q_blk, k_blk, v_blk = (5 * HG_WIDTH) // Q_COLS, (5 * HG_WIDTH + Q_COLS) // KV_COLS, \
        (5 * HG_WIDTH + Q_COLS + KV_COLS) // KV_COLS
    return pl.pallas_call(
        _qk_rope_kernel,
        grid=(n // tm,),
        in_specs=[pl.BlockSpec((tm, Q_COLS), lambda i: (i, q_blk)),
                  pl.BlockSpec((tm, KV_COLS), lambda i: (i, k_blk)),
                  pl.BlockSpec((tm, KV_COLS), lambda i: (i, v_blk)),
                  tab_spec, tab_spec, tab_spec,
                  pl.BlockSpec((1, HEAD_DIM), lambda i: (0, 0)),
                  pl.BlockSpec((1, HEAD_DIM), lambda i: (0, 0))],
        out_specs=[pl.BlockSpec((tm, Q_COLS), lambda i: (i, 0)),
                   pl.BlockSpec((tm, KV_COLS), lambda i: (i, 0)),
                   pl.BlockSpec((tm, KV_COLS), lambda i: (i, 0))],
        out_shape=[jax.ShapeDtypeStruct((n, Q_COLS), BF16),
                   jax.ShapeDtypeStruct((n, KV_COLS), BF16),
                   jax.ShapeDtypeStruct((n, KV_COLS), BF16)],
        compiler_params=_params(("parallel",), 32 << 20),
        name="qk_rope",
    )(proj, proj, proj, cos, s_next, s_prev,
      q_norm_w.reshape(1, HEAD_DIM), k_norm_w.reshape(1, HEAD_DIM))


def _attn_kernel(q_ref, k_ref, v_ref, o_ref):
    n_kv = k_ref.shape[0] // ATT_TK
    items = [(h, c) for h in range(Q_PER_KV) for c in range(n_kv)]

    def scores(item):
        h, c = item
        return _dot_nt(q_ref[:, h * HEAD_DIM:(h + 1) * HEAD_DIM], k_ref[c * ATT_TK:(c + 1) * ATT_TK, :])

    s = scores(items[0])
    m = l = acc = None
    for i, (h, c) in enumerate(items):
        s_next = scores(items[i + 1]) if i + 1 < len(items) else None
        vc = v_ref[c * ATT_TK:(c + 1) * ATT_TK, :]
        mc = jnp.max(s, axis=-1, keepdims=True)
        if c == 0:
            m = mc
            p = jnp.exp(s - m)
            l = jnp.sum(p, axis=-1, keepdims=True)
            acc = jnp.dot(p.astype(BF16), vc, preferred_element_type=F32)
        else:
            m_new = jnp.maximum(m, mc)
            alpha = jnp.exp(m - m_new)
            p = jnp.exp(s - m_new)
            l = alpha * l + jnp.sum(p, axis=-1, keepdims=True)
            acc = alpha * acc + jnp.dot(p.astype(BF16), vc, preferred_element_type=F32)
            m = m_new
        if c == n_kv - 1:
            o_ref[:, h * HEAD_DIM:(h + 1) * HEAD_DIM] = (acc / l).astype(BF16)
        s = s_next


def _attention(q, k, v, tok0, batch, t):
    tq = ATT_TQ
    nq = t // tq
    row0, seq0 = tok0 // tq, tok0 // t
    gcols = Q_PER_KV * HEAD_DIM
    vmem = 4 * t * HEAD_DIM * 2 + 3 * tq * t * 4 + (8 << 20)
    q_spec = pl.BlockSpec((tq, gcols), lambda b, g, i: (row0 + b * nq + i, g))
    kv_spec = pl.BlockSpec((t, HEAD_DIM), lambda b, g, i: (seq0 + b, g))
    return pl.pallas_call(
        _attn_kernel,
        grid=(batch, N_KV_HEADS, nq),
        in_specs=[q_spec, kv_spec, kv_spec],
        out_specs=q_spec,
        out_shape=jax.ShapeDtypeStruct(q.shape, BF16),
        input_output_aliases={0: 0},
        compiler_params=_params(("parallel", "parallel", "arbitrary"), vmem),
        name="attention",
    )(q, k, v)


def _hgrn_constants():
    c = CHUNK
    t = np.arange(c)[:, None]
    u = np.arange(c)[None, :]
    sums = [(u <= t), (u > t)]
    masks = []
    for lvl in range(N_LEVELS):
        half = (c // 2) >> lvl
        bound = (t // (2 * half)) * (2 * half) + half - 1
        late = t > bound
        sums.append(np.where(late, (u > bound) & (u <= t), (u > t) & (u <= bound)))
        same = (t // (2 * half)) == (u // (2 * half))
        masks.append(same & late & (u <= bound))
    masks.append(t == u)
    sums = np.stack(sums).astype(np.float32)
    masks = np.stack(masks).astype(np.float32)
    both_s = np.stack([sums, sums[:, ::-1, ::-1]]).reshape(2, -1, c)
    both_m = np.stack([masks, masks[:, ::-1, ::-1]]).reshape(2, -1, c)
    return jnp.asarray(both_s, BF16), jnp.asarray(both_m, F32)


def _hgrn_kernel(q_ref, f_ref, v_ref, lb_ref, sums_ref, masks_ref, o_ref, st_ref, *, nchunk, nblk,
                 seq_first_blk, seq_last_blk):
    d = pl.program_id(0)
    i = pl.program_id(1)
    blk = i + d * (nblk - 1 - 2 * i)
    fresh = functools.reduce(
        jnp.logical_or,
        [jnp.logical_or(jnp.logical_and(d == 0, blk == f), jnp.logical_and(d == 1, blk == l))
         for f, l in zip(seq_first_blk, seq_last_blk)])

    @pl.when(fresh)
    def _():
        st_ref[...] = jnp.zeros_like(st_ref)

    lb = lb_ref[0]
    sums = sums_ref[0]
    c = CHUNK

    def chunk_body(j, carry):
        jj = j + d * (nchunk - 1 - 2 * j)
        rows = pl.ds(pl.multiple_of(jj * c, c), c)
        f = lb + (1.0 - lb) * jax.nn.sigmoid(f_ref[rows, :])
        k = 1.0 - f
        lf = jnp.log(f)
        qpre = q_ref[rows, :]
        q = qpre * jax.nn.sigmoid(qpre)
        vb = v_ref[rows, :].astype(BF16)
        lf_hi = lf.astype(BF16)
        lf_lo = (lf - lf_hi.astype(F32)).astype(BF16)
        e = jnp.exp(jnp.dot(sums, lf_hi, preferred_element_type=F32)
                    + jnp.dot(sums, lf_lo, preferred_element_type=F32))
        g = jnp.exp(jnp.sum(lf, axis=0, keepdims=True))
        qt = (q * e[0:c]).astype(BF16)
        kd = (k * e[c:2 * c]).astype(BF16)
        ql = [q.astype(BF16)] + [(q * e[(2 + lvl) * c:(3 + lvl) * c]).astype(BF16) for lvl in range(N_LEVELS)]
        kl = [k.astype(BF16)] + [(k * e[(2 + lvl) * c:(3 + lvl) * c]).astype(BF16) for lvl in range(N_LEVELS)]
        mk = [masks_ref[0, N_LEVELS * c:(N_LEVELS + 1) * c, :]] + [
            masks_ref[0, lvl * c:(lvl + 1) * c, :] for lvl in range(N_LEVELS)]
        heads = [slice(h * HG_HEAD_DIM, (h + 1) * HG_HEAD_DIM) for h in range(HG_HEADS)]
        scores = []
        for sl in heads:
            a = mk[0] * _dot_nt(ql[0][:, sl], kl[0][:, sl])
            for lvl in range(1, N_LEVELS + 1):
                a = a + mk[lvl] * _dot_nt(ql[lvl][:, sl], kl[lvl][:, sl])
            scores.append(a.astype(BF16))
        for h, sl in enumerate(heads):
            st = st_ref[h]
            o = _dot_nt(qt[:, sl], st.astype(BF16)) + jnp.dot(scores[h], vb[:, sl], preferred_element_type=F32)
            ut = lax.dot_general(vb[:, sl], kd[:, sl], (((0,), (0,)), ((), ())), preferred_element_type=F32)
            st_ref[h] = st * g[:, sl] + ut
            o_ref[0, rows, sl] = o
        return carry

    lax.fori_loop(0, nchunk, chunk_body, 0, unroll=HG_UNROLL)


def _hgrn(proj, lb, sums, masks, groups):
    n = proj.shape[0]
    tb = HG_TB
    nblk = n // tb
    first_blk, last_blk, tok = [], [], 0
    for n_tok, t in groups:
        for s in range(n_tok // t):
            first_blk.append((tok + s * t) // tb)
            last_blk.append((tok + (s + 1) * t) // tb - 1)
        tok += n_tok

    def rows(d, i):
        return i + d * (nblk - 1 - 2 * i)

    in_blk = lambda col: pl.BlockSpec((tb, HG_WIDTH), lambda d, i: (rows(d, i), col(d)))
    return pl.pallas_call(
        functools.partial(_hgrn_kernel, nchunk=tb // CHUNK, nblk=nblk,
                          seq_first_blk=tuple(first_blk), seq_last_blk=tuple(last_blk)),
        grid=(2, nblk),
        in_specs=[in_blk(lambda d: 0), in_blk(lambda d: 1 + d), in_blk(lambda d: 3),
                  pl.BlockSpec((1, 1, HG_WIDTH), lambda d, i: (d, 0, 0)),
                  pl.BlockSpec((1,) + sums.shape[1:], lambda d, i: (d, 0, 0)),
                  pl.BlockSpec((1,) + masks.shape[1:], lambda d, i: (d, 0, 0))],
        out_specs=pl.BlockSpec((1, tb, HG_WIDTH), lambda d, i: (d, rows(d, i), 0)),
        out_shape=jax.ShapeDtypeStruct((2, n, HG_WIDTH), F32),
        scratch_shapes=[pltpu.VMEM((HG_HEADS, HG_HEAD_DIM, HG_HEAD_DIM), F32)],
        compiler_params=_params(("parallel", "arbitrary"), 40 << 20),
        name="hgrn2",
    )(proj, proj, proj, lb.reshape(2, 1, HG_WIDTH), sums, masks)


def _out_router_kernel(xa_ref, xb_ref, of_ref, ob_ref, g_ref, att_ref, hw_ref, wo_ref, fw_ref, rw_ref, rb_ref,
                       h_ref, xn_ref, eidx_ref, gate_ref, cnt_ref, cnt_acc, *, n_first):
    @pl.when(pl.program_id(0) == 0)
    def _():
        cnt_acc[...] = jnp.zeros_like(cnt_acc)

    o = of_ref[0] + ob_ref[0]
    parts = []
    for h in range(HG_HEADS):
        sl = slice(h * HG_HEAD_DIM, (h + 1) * HG_HEAD_DIM)
        oh = o[:, sl]
        gh = g_ref[:, sl]
        parts.append(((oh * _rms_scale(oh) * hw_ref[...]) * (gh * jax.nn.sigmoid(gh))).astype(BF16))
    oa = jnp.concatenate(parts, axis=1)
    h1 = (jnp.where(pl.program_id(0) < n_first, xa_ref[...], xb_ref[...])
          + jnp.dot(oa, wo_ref[0:HG_WIDTH, :], preferred_element_type=F32)
          + jnp.dot(att_ref[...], wo_ref[HG_WIDTH:, :], preferred_element_type=F32))
    h_ref[...] = h1
    xn = h1 * _rms_scale(h1) * fw_ref[...]
    for cb in range(xn_ref.shape[1]):
        xn_ref[:, cb, :] = xn[:, cb * LANES:(cb + 1) * LANES]

    xn_hi = xn.astype(BF16)
    xn_lo = (xn - xn_hi.astype(F32)).astype(BF16)
    logits = (jnp.dot(xn_hi, rw_ref[0], preferred_element_type=F32)
              + (jnp.dot(xn_lo, rw_ref[0], preferred_element_type=F32)
                 + jnp.dot(xn_hi, rw_ref[1], preferred_element_type=F32))) + rb_ref[...]
    tm = logits.shape[0]
    lane = lax.broadcasted_iota(jnp.int32, (tm, LANES), 1)
    work = logits
    vals, idxs = [], []
    for _ in range(TOP_K):
        m = jnp.max(work, axis=-1, keepdims=True)
        idx = jnp.min(jnp.where(work == m, lane, LANES), axis=-1, keepdims=True)
        vals.append(m)
        idxs.append(idx)
        work = jnp.where(lane == idx, -jnp.inf, work)
    exps = [jnp.exp(v - vals[0]) for v in vals]
    denom = exps[0] + exps[1] + exps[2] + exps[3]

    onehot = jnp.zeros((tm, LANES), F32)
    for idx in idxs:
        onehot = onehot + (lane == idx).astype(F32)
    eidx = jnp.zeros((tm, LANES), jnp.int32)
    gate = jnp.zeros((tm, LANES), F32)
    for kk in range(TOP_K):
        eidx = jnp.where(lane == kk, idxs[kk], eidx)
        gate = jnp.where(lane == kk, exps[kk] / denom, gate)
    eidx_ref[...] = eidx
    gate_ref[...] = gate
    cnt_acc[...] = cnt_acc[...] + jnp.sum(onehot, axis=0, keepdims=True)
    cnt_ref[...] = cnt_acc[...]


def _out_router(xa, xb, o_dirs, proj, att, hgrn_norm_w, w_out_bf16, ffn_norm_w, router_w, router_b):
    d = xa.shape[1]
    n = xa.shape[0] + xb.shape[0]
    tm = OUT_TM
    rw = jnp.zeros((d, LANES), F32).at[:, :N_EXPERTS].set(router_w.astype(F32))
    rw_hi = rw.astype(BF16)
    rw = jnp.stack([rw_hi, (rw - rw_hi.astype(F32)).astype(BF16)])
    rb = jnp.full((1, LANES), -1e30, F32).at[0, :N_EXPERTS].set(router_b.astype(F32))
    g_blk = 4
    row = lambda i: (i, 0)
    const = lambda i: (0, 0)
    vmem = 2 * d * d * 2 + 12 * tm * d * 4 + (12 << 20)
    return pl.pallas_call(
        functools.partial(_out_router_kernel, n_first=xa.shape[0] // tm),
        grid=(n // tm,),
        in_specs=_pair_specs((tm, d), xa.shape[0] // tm) + [
                  pl.BlockSpec((1, tm, HG_WIDTH), lambda i: (0, i, 0)),
                  pl.BlockSpec((1, tm, HG_WIDTH), lambda i: (1, i, 0)),
                  pl.BlockSpec((tm, HG_WIDTH), lambda i: (i, g_blk)),
                  pl.BlockSpec((tm, Q_COLS), row),
                  pl.BlockSpec((1, HG_HEAD_DIM), const),
                  pl.BlockSpec((d, d), const),
                  pl.BlockSpec((1, d), const),
                  pl.BlockSpec((2, d, LANES), lambda i: (0, 0, 0)),
                  pl.BlockSpec((1, LANES), const)],
        out_specs=[pl.BlockSpec((tm, d), row), pl.BlockSpec((tm, d // LANES, LANES), lambda i: (i, 0, 0)),
                   pl.BlockSpec((tm, LANES), row), pl.BlockSpec((tm, LANES), row),
                   pl.BlockSpec((1, LANES), const)],
        out_shape=[jax.ShapeDtypeStruct((n, d), F32), jax.ShapeDtypeStruct((n, d // LANES, LANES), F32),
                   jax.ShapeDtypeStruct((n, LANES), jnp.int32),
                   jax.ShapeDtypeStruct((n, LANES), F32), jax.ShapeDtypeStruct((1, LANES), F32)],
        scratch_shapes=[pltpu.VMEM((1, LANES), F32)],
        compiler_params=_params(("arbitrary",), vmem),
        name="out_router",
    )(xa, xb, o_dirs, o_dirs, proj, att, hgrn_norm_w.reshape(1, HG_HEAD_DIM), w_out_bf16,
      ffn_norm_w.reshape(1, d), rw, rb)


ROW_PITCH = 20
GATHER_UNROLL = 8
ROW_COPY_PRIORITY = 1


def _start_row_gather(idx_ref, n_rows, src_hbm, buf, sem):
    slabs = src_hbm.shape[1]

    def issue(g, carry):
        base = pl.multiple_of(g * GATHER_UNROLL, GATHER_UNROLL)
        for u in range(GATHER_UNROLL):
            r = base + u
            pltpu.make_async_copy(src_hbm.at[idx_ref[0, 0, r]], buf.at[pl.ds(r * ROW_PITCH, slabs), :],
                                  sem).start(priority=ROW_COPY_PRIORITY)
        return carry
    lax.fori_loop(0, n_rows // GATHER_UNROLL, issue, 0)


def _wait_row_gather(n_rows, slabs, buf, sem):
    done = buf.at[pl.ds(0, n_rows * slabs), :]
    pltpu.make_async_copy(done, done, sem).wait()


def _gathered_cols(buf, row0, n_rows, cb):
    return buf[pl.ds(row0 * ROW_PITCH + cb, n_rows, stride=ROW_PITCH), :]


def _start_row_scatter(idx_ref, src, dst_hbm, sem):
    def issue(g, carry):
        base = pl.multiple_of(g * GATHER_UNROLL, GATHER_UNROLL)
        for u in range(GATHER_UNROLL):
            r = base + u
            pltpu.make_async_copy(src.at[pl.ds(r, 1), :], dst_hbm.at[pl.ds(idx_ref[0, 0, r], 1), :],
                                  sem).start(priority=ROW_COPY_PRIORITY)
        return carry
    lax.fori_loop(0, src.shape[0] // GATHER_UNROLL, issue, 0)


def _wait_row_scatter(src, sem):
    pltpu.make_async_copy(src, src, sem).wait()


def _moe_kernel(vb_ref, ve_ref, vlo_ref, vhi_ref, nvis_ref,
                tok_ref, tok_next_ref, orow_ref, xn_hbm, wg_ref, wl_ref, bg_ref, bl_ref, wd_ref, bd_ref,
                y_hbm, raw, xb, acc, sem_in, sem_out, *, nblk):
    w = pl.program_id(0)
    j = pl.program_id(1)
    nf = pl.num_programs(1)
    r = xb.shape[0]
    slabs = xn_hbm.shape[1]
    valid = w < nvis_ref[0]
    blk, lo, hi = vb_ref[w], vlo_ref[w], vhi_ref[w]
    slot = lax.rem(blk, 2)
    acc_blk = acc.at[slot]

    @pl.when(jnp.logical_and(valid, jnp.logical_and(lo == 0, j == 0)))
    def _():
        @pl.when(w == 0)
        def _():
            _start_row_gather(tok_ref, r, xn_hbm, raw, sem_in)
        _wait_row_gather(r, slabs, raw, sem_in)
        for cb in range(slabs):
            xb[:, cb * LANES:(cb + 1) * LANES] = _gathered_cols(raw, 0, r, cb).astype(BF16)

        @pl.when(blk + 1 < nblk)
        def _():
            _start_row_gather(tok_next_ref, r, xn_hbm, raw, sem_in)

        @pl.when(blk >= 2)
        def _():
            _wait_row_scatter(acc_blk, sem_out.at[slot])
        acc_blk[...] = jnp.zeros(acc_blk.shape, F32)

    @pl.when(valid)
    def _():
        sub_lo = lax.shift_right_logical(lo, MOE_SUB.bit_length() - 1)
        sub_hi = lax.shift_right_logical(hi + (MOE_SUB - 1), MOE_SUB.bit_length() - 1)
        n_sub = r // MOE_SUB
        for s0 in range(n_sub):
            for s1 in range(s0 + 1, n_sub + 1):
                rows = slice(s0 * MOE_SUB, s1 * MOE_SUB)

                @pl.when(jnp.logical_and(sub_lo == s0, sub_hi == s1))
                def _(rows=rows):
                    m = rows.stop - rows.start
                    xs = xb[rows, :]
                    gl = jnp.dot(xs, wg_ref[0].astype(BF16), preferred_element_type=F32) + bg_ref[0]
                    lin = jnp.dot(xs, wl_ref[0].astype(BF16), preferred_element_type=F32) + bl_ref[0]
                    gl = jnp.minimum(gl, SWIGLU_LIMIT)
                    lin = jnp.clip(lin, -SWIGLU_LIMIT, SWIGLU_LIMIT)
                    act = (lin + 1.0) * (gl * jax.nn.sigmoid(SWIGLU_ALPHA * gl))
                    rid = rows.start + lax.broadcasted_iota(jnp.int32, (m, 1), 0)
                    mine = jnp.logical_and(rid >= lo, rid < hi).astype(F32)
                    acc_blk[rows, :] += jnp.dot((act * mine).astype(BF16), wd_ref[0].astype(BF16),
                                                preferred_element_type=F32)

                    @pl.when(j == 0)
                    def _():
                        acc_blk[rows, :] += mine * bd_ref[0]

        @pl.when(jnp.logical_and(hi == r, j == nf - 1))
        def _():
            _start_row_scatter(orow_ref, acc_blk, y_hbm, sem_out.at[slot])

    @pl.when(jnp.logical_and(w == pl.num_programs(0) - 1, j == nf - 1))
    def _():
        for s in range(min(2, nblk)):
            _wait_row_scatter(acc.at[s], sem_out.at[s])


def _moe_experts(vis_blk, vis_exp, vis_lo, vis_hi, n_vis, slot_tok, out_row, xn3,
                 w_gate_up, b_gate_up, w_down, b_down):
    r, tf = MOE_R, MOE_TF
    n, slabs, _ = xn3.shape
    d = slabs * LANES
    nblk = slot_tok.shape[0]
    nf = D_FF // tf

    def tile(w, j, nv):
        return jnp.where(w < nv[0], j, nf - 1)

    idx_spec = lambda f: pl.BlockSpec((1, 1, r), lambda w, j, vb, ve, lo, hi, nv: (f(vb[w]), 0, 0),
                                      memory_space=pltpu.SMEM)
    vmem = (r * ROW_PITCH * LANES * 4 + r * d * 2 + 2 * r * d * 4 + 2 * 3 * d * tf * 4 + 3 * d * tf * 2
            + 3 * MOE_SUB * d * 4 + (6 << 20))
    return pl.pallas_call(
        functools.partial(_moe_kernel, nblk=nblk),
        grid_spec=pltpu.PrefetchScalarGridSpec(
            num_scalar_prefetch=5,
            grid=(vis_blk.shape[0], nf),
            in_specs=[idx_spec(lambda b: b), idx_spec(lambda b: jnp.minimum(b + 1, nblk - 1)), idx_spec(lambda b: b),
                      pl.BlockSpec(memory_space=pl.ANY),
                      pl.BlockSpec((1, d, tf), lambda w, j, vb, ve, lo, hi, nv: (ve[w], 0, tile(w, j, nv))),
                      pl.BlockSpec((1, d, tf), lambda w, j, vb, ve, lo, hi, nv: (ve[w], 0, nf + tile(w, j, nv))),
                      pl.BlockSpec((1, 1, tf), lambda w, j, vb, ve, lo, hi, nv: (ve[w], 0, tile(w, j, nv))),
                      pl.BlockSpec((1, 1, tf), lambda w, j, vb, ve, lo, hi, nv: (ve[w], 0, nf + tile(w, j, nv))),
                      pl.BlockSpec((1, tf, d), lambda w, j, vb, ve, lo, hi, nv: (ve[w], tile(w, j, nv), 0)),
                      pl.BlockSpec((1, 1, d), lambda w, j, vb, ve, lo, hi, nv: (ve[w], 0, 0))],
            out_specs=pl.BlockSpec(memory_space=pl.ANY),
            scratch_shapes=[pltpu.VMEM((r * ROW_PITCH, LANES), F32),
                            pltpu.VMEM((r, d), BF16),
                            pltpu.VMEM((2, r, d), F32),
                            pltpu.SemaphoreType.DMA, pltpu.SemaphoreType.DMA((2,))]),
        out_shape=jax.ShapeDtypeStruct((TOP_K * n, d), F32),
        compiler_params=_params(("arbitrary", "arbitrary"), vmem),
        name="moe_experts",
    )(vis_blk, vis_exp, vis_lo, vis_hi, n_vis, slot_tok, slot_tok, out_row, xn3,
      w_gate_up, w_gate_up, b_gate_up.reshape(N_EXPERTS, 1, 2 * D_FF),
      b_gate_up.reshape(N_EXPERTS, 1, 2 * D_FF), w_down, b_down.reshape(N_EXPERTS, 1, d))


def _final_kernel(h_ref, gate_ref, y0_ref, y1_ref, y2_ref, y3_ref, pa_ref, pb_ref, pp_ref, pnw_ref, pg_ref,
                  oa_ref, ob_ref, *, n_first):
    first = pl.program_id(0) < n_first
    h = h_ref[...]
    for kk, y_ref in enumerate((y0_ref, y1_ref, y2_ref, y3_ref)):
        h = h + gate_ref[:, kk:kk + 1] * y_ref[...]
    gate = jax.nn.sigmoid(jnp.dot(h.astype(BF16), pg_ref[...], preferred_element_type=F32))
    p = jnp.where(first, pa_ref[...], pb_ref[...])
    pe = jnp.dot(p.astype(BF16), pp_ref[...], preferred_element_type=F32)
    pe = pe * _rms_scale(pe) * pnw_ref[...]
    out = h + gate * pe

    @pl.when(first)
    def _():
        oa_ref[...] = out

    @pl.when(jnp.logical_not(first))
    def _():
        ob_ref[...] = out


def _final(y4, h1, gates, pa, pb, ple_proj_bf16, ple_norm_w, ple_gate_bf16):
    n, d = h1.shape
    tm = FIN_TM
    nt = n // tm
    n_first = pa.shape[0] // tm
    row = lambda i: (i, 0)
    const = lambda i: (0, 0)
    vmem = 2 * d * d * 2 + 20 * tm * d * 4 + (8 << 20)
    return pl.pallas_call(
        functools.partial(_final_kernel, n_first=n_first),
        grid=(nt,),
        in_specs=[pl.BlockSpec((tm, d), row),
                  pl.BlockSpec((tm, LANES), row)]
        + [pl.BlockSpec((tm, d), lambda i, kk=kk: (kk * nt + i, 0)) for kk in range(TOP_K)]
        + _pair_specs((tm, PLE_DIM), n_first)
        + [pl.BlockSpec((PLE_DIM, d), const),
           pl.BlockSpec((1, d), const),
           pl.BlockSpec((d, d), const)],
        out_specs=_pair_specs((tm, d), n_first),
        out_shape=[jax.ShapeDtypeStruct((pa.shape[0], d), F32), jax.ShapeDtypeStruct((pb.shape[0], d), F32)],
        compiler_params=_params(("arbitrary",), vmem),
        name="combine_final",
    )(h1, gates, y4, y4, y4, y4, pa, pb, ple_proj_bf16, ple_norm_w.reshape(1, d), ple_gate_bf16)


def _routing(eidx, counts, n_tokens):
    r = MOE_R
    nblk = (n_tokens * TOP_K) // r
    nvis_max = nblk + N_EXPERTS - 1
    order = jnp.argsort(eidx[:, :TOP_K].reshape(-1), stable=True).astype(jnp.int32)
    slot_tok = order // TOP_K
    out_row = (order % TOP_K) * n_tokens + slot_tok
    cnt = counts[0, :N_EXPERTS].astype(jnp.int32)
    ends = jnp.cumsum(cnt)
    starts = ends - cnt
    first_blk = starts // r
    n_blk = jnp.where(cnt > 0, (ends - 1) // r - first_blk + 1, 0)
    vis_end = jnp.cumsum(n_blk)
    n_vis = vis_end[-1]
    w = jnp.arange(nvis_max, dtype=jnp.int32)
    valid = w < n_vis
    e = jnp.minimum(jnp.sum((vis_end[None, :] <= w[:, None]).astype(jnp.int32), axis=1), N_EXPERTS - 1)
    e = jnp.where(valid, e, jnp.max(jnp.where(valid, e, 0)))
    blk = jnp.where(valid, first_blk[e] + w - (vis_end[e] - n_blk[e]), nblk - 1)
    lo = jnp.where(valid, jnp.clip(starts[e] - blk * r, 0, r), 0)
    hi = jnp.where(valid, jnp.clip(ends[e] - blk * r, 0, r), 0)
    i32 = lambda a: a.astype(jnp.int32)
    return (i32(blk), i32(e), i32(lo), i32(hi), i32(n_vis).reshape(1),
            slot_tok.reshape(nblk, 1, r), out_row.reshape(nblk, 1, r))


def _layer(xa, xb, pa, pb, groups, lb, mix_norm_w, w_in, hgrn_norm_w, q_norm_w, k_norm_w, w_out, ffn_norm_w,
           router_w, router_b, w_gate_up, b_gate_up, w_down, b_down, ple_proj, ple_norm_w, ple_gate):
    n = xa.shape[0] + xb.shape[0]
    proj = _in_proj(xa, xb, mix_norm_w, w_in.astype(BF16))
    att, k, v = _qk_rope(proj, q_norm_w, k_norm_w, groups)
    sums, masks = _hgrn_constants()
    o_dirs = _hgrn(proj, lb, sums, masks, groups)
    tok0 = 0
    for n_tok, t in groups:
        att = _attention(att, k, v, tok0, n_tok // t, t)
        tok0 += n_tok
    h1, xn3, eidx, gates, counts = _out_router(
        xa, xb, o_dirs, proj, att, hgrn_norm_w, w_out.astype(BF16), ffn_norm_w, router_w, router_b)
    y4 = _moe_experts(*_routing(eidx, counts, n), xn3, w_gate_up, b_gate_up, w_down, b_down)
    return _final(y4, h1, gates, pa, pb, ple_proj.astype(BF16), ple_norm_w, ple_gate.astype(BF16))


def kernel(x_prompt, x_sample, p_prompt, p_sample, mix_norm_w, w_in, hgrn_lb, hgrn_norm_w, q_norm_w, k_norm_w, w_out, ffn_norm_w, router_w, router_b, w_gate_up, b_gate_up, w_down, b_down, ple_proj, ple_norm_w, ple_gate):
    depth = w_in.shape[0]
    groups = ((x_prompt.shape[0] * x_prompt.shape[1], x_prompt.shape[1]),
              (x_sample.shape[0] * x_sample.shape[1], x_sample.shape[1]))
    d = x_prompt.shape[-1]
    ha, hb = x_prompt.reshape(-1, d), x_sample.reshape(-1, d)
    lb_all = jnp.cumsum(jax.nn.softmax(hgrn_lb.astype(F32), axis=0), axis=0)
    for i in range(depth):
        ha, hb = _layer(ha, hb, p_prompt[i].reshape(-1, PLE_DIM), p_sample[i].reshape(-1, PLE_DIM), groups,
                        lb_all[i], mix_norm_w[i], w_in[i], hgrn_norm_w[i], q_norm_w[i], k_norm_w[i],
                        w_out[i], ffn_norm_w[i], router_w[i], router_b[i], w_gate_up[i], b_gate_up[i],
                        w_down[i], b_down[i], ple_proj[i], ple_norm_w[i], ple_gate[i])
    return ha.reshape(x_prompt.shape), hb.reshape(x_sample.shape)
```

```python
import functools

import numpy as np
import jax
import jax.numpy as jnp
from jax import lax
from jax.experimental import pallas as pl
from jax.experimental.pallas import tpu as pltpu

F32 = jnp.float32
BF16 = jnp.bfloat16

D_MODEL = 2048
GRID_W = 64
HG_WIDTH = 1024
HG_HEAD_DIM = 128
HG_HEADS = HG_WIDTH // HG_HEAD_DIM
CHUNK = 64
HEAD_DIM = 128
N_Q_HEADS = 8
N_KV_HEADS = 2
Q_PER_KV = N_Q_HEADS // N_KV_HEADS
ROPE_THETA = 10000.0
ROPE_AXIS_PAIRS = HEAD_DIM // 4
N_EXPERTS = 32
TOP_K = 4
D_FF = D_MODEL
SWIGLU_LIMIT = 7.0
SWIGLU_ALPHA = 1.702
PLE_DIM = 256
NORM_EPS = 1e-6
Q_COLS = N_Q_HEADS * HEAD_DIM
KV_COLS = N_KV_HEADS * HEAD_DIM
IN_COLS = 5 * HG_WIDTH + Q_COLS + 2 * KV_COLS

LANES = 128
V7X_VMEM_BUDGET_BYTES = 56 * 1024 * 1024

IN_TM, IN_TN = 1024, 512
QK_TM = 512
ATT_TQ = 256
ATT_TK = 1024
HG_TB = 512
HG_UNROLL = 4
OUT_TM = 256
MOE_R = 512
MOE_SUB = 128
MOE_TF = 512
FIN_TM = 256
ROW_CHUNK = 128
N_LEVELS = 6


def _params(sem, vmem_bytes):
    return pltpu.CompilerParams(dimension_semantics=sem,
                                vmem_limit_bytes=min(int(vmem_bytes), V7X_VMEM_BUDGET_BYTES))


def _rms_scale(x):
    return lax.rsqrt(jnp.mean(x * x, axis=-1, keepdims=True) + NORM_EPS)


def _dot_nt(a, b):
    return lax.dot_general(a, b, (((1,), (1,)), ((), ())), preferred_element_type=F32)


def _pair_specs(block, n_first):
    zeros = (0,) * (len(block) - 1)
    return [pl.BlockSpec(block, lambda i, *_: (jnp.minimum(i, n_first - 1),) + zeros),
            pl.BlockSpec(block, lambda i, *_: (jnp.maximum(i - n_first, 0),) + zeros)]


def _in_proj_kernel(xa_ref, xb_ref, nw_ref, w_ref, o_ref, xn_ref, *, n_first):
    @pl.when(pl.program_id(1) == 0)
    def _():
        first = pl.program_id(0) < n_first

        def body(c, carry):
            rows = pl.ds(pl.multiple_of(c * ROW_CHUNK, ROW_CHUNK), ROW_CHUNK)
            x = jnp.where(first, xa_ref[rows, :], xb_ref[rows, :])
            xn_ref[rows, :] = (x * _rms_scale(x) * nw_ref[...]).astype(BF16)
            return carry
        lax.fori_loop(0, xa_ref.shape[0] // ROW_CHUNK, body, 0)

    o_ref[...] = jnp.dot(xn_ref[...], w_ref[...], preferred_element_type=F32)


def _in_proj(xa, xb, norm_w, w_bf16):
    d = xa.shape[1]
    n = xa.shape[0] + xb.shape[0]
    cols = w_bf16.shape[1]
    tm, tn = IN_TM, IN_TN
    vmem = 4 * tm * d * 4 + tm * d * 2 + 2 * d * tn * 2 + 2 * tm * tn * 4 + (8 << 20)
    return pl.pallas_call(
        functools.partial(_in_proj_kernel, n_first=xa.shape[0] // tm),
        grid=(n // tm, cols // tn),
        in_specs=_pair_specs((tm, d), xa.shape[0] // tm) + [
            pl.BlockSpec((1, d), lambda i, j: (0, 0)),
            pl.BlockSpec((d, tn), lambda i, j: (0, j))],
        out_specs=pl.BlockSpec((tm, tn), lambda i, j: (i, j)),
        out_shape=jax.ShapeDtypeStruct((n, cols), F32),
        scratch_shapes=[pltpu.VMEM((tm, d), BF16)],
        compiler_params=_params(("parallel", "arbitrary"), vmem),
        name="in_proj",
    )(xa, xb, norm_w.reshape(1, d), w_bf16)


def _rope_tables(t_max):
    rows = t_max // GRID_W
    row = jnp.repeat(jnp.arange(rows, dtype=F32), GRID_W)
    col = jnp.tile(jnp.arange(GRID_W, dtype=F32), rows)
    inv_freq = ROPE_THETA ** (-jnp.arange(ROPE_AXIS_PAIRS, dtype=F32) / ROPE_AXIS_PAIRS)
    ang = jnp.concatenate([row[:, None] * inv_freq, col[:, None] * inv_freq], axis=-1)
    cos = jnp.repeat(jnp.cos(ang), 2, axis=-1)
    sin = jnp.repeat(jnp.sin(ang), 2, axis=-1)
    even = (jnp.arange(HEAD_DIM) % 2 == 0)[None, :]
    s_next = jnp.where(even, -sin, 0.0)
    s_prev = jnp.where(even, 0.0, sin)
    return cos, s_next, s_prev


def _qk_rope_kernel(q_ref, k_ref, v_ref, c_ref, sn_ref, sp_ref, qw_ref, kw_ref,
                    qo_ref, ko_ref, vo_ref):
    c, sn, sp = c_ref[...], sn_ref[...], sp_ref[...]

    def norm_rope(x, w):
        y = x * _rms_scale(x) * w
        return y * c + pltpu.roll(y, HEAD_DIM - 1, 1) * sn + pltpu.roll(y, 1, 1) * sp

    scale = HEAD_DIM ** -0.5
    for h in range(N_Q_HEADS):
        sl = slice(h * HEAD_DIM, (h + 1) * HEAD_DIM)
        qo_ref[:, sl] = (norm_rope(q_ref[:, sl], qw_ref[...]) * scale).astype(BF16)
    for h in range(N_KV_HEADS):
        sl = slice(h * HEAD_DIM, (h + 1) * HEAD_DIM)
        ko_ref[:, sl] = norm_rope(k_ref[:, sl], kw_ref[...]).astype(BF16)
    vo_ref[...] = v_ref[...].astype(BF16)


def _qk_rope(proj, q_norm_w, k_norm_w, groups):
    n = proj.shape[0]
    tm = QK_TM
    t_max = max(t for _, t in groups)
    cos, s_next, s_prev = _rope_tables(t_max)

    def pos_block(i):
        blk = jnp.int32(0)
        start = 0
        for n_tok, t in groups:
            first = start // tm
            blk = jnp.where(i >= first, (i - first) % (t // tm), blk)
            start += n_tok
        return blk

    tab_spec = pl.BlockSpec((tm, HEAD_DIM), lambda i: (pos_block(i), 0))
    q_blk, k_blk, v_blk = (5 * HG_WIDTH) // Q_COLS, (5 * HG_WIDTH + Q_COLS) // KV_COLS, \
        (5 * HG_WIDTH + Q_COLS + KV_COLS) // KV_COLS
    return pl.pallas_call(
        _qk_rope_kernel,
        grid=(n // tm,),
        in_specs=[pl.BlockSpec((tm, Q_COLS), lambda i: (i, q_blk)),
                  pl.BlockSpec((tm, KV_COLS), lambda i: (i, k_blk)),
                  pl.BlockSpec((tm, KV_COLS), lambda i: (i, v_blk)),
                  tab_spec, tab_spec, tab_spec,
                  pl.BlockSpec((1, HEAD_DIM), lambda i: (0, 0)),
                  pl.BlockSpec((1, HEAD_DIM), lambda i: (0, 0))],
        out_specs=[pl.BlockSpec((tm, Q_COLS), lambda i: (i, 0)),
                   pl.BlockSpec((tm, KV_COLS), lambda i: (i, 0)),
                   pl.BlockSpec((tm, KV_COLS), lambda i: (i, 0))],
        out_shape=[jax.ShapeDtypeStruct((n, Q_COLS), BF16),
                   jax.ShapeDtypeStruct((n, KV_COLS), BF16),
                   jax.ShapeDtypeStruct((n, KV_COLS), BF16)],
        compiler_params=_params(("parallel",), 32 << 20),
        name="qk_rope",
    )(proj, proj, proj, cos, s_next, s_prev,
      q_norm_w.reshape(1, HEAD_DIM), k_norm_w.reshape(1, HEAD_DIM))


def _attn_kernel(q_ref, k_ref, v_ref, o_ref):
    n_kv = k_ref.shape[0] // ATT_TK
    items = [(h, c) for h in range(Q_PER_KV) for c in range(n_kv)]

    def scores(item):
        h, c = item
        return _dot_nt(q_ref[:, h * HEAD_DIM:(h + 1) * HEAD_DIM], k_ref[c * ATT_TK:(c + 1) * ATT_TK, :])

    s = scores(items[0])
    m = l = acc = None
    for i, (h, c) in enumerate(items):
        s_next = scores(items[i + 1]) if i + 1 < len(items) else None
        vc = v_ref[c * ATT_TK:(c + 1) * ATT_TK, :]
        mc = jnp.max(s, axis=-1, keepdims=True)
        if c == 0:
            m = mc
            p = jnp.exp(s - m)
            l = jnp.sum(p, axis=-1, keepdims=True)
            acc = jnp.dot(p.astype(BF16), vc, preferred_element_type=F32)
        else:
            m_new = jnp.maximum(m, mc)
            alpha = jnp.exp(m - m_new)
            p = jnp.exp(s - m_new)
            l = alpha * l + jnp.sum(p, axis=-1, keepdims=True)
            acc = alpha * acc + jnp.dot(p.astype(BF16), vc, preferred_element_type=F32)
            m = m_new
        if c == n_kv - 1:
            o_ref[:, h * HEAD_DIM:(h + 1) * HEAD_DIM] = (acc / l).astype(BF16)
        s = s_next


def _attention(q, k, v, tok0, batch, t):
    tq = ATT_TQ
    nq = t // tq
    row0, seq0 = tok0 // tq, tok0 // t
    gcols = Q_PER_KV * HEAD_DIM
    vmem = 4 * t * HEAD_DIM * 2 + 3 * tq * t * 4 + (8 << 20)
    q_spec = pl.BlockSpec((tq, gcols), lambda b, g, i: (row0 + b * nq + i, g))
    kv_spec = pl.BlockSpec((t, HEAD_DIM), lambda b, g, i: (seq0 + b, g))
    return pl.pallas_call(
        _attn_kernel,
        grid=(batch, N_KV_HEADS, nq),
        in_specs=[q_spec, kv_spec, kv_spec],
        out_specs=q_spec,
        out_shape=jax.ShapeDtypeStruct(q.shape, BF16),
        input_output_aliases={0: 0},
        compiler_params=_params(("parallel", "parallel", "arbitrary"), vmem),
        name="attention",
    )(q, k, v)


def _hgrn_constants():
    c = CHUNK
    t = np.arange(c)[:, None]
    u = np.arange(c)[None, :]
    sums = [(u <= t), (u > t)]
    masks = []
    for lvl in range(N_LEVELS):
        half = (c // 2) >> lvl
        bound = (t // (2 * half)) * (2 * half) + half - 1
        late = t > bound
        sums.append(np.where(late, (u > bound) & (u <= t), (u > t) & (u <= bound)))
        same = (t // (2 * half)) == (u // (2 * half))
        masks.append(same & late & (u <= bound))
    masks.append(t == u)
    sums = np.stack(sums).astype(np.float32)
    masks = np.stack(masks).astype(np.float32)
    both_s = np.stack([sums, sums[:, ::-1, ::-1]]).reshape(2, -1, c)
    both_s = np.concatenate([both_s, both_s], axis=-1)
    both_m = np.stack([masks, masks[:, ::-1, ::-1]]).reshape(2, -1, c)
    return jnp.asarray(both_s, BF16), jnp.asarray(both_m, F32)


def _hgrn_kernel(q_ref, f_ref, v_ref, lb_ref, sums_ref, masks_ref, o_ref, st_ref, *, nchunk, nblk,
                 seq_first_blk, seq_last_blk):
    d = pl.program_id(0)
    i = pl.program_id(1)
    blk = i + d * (nblk - 1 - 2 * i)
    fresh = functools.reduce(
        jnp.logical_or,
        [jnp.logical_or(jnp.logical_and(d == 0, blk == f), jnp.logical_and(d == 1, blk == l))
         for f, l in zip(seq_first_blk, seq_last_blk)])

    @pl.when(fresh)
    def _():
        st_ref[...] = jnp.zeros_like(st_ref)

    lb = lb_ref[0]
    sums = sums_ref[0]
    c = CHUNK

    def chunk_body(j, carry):
        jj = j + d * (nchunk - 1 - 2 * j)
        rows = pl.ds(pl.multiple_of(jj * c, c), c)
        f = lb + (1.0 - lb) * jax.nn.sigmoid(f_ref[rows, :])
        k = 1.0 - f
        lf = jnp.log(f)
        qpre = q_ref[rows, :]
        q = qpre * jax.nn.sigmoid(qpre)
        vb = v_ref[rows, :].astype(BF16)
        lf_hi = lf.astype(BF16)
        lf_lo = (lf - lf_hi.astype(F32)).astype(BF16)
        e = jnp.exp(jnp.dot(sums, jnp.concatenate([lf_hi, lf_lo], axis=0),
                            preferred_element_type=F32))
        g = jnp.exp(jnp.sum(lf, axis=0, keepdims=True))
        qt = (q * e[0:c]).astype(BF16)
        kd = (k * e[c:2 * c]).astype(BF16)
        ql = [q.astype(BF16)] + [(q * e[(2 + lvl) * c:(3 + lvl) * c]).astype(BF16) for lvl in range(N_LEVELS)]
        kl = [k.astype(BF16)] + [(k * e[(2 + lvl) * c:(3 + lvl) * c]).astype(BF16) for lvl in range(N_LEVELS)]
        mk = [masks_ref[0, N_LEVELS * c:(N_LEVELS + 1) * c, :]] + [
            masks_ref[0, lvl * c:(lvl + 1) * c, :] for lvl in range(N_LEVELS)]
        heads = [slice(h * HG_HEAD_DIM, (h + 1) * HG_HEAD_DIM) for h in range(HG_HEADS)]
        scores = []
        for sl in heads:
            a = mk[0] * _dot_nt(ql[0][:, sl], kl[0][:, sl])
            for lvl in range(1, N_LEVELS + 1):
                a = a + mk[lvl] * _dot_nt(ql[lvl][:, sl], kl[lvl][:, sl])
            scores.append(a.astype(BF16))
        for h, sl in enumerate(heads):
            st = st_ref[h]
            o = _dot_nt(qt[:, sl], st.astype(BF16)) + jnp.dot(scores[h], vb[:, sl], preferred_element_type=F32)
            ut = lax.dot_general(vb[:, sl], kd[:, sl], (((0,), (0,)), ((), ())), preferred_element_type=F32)
            st_ref[h] = st * g[:, sl] + ut
            o_ref[0, rows, sl] = o
        return carry

    lax.fori_loop(0, nchunk, chunk_body, 0, unroll=HG_UNROLL)


def _hgrn(proj, lb, sums, masks, groups):
    n = proj.shape[0]
    tb = HG_TB
    nblk = n // tb
    first_blk, last_blk, tok = [], [], 0
    for n_tok, t in groups:
        for s in range(n_tok // t):
            first_blk.append((tok + s * t) // tb)
            last_blk.append((tok + (s + 1) * t) // tb - 1)
        tok += n_tok

    def rows(d, i):
        return i + d * (nblk - 1 - 2 * i)

    in_blk = lambda col: pl.BlockSpec((tb, HG_WIDTH), lambda d, i: (rows(d, i), col(d)))
    return pl.pallas_call(
        functools.partial(_hgrn_kernel, nchunk=tb // CHUNK, nblk=nblk,
                          seq_first_blk=tuple(first_blk), seq_last_blk=tuple(last_blk)),
        grid=(2, nblk),
        in_specs=[in_blk(lambda d: 0), in_blk(lambda d: 1 + d), in_blk(lambda d: 3),
                  pl.BlockSpec((1, 1, HG_WIDTH), lambda d, i: (d, 0, 0)),
                  pl.BlockSpec((1,) + sums.shape[1:], lambda d, i: (d, 0, 0)),
                  pl.BlockSpec((1,) + masks.shape[1:], lambda d, i: (d, 0, 0))],
        out_specs=pl.BlockSpec((1, tb, HG_WIDTH), lambda d, i: (d, rows(d, i), 0)),
        out_shape=jax.ShapeDtypeStruct((2, n, HG_WIDTH), F32),
        scratch_shapes=[pltpu.VMEM((HG_HEADS, HG_HEAD_DIM, HG_HEAD_DIM), F32)],
        compiler_params=_params(("parallel", "arbitrary"), 40 << 20),
        name="hgrn2",
    )(proj, proj, proj, lb.reshape(2, 1, HG_WIDTH), sums, masks)


def _out_router_kernel(xa_ref, xb_ref, of_ref, ob_ref, g_ref, att_ref, hw_ref, wo_ref, fw_ref, rw_ref, rb_ref,
                       h_ref, xn_ref, eidx_ref, gate_ref, cnt_ref, cnt_acc, *, n_first):
    @pl.when(pl.program_id(0) == 0)
    def _():
        cnt_acc[...] = jnp.zeros_like(cnt_acc)

    o = of_ref[0] + ob_ref[0]
    parts = []
    for h in range(HG_HEADS):
        sl = slice(h * HG_HEAD_DIM, (h + 1) * HG_HEAD_DIM)
        oh = o[:, sl]
        gh = g_ref[:, sl]
        parts.append(((oh * _rms_scale(oh) * hw_ref[...]) * (gh * jax.nn.sigmoid(gh))).astype(BF16))
    oa = jnp.concatenate(parts, axis=1)
    h1 = (jnp.where(pl.program_id(0) < n_first, xa_ref[...], xb_ref[...])
          + jnp.dot(oa, wo_ref[0:HG_WIDTH, :], preferred_element_type=F32)
          + jnp.dot(att_ref[...], wo_ref[HG_WIDTH:, :], preferred_element_type=F32))
    h_ref[...] = h1
    xn = h1 * _rms_scale(h1) * fw_ref[...]
    for cb in range(xn_ref.shape[1]):
        xn_ref[:, cb, :] = xn[:, cb * LANES:(cb + 1) * LANES]

    xn_hi = xn.astype(BF16)
    xn_lo = (xn - xn_hi.astype(F32)).astype(BF16)
    logits = (jnp.dot(xn_hi, rw_ref[0], preferred_element_type=F32)
              + (jnp.dot(xn_lo, rw_ref[0], preferred_element_type=F32)
                 + jnp.dot(xn_hi, rw_ref[1], preferred_element_type=F32))) + rb_ref[...]
    tm = logits.shape[0]
    lane = lax.broadcasted_iota(jnp.int32, (tm, LANES), 1)
    work = logits
    vals, idxs = [], []
    for _ in range(TOP_K):
        m = jnp.max(work, axis=-1, keepdims=True)
        idx = jnp.min(jnp.where(work == m, lane, LANES), axis=-1, keepdims=True)
        vals.append(m)
        idxs.append(idx)
        work = jnp.where(lane == idx, -jnp.inf, work)
    exps = [jnp.exp(v - vals[0]) for v in vals]
    denom = exps[0] + exps[1] + exps[2] + exps[3]

    onehot = jnp.zeros((tm, LANES), F32)
    for idx in idxs:
        onehot = onehot + (lane == idx).astype(F32)
    eidx = jnp.zeros((tm, LANES), jnp.int32)
    gate = jnp.zeros((tm, LANES), F32)
    for kk in range(TOP_K):
        eidx = jnp.where(lane == kk, idxs[kk], eidx)
        gate = jnp.where(lane == kk, exps[kk] / denom, gate)
    eidx_ref[...] = eidx
    gate_ref[...] = gate
    cnt_acc[...] = cnt_acc[...] + jnp.sum(onehot, axis=0, keepdims=True)
    cnt_ref[...] = cnt_acc[...]


def _out_router(xa, xb, o_dirs, proj, att, hgrn_norm_w, w_out_bf16, ffn_norm_w, router_w, router_b):
    d = xa.shape[1]
    n = xa.shape[0] + xb.shape[0]
    tm = OUT_TM
    rw = jnp.zeros((d, LANES), F32).at[:, :N_EXPERTS].set(router_w.astype(F32))
    rw_hi = rw.astype(BF16)
    rw = jnp.stack([rw_hi, (rw - rw_hi.astype(F32)).astype(BF16)])
    rb = jnp.full((1, LANES), -1e30, F32).at[0, :N_EXPERTS].set(router_b.astype(F32))
    g_blk = 4
    row = lambda i: (i, 0)
    const = lambda i: (0, 0)
    vmem = 2 * d * d * 2 + 12 * tm * d * 4 + (12 << 20)
    return pl.pallas_call(
        functools.partial(_out_router_kernel, n_first=xa.shape[0] // tm),
        grid=(n // tm,),
        in_specs=_pair_specs((tm, d), xa.shape[0] // tm) + [
                  pl.BlockSpec((1, tm, HG_WIDTH), lambda i: (0, i, 0)),
                  pl.BlockSpec((1, tm, HG_WIDTH), lambda i: (1, i, 0)),
                  pl.BlockSpec((tm, HG_WIDTH), lambda i: (i, g_blk)),
                  pl.BlockSpec((tm, Q_COLS), row),
                  pl.BlockSpec((1, HG_HEAD_DIM), const),
                  pl.BlockSpec((d, d), const),
                  pl.BlockSpec((1, d), const),
                  pl.BlockSpec((2, d, LANES), lambda i: (0, 0, 0)),
                  pl.BlockSpec((1, LANES), const)],
        out_specs=[pl.BlockSpec((tm, d), row), pl.BlockSpec((tm, d // LANES, LANES), lambda i: (i, 0, 0)),
                   pl.BlockSpec((tm, LANES), row), pl.BlockSpec((tm, LANES), row),
                   pl.BlockSpec((1, LANES), const)],
        out_shape=[jax.ShapeDtypeStruct((n, d), F32), jax.ShapeDtypeStruct((n, d // LANES, LANES), F32),
                   jax.ShapeDtypeStruct((n, LANES), jnp.int32),
                   jax.ShapeDtypeStruct((n, LANES), F32), jax.ShapeDtypeStruct((1, LANES), F32)],
        scratch_shapes=[pltpu.VMEM((1, LANES), F32)],
        compiler_params=_params(("arbitrary",), vmem),
        name="out_router",
    )(xa, xb, o_dirs, o_dirs, proj, att, hgrn_norm_w.reshape(1, HG_HEAD_DIM), w_out_bf16,
      ffn_norm_w.reshape(1, d), rw, rb)


ROW_PITCH = 20
GATHER_UNROLL = 8
ROW_COPY_PRIORITY = 1


def _start_row_gather(idx_ref, n_rows, src_hbm, buf, sem):
    slabs = src_hbm.shape[1]

    def issue(g, carry):
        base = pl.multiple_of(g * GATHER_UNROLL, GATHER_UNROLL)
        for u in range(GATHER_UNROLL):
            r = base + u
            pltpu.make_async_copy(src_hbm.at[idx_ref[0, 0, r]], buf.at[pl.ds(r * ROW_PITCH, slabs), :],
                                  sem).start(priority=ROW_COPY_PRIORITY)
        return carry
    lax.fori_loop(0, n_rows // GATHER_UNROLL, issue, 0)


def _wait_row_gather(n_rows, slabs, buf, sem):
    done = buf.at[pl.ds(0, n_rows * slabs), :]
    pltpu.make_async_copy(done, done, sem).wait()


def _gathered_cols(buf, row0, n_rows, cb):
    return buf[pl.ds(row0 * ROW_PITCH + cb, n_rows, stride=ROW_PITCH), :]


def _start_row_scatter(idx_ref, src, dst_hbm, sem):
    def issue(g, carry):
        base = pl.multiple_of(g * GATHER_UNROLL, GATHER_UNROLL)
        for u in range(GATHER_UNROLL):
            r = base + u
            pltpu.make_async_copy(src.at[pl.ds(r, 1), :], dst_hbm.at[pl.ds(idx_ref[0, 0, r], 1), :],
                                  sem).start(priority=ROW_COPY_PRIORITY)
        return carry
    lax.fori_loop(0, src.shape[0] // GATHER_UNROLL, issue, 0)


def _wait_row_scatter(src, sem):
    pltpu.make_async_copy(src, src, sem).wait()


def _moe_kernel(vb_ref, ve_ref, vlo_ref, vhi_ref, nvis_ref,
                tok_ref, tok_next_ref, orow_ref, xn_hbm, wg_ref, wl_ref, bg_ref, bl_ref, wd_ref, bd_ref,
                y_hbm, raw, xb, acc, sem_in, sem_out, *, nblk):
    w = pl.program_id(0)
    j = pl.program_id(1)
    nf = pl.num_programs(1)
    r = xb.shape[0]
    slabs = xn_hbm.shape[1]
    valid = w < nvis_ref[0]
    blk, lo, hi = vb_ref[w], vlo_ref[w], vhi_ref[w]
    slot = lax.rem(blk, 2)
    acc_blk = acc.at[slot]

    @pl.when(jnp.logical_and(valid, jnp.logical_and(lo == 0, j == 0)))
    def _():
        @pl.when(w == 0)
        def _():
            _start_row_gather(tok_ref, r, xn_hbm, raw, sem_in)
        _wait_row_gather(r, slabs, raw, sem_in)
        for cb in range(slabs):
            xb[:, cb * LANES:(cb + 1) * LANES] = _gathered_cols(raw, 0, r, cb).astype(BF16)

        @pl.when(blk + 1 < nblk)
        def _():
            _start_row_gather(tok_next_ref, r, xn_hbm, raw, sem_in)

        @pl.when(blk >= 2)
        def _():
            _wait_row_scatter(acc_blk, sem_out.at[slot])
        acc_blk[...] = jnp.zeros(acc_blk.shape, F32)

    @pl.when(valid)
    def _():
        sub_lo = lax.shift_right_logical(lo, MOE_SUB.bit_length() - 1)
        sub_hi = lax.shift_right_logical(hi + (MOE_SUB - 1), MOE_SUB.bit_length() - 1)
        n_sub = r // MOE_SUB
        for s0 in range(n_sub):
            for s1 in range(s0 + 1, n_sub + 1):
                rows = slice(s0 * MOE_SUB, s1 * MOE_SUB)

                @pl.when(jnp.logical_and(sub_lo == s0, sub_hi == s1))
                def _(rows=rows):
                    m = rows.stop - rows.start
                    xs = xb[rows, :]
                    gl = jnp.dot(xs, wg_ref[0].astype(BF16), preferred_element_type=F32) + bg_ref[0]
                    lin = jnp.dot(xs, wl_ref[0].astype(BF16), preferred_element_type=F32) + bl_ref[0]
                    gl = jnp.minimum(gl, SWIGLU_LIMIT)
                    lin = jnp.clip(lin, -SWIGLU_LIMIT, SWIGLU_LIMIT)
                    act = (lin + 1.0) * (gl * jax.nn.sigmoid(SWIGLU_ALPHA * gl))
                    rid = rows.start + lax.broadcasted_iota(jnp.int32, (m, 1), 0)
                    mine = jnp.logical_and(rid >= lo, rid < hi).astype(F32)
                    acc_blk[rows, :] += jnp.dot((act * mine).astype(BF16), wd_ref[0].astype(BF16),
                                                preferred_element_type=F32)

                    @pl.when(j == 0)
                    def _():
                        acc_blk[rows, :] += mine * bd_ref[0]

        @pl.when(jnp.logical_and(hi == r, j == nf - 1))
        def _():
            _start_row_scatter(orow_ref, acc_blk, y_hbm, sem_out.at[slot])

    @pl.when(jnp.logical_and(w == pl.num_programs(0) - 1, j == nf - 1))
    def _():
        for s in range(min(2, nblk)):
            _wait_row_scatter(acc.at[s], sem_out.at[s])


def _moe_experts(vis_blk, vis_exp, vis_lo, vis_hi, n_vis, slot_tok, out_row, xn3,
                 w_gate_up, b_gate_up, w_down, b_down):
    r, tf = MOE_R, MOE_TF
    n, slabs, _ = xn3.shape
    d = slabs * LANES
    nblk = slot_tok.shape[0]
    nf = D_FF // tf

    def tile(w, j, nv):
        w_eff = jnp.minimum(w, nv[0] - 1)
        j_eff = jnp.where(w < nv[0], j, nf - 1)
        return jnp.where(w_eff % 2 == 0, j_eff, nf - 1 - j_eff)

    idx_spec = lambda f: pl.BlockSpec((1, 1, r), lambda w, j, vb, ve, lo, hi, nv: (f(vb[w]), 0, 0),
                                      memory_space=pltpu.SMEM)
    vmem = (r * ROW_PITCH * LANES * 4 + r * d * 2 + 2 * r * d * 4 + 2 * 3 * d * tf * 4 + 3 * d * tf * 2
            + 3 * MOE_SUB * d * 4 + (6 << 20))
    return pl.pallas_call(
        functools.partial(_moe_kernel, nblk=nblk),
        grid_spec=pltpu.PrefetchScalarGridSpec(
            num_scalar_prefetch=5,
            grid=(vis_blk.shape[0], nf),
            in_specs=[idx_spec(lambda b: b), idx_spec(lambda b: jnp.minimum(b + 1, nblk - 1)), idx_spec(lambda b: b),
                      pl.BlockSpec(memory_space=pl.ANY),
                      pl.BlockSpec((1, d, tf), lambda w, j, vb, ve, lo, hi, nv: (ve[w], 0, tile(w, j, nv))),
                      pl.BlockSpec((1, d, tf), lambda w, j, vb, ve, lo, hi, nv: (ve[w], 0, nf + tile(w, j, nv))),
                      pl.BlockSpec((1, 1, tf), lambda w, j, vb, ve, lo, hi, nv: (ve[w], 0, tile(w, j, nv))),
                      pl.BlockSpec((1, 1, tf), lambda w, j, vb, ve, lo, hi, nv: (ve[w], 0, nf + tile(w, j, nv))),
                      pl.BlockSpec((1, tf, d), lambda w, j, vb, ve, lo, hi, nv: (ve[w], tile(w, j, nv), 0)),
                      pl.BlockSpec((1, 1, d), lambda w, j, vb, ve, lo, hi, nv: (ve[w], 0, 0))],
            out_specs=pl.BlockSpec(memory_space=pl.ANY),
            scratch_shapes=[pltpu.VMEM((r * ROW_PITCH, LANES), F32),
                            pltpu.VMEM((r, d), BF16),
                            pltpu.VMEM((2, r, d), F32),
                            pltpu.SemaphoreType.DMA, pltpu.SemaphoreType.DMA((2,))]),
        out_shape=jax.ShapeDtypeStruct((TOP_K * n, d), F32),
        compiler_params=_params(("arbitrary", "arbitrary"), vmem),
        name="moe_experts",
    )(vis_blk, vis_exp, vis_lo, vis_hi, n_vis, slot_tok, slot_tok, out_row, xn3,
      w_gate_up, w_gate_up, b_gate_up.reshape(N_EXPERTS, 1, 2 * D_FF),
      b_gate_up.reshape(N_EXPERTS, 1, 2 * D_FF), w_down, b_down.reshape(N_EXPERTS, 1, d))


def _final_kernel(h_ref, gate_ref, y0_ref, y1_ref, y2_ref, y3_ref, pa_ref, pb_ref, pp_ref, pnw_ref, pg_ref,
                  oa_ref, ob_ref, *, n_first):
    first = pl.program_id(0) < n_first
    h = h_ref[...]
    for kk, y_ref in enumerate((y0_ref, y1_ref, y2_ref, y3_ref)):
        h = h + gate_ref[:, kk:kk + 1] * y_ref[...]
    gate = jax.nn.sigmoid(jnp.dot(h.astype(BF16), pg_ref[...], preferred_element_type=F32))
    p = jnp.where(first, pa_ref[...], pb_ref[...])
    pe = jnp.dot(p.astype(BF16), pp_ref[...], preferred_element_type=F32)
    pe = pe * _rms_scale(pe) * pnw_ref[...]
    out = h + gate * pe

    @pl.when(first)
    def _():
        oa_ref[...] = out

    @pl.when(jnp.logical_not(first))
    def _():
        ob_ref[...] = out


def _final(y4, h1, gates, pa, pb, ple_proj_bf16, ple_norm_w, ple_gate_bf16):
    n, d = h1.shape
    tm = FIN_TM
    nt = n // tm
    n_first = pa.shape[0] // tm
    row = lambda i: (i, 0)
    const = lambda i: (0, 0)
    vmem = 2 * d * d * 2 + 20 * tm * d * 4 + (8 << 20)
    return pl.pallas_call(
        functools.partial(_final_kernel, n_first=n_first),
        grid=(nt,),
        in_specs=[pl.BlockSpec((tm, d), row),
                  pl.BlockSpec((tm, LANES), row)]
        + [pl.BlockSpec((tm, d), lambda i, kk=kk: (kk * nt + i, 0)) for kk in range(TOP_K)]
        + _pair_specs((tm, PLE_DIM), n_first)
        + [pl.BlockSpec((PLE_DIM, d), const),
           pl.BlockSpec((1, d), const),
           pl.BlockSpec((d, d), const)],
        out_specs=_pair_specs((tm, d), n_first),
        out_shape=[jax.ShapeDtypeStruct((pa.shape[0], d), F32), jax.ShapeDtypeStruct((pb.shape[0], d), F32)],
        compiler_params=_params(("arbitrary",), vmem),
        name="combine_final",
    )(h1, gates, y4, y4, y4, y4, pa, pb, ple_proj_bf16, ple_norm_w.reshape(1, d), ple_gate_bf16)


def _routing(eidx, counts, n_tokens):
    r = MOE_R
    nblk = (n_tokens * TOP_K) // r
    nvis_max = nblk + N_EXPERTS - 1
    order = jnp.argsort(eidx[:, :TOP_K].reshape(-1), stable=True).astype(jnp.int32)
    slot_tok = order // TOP_K
    out_row = (order % TOP_K) * n_tokens + slot_tok
    cnt = counts[0, :N_EXPERTS].astype(jnp.int32)
    ends = jnp.cumsum(cnt)
    starts = ends - cnt
    first_blk = starts // r
    n_blk = jnp.where(cnt > 0, (ends - 1) // r - first_blk + 1, 0)
    vis_end = jnp.cumsum(n_blk)
    n_vis = vis_end[-1]
    w = jnp.arange(nvis_max, dtype=jnp.int32)
    valid = w < n_vis
    e = jnp.minimum(jnp.sum((vis_end[None, :] <= w[:, None]).astype(jnp.int32), axis=1), N_EXPERTS - 1)
    e = jnp.where(valid, e, jnp.max(jnp.where(valid, e, 0)))
    blk = jnp.where(valid, first_blk[e] + w - (vis_end[e] - n_blk[e]), nblk - 1)
    lo = jnp.where(valid, jnp.clip(starts[e] - blk * r, 0, r), 0)
    hi = jnp.where(valid, jnp.clip(ends[e] - blk * r, 0, r), 0)
    i32 = lambda a: a.astype(jnp.int32)
    return (i32(blk), i32(e), i32(lo), i32(hi), i32(n_vis).reshape(1),
            slot_tok.reshape(nblk, 1, r), out_row.reshape(nblk, 1, r))


def _layer(xa, xb, pa, pb, groups, lb, mix_norm_w, w_in, hgrn_norm_w, q_norm_w, k_norm_w, w_out, ffn_norm_w,
           router_w, router_b, w_gate_up, b_gate_up, w_down, b_down, ple_proj, ple_norm_w, ple_gate):
    n = xa.shape[0] + xb.shape[0]
    proj = _in_proj(xa, xb, mix_norm_w, w_in.astype(BF16))
    att, k, v = _qk_rope(proj, q_norm_w, k_norm_w, groups)
    sums, masks = _hgrn_constants()
    o_dirs = _hgrn(proj, lb, sums, masks, groups)
    tok0 = 0
    for n_tok, t in groups:
        att = _attention(att, k, v, tok0, n_tok // t, t)
        tok0 += n_tok
    h1, xn3, eidx, gates, counts = _out_router(
        xa, xb, o_dirs, proj, att, hgrn_norm_w, w_out.astype(BF16), ffn_norm_w, router_w, router_b)
    y4 = _moe_experts(*_routing(eidx, counts, n), xn3, w_gate_up, b_gate_up, w_down, b_down)
    return _final(y4, h1, gates, pa, pb, ple_proj.astype(BF16), ple_norm_w, ple_gate.astype(BF16))


def kernel(x_prompt, x_sample, p_prompt, p_sample, mix_norm_w, w_in, hgrn_lb, hgrn_norm_w, q_norm_w, k_norm_w, w_out, ffn_norm_w, router_w, router_b, w_gate_up, b_gate_up, w_down, b_down, ple_proj, ple_norm_w, ple_gate):
    depth = w_in.shape[0]
    groups = ((x_prompt.shape[0] * x_prompt.shape[1], x_prompt.shape[1]),
              (x_sample.shape[0] * x_sample.shape[1], x_sample.shape[1]))
    d = x_prompt.shape[-1]
    ha, hb = x_prompt.reshape(-1, d), x_sample.reshape(-1, d)
    lb_all = jnp.cumsum(jax.nn.softmax(hgrn_lb.astype(F32), axis=0), axis=0)
    for i in range(depth):
        ha, hb = _layer(ha, hb, p_prompt[i].reshape(-1, PLE_DIM), p_sample[i].reshape(-1, PLE_DIM), groups,
                        lb_all[i], mix_norm_w[i], w_in[i], hgrn_norm_w[i], q_norm_w[i], k_norm_w[i],
                        w_out[i], ffn_norm_w[i], router_w[i], router_b[i], w_gate_up[i], b_gate_up[i],
                        w_down[i], b_down[i], ple_proj[i], ple_norm_w[i], ple_gate[i])
    return ha.reshape(x_prompt.shape), hb.reshape(x_sample.shape)
```

```python
import functools

import numpy as np
import jax
import jax.numpy as jnp
from jax import lax
from jax.experimental import pallas as pl
from jax.experimental.pallas import tpu as pltpu

F32 = jnp.float32
BF16 = jnp.bfloat16

D_MODEL = 2048
GRID_W = 64
HG_WIDTH = 1024
HG_HEAD_DIM = 128
HG_HEADS = HG_WIDTH // HG_HEAD_DIM
CHUNK = 64
HEAD_DIM = 128
N_Q_HEADS = 8
N_KV_HEADS = 2
Q_PER_KV = N_Q_HEADS // N_KV_HEADS
ROPE_THETA = 10000.0
ROPE_AXIS_PAIRS = HEAD_DIM // 4
N_EXPERTS = 32
TOP_K = 4
D_FF = D_MODEL
SWIGLU_LIMIT = 7.0
SWIGLU_ALPHA = 1.702
PLE_DIM = 256
NORM_EPS = 1e-6
Q_COLS = N_Q_HEADS * HEAD_DIM
KV_COLS = N_KV_HEADS * HEAD_DIM
IN_COLS = 5 * HG_WIDTH + Q_COLS + 2 * KV_COLS

LANES = 128
V7X_VMEM_BUDGET_BYTES = 56 * 1024 * 1024

IN_TM, IN_TN = 1024, 512
QK_TM = 512
ATT_TQ = 256
ATT_TK = 512
HG_TB = 512
HG_UNROLL = 4
OUT_TM = 256
MOE_R = 512
MOE_SUB = 128
MOE_TF = 512
FIN_TM = 256
ROW_CHUNK = 128
N_LEVELS = 6


def _params(sem, vmem_bytes):
    return pltpu.CompilerParams(dimension_semantics=sem,
                                vmem_limit_bytes=min(int(vmem_bytes), V7X_VMEM_BUDGET_BYTES))


def _rms_scale(x):
    return lax.rsqrt(jnp.mean(x * x, axis=-1, keepdims=True) + NORM_EPS)


def _dot_nt(a, b):
    return lax.dot_general(a, b, (((1,), (1,)), ((), ())), preferred_element_type=F32)


def _pair_specs(block, n_first):
    zeros = (0,) * (len(block) - 1)
    return [pl.BlockSpec(block, lambda i, *_: (jnp.minimum(i, n_first - 1),) + zeros),
            pl.BlockSpec(block, lambda i, *_: (jnp.maximum(i - n_first, 0),) + zeros)]


def _in_proj_kernel(xa_ref, xb_ref, nw_ref, w_ref, o_ref, xn_ref, *, n_first):
    @pl.when(pl.program_id(1) == 0)
    def _():
        first = pl.program_id(0) < n_first

        def body(c, carry):
            rows = pl.ds(pl.multiple_of(c * ROW_CHUNK, ROW_CHUNK), ROW_CHUNK)
            x = jnp.where(first, xa_ref[rows, :], xb_ref[rows, :])
            xn_ref[rows, :] = (x * _rms_scale(x) * nw_ref[...]).astype(BF16)
            return carry
        lax.fori_loop(0, xa_ref.shape[0] // ROW_CHUNK, body, 0)

    o_ref[...] = jnp.dot(xn_ref[...], w_ref[...], preferred_element_type=F32)


def _in_proj(xa, xb, norm_w, w_bf16):
    d = xa.shape[1]
    n = xa.shape[0] + xb.shape[0]
    cols = w_bf16.shape[1]
    tm, tn = IN_TM, IN_TN
    vmem = 4 * tm * d * 4 + tm * d * 2 + 2 * d * tn * 2 + 2 * tm * tn * 4 + (8 << 20)
    return pl.pallas_call(
        functools.partial(_in_proj_kernel, n_first=xa.shape[0] // tm),
        grid=(n // tm, cols // tn),
        in_specs=_pair_specs((tm, d), xa.shape[0] // tm) + [
            pl.BlockSpec((1, d), lambda i, j: (0, 0)),
            pl.BlockSpec((d, tn), lambda i, j: (0, j))],
        out_specs=pl.BlockSpec((tm, tn), lambda i, j: (i, j)),
        out_shape=jax.ShapeDtypeStruct((n, cols), F32),
        scratch_shapes=[pltpu.VMEM((tm, d), BF16)],
        compiler_params=_params(("parallel", "arbitrary"), vmem),
        name="in_proj",
    )(xa, xb, norm_w.reshape(1, d), w_bf16)


def _rope_tables(t_max):
    rows = t_max // GRID_W
    row = jnp.repeat(jnp.arange(rows, dtype=F32), GRID_W)
    col = jnp.tile(jnp.arange(GRID_W, dtype=F32), rows)
    inv_freq = ROPE_THETA ** (-jnp.arange(ROPE_AXIS_PAIRS, dtype=F32) / ROPE_AXIS_PAIRS)
    ang = jnp.concatenate([row[:, None] * inv_freq, col[:, None] * inv_freq], axis=-1)
    cos = jnp.repeat(jnp.cos(ang), 2, axis=-1)
    sin = jnp.repeat(jnp.sin(ang), 2, axis=-1)
    even = (jnp.arange(HEAD_DIM) % 2 == 0)[None, :]
    s_next = jnp.where(even, -sin, 0.0)
    s_prev = jnp.where(even, 0.0, sin)
    return cos, s_next, s_prev


def _qk_rope_kernel(q_ref, k_ref, v_ref, c_ref, sn_ref, sp_ref, qw_ref, kw_ref,
                    qo_ref, ko_ref, vo_ref):
    c, sn, sp = c_ref[...], sn_ref[...], sp_ref[...]

    def norm_rope(x, w):
        y = x * _rms_scale(x) * w
        return y * c + pltpu.roll(y, HEAD_DIM - 1, 1) * sn + pltpu.roll(y, 1, 1) * sp

    scale = HEAD_DIM ** -0.5
    for h in range(N_Q_HEADS):
        sl = slice(h * HEAD_DIM, (h + 1) * HEAD_DIM)
        qo_ref[:, sl] = (norm_rope(q_ref[:, sl], qw_ref[...]) * scale).astype(BF16)
    for h in range(N_KV_HEADS):
        sl = slice(h * HEAD_DIM, (h + 1) * HEAD_DIM)
        ko_ref[:, sl] = norm_rope(k_ref[:, sl], kw_ref[...]).astype(BF16)
    vo_ref[...] = v_ref[...].astype(BF16)


def _qk_rope(proj, q_norm_w, k_norm_w, groups):
    n = proj.shape[0]
    tm = QK_TM
    t_max = max(t for _, t in groups)
    cos, s_next, s_prev = _rope_tables(t_max)

    def pos_block(i):
        blk = jnp.int32(0)
        start = 0
        for n_tok, t in groups:
            first = start // tm
            blk = jnp.where(i >= first, (i - first) % (t // tm), blk)
            start += n_tok
        return blk

    tab_spec = pl.BlockSpec((tm, HEAD_DIM), lambda i: (pos_block(i), 0))
    q_blk, k_blk, v_blk = (5 * HG_WIDTH) // Q_COLS, (5 * HG_WIDTH + Q_COLS) // KV_COLS, \
        (5 * HG_WIDTH + Q_COLS + KV_COLS) // KV_COLS
    return pl.pallas_call(
        _qk_rope_kernel,
        grid=(n // tm,),
        in_specs=[pl.BlockSpec((tm, Q_COLS), lambda i: (i, q_blk)),
                  pl.BlockSpec((tm, KV_COLS), lambda i: (i, k_blk)),
                  pl.BlockSpec((tm, KV_COLS), lambda i: (i, v_blk)),
                  tab_spec, tab_spec, tab_spec,
                  pl.BlockSpec((1, HEAD_DIM), lambda i: (0, 0)),
                  pl.BlockSpec((1, HEAD_DIM), lambda i: (0, 0))],
        out_specs=[pl.BlockSpec((tm, Q_COLS), lambda i: (i, 0)),
                   pl.BlockSpec((tm, KV_COLS), lambda i: (i, 0)),
                   pl.BlockSpec((tm, KV_COLS), lambda i: (i, 0))],
        out_shape=[jax.ShapeDtypeStruct((n, Q_COLS), BF16),
                   jax.ShapeDtypeStruct((n, KV_COLS), BF16),
                   jax.ShapeDtypeStruct((n, KV_COLS), BF16)],
        compiler_params=_params(("parallel",), 32 << 20),
        name="qk_rope",
    )(proj, proj, proj, cos, s_next, s_prev,
      q_norm_w.reshape(1, HEAD_DIM), k_norm_w.reshape(1, HEAD_DIM))


def _attn_kernel(q_ref, k_ref, v_ref, o_ref):
    tq = q_ref.shape[0]
    n_kv = k_ref.shape[0] // ATT_TK
    heads = [slice(h * HEAD_DIM, (h + 1) * HEAD_DIM) for h in range(Q_PER_KV)]
    q = jnp.concatenate([q_ref[:, sl] for sl in heads], axis=0)

    def scores(c):
        return _dot_nt(q, k_ref[c * ATT_TK:(c + 1) * ATT_TK, :])

    s = scores(0)
    m = l = acc = None
    for c in range(n_kv):
        s_next = scores(c + 1) if c + 1 < n_kv else None
        vc = v_ref[c * ATT_TK:(c + 1) * ATT_TK, :]
        mc = jnp.max(s, axis=-1, keepdims=True)
        if c == 0:
            m = mc
            p = jnp.exp(s - m)
            l = jnp.sum(p, axis=-1, keepdims=True)
            acc = jnp.dot(p.astype(BF16), vc, preferred_element_type=F32)
        else:
            m_new = jnp.maximum(m, mc)
            alpha = jnp.exp(m - m_new)
            p = jnp.exp(s - m_new)
            l = alpha * l + jnp.sum(p, axis=-1, keepdims=True)
            acc = alpha * acc + jnp.dot(p.astype(BF16), vc, preferred_element_type=F32)
            m = m_new
        s = s_next
    o = (acc / l).astype(BF16)
    for h, sl in enumerate(heads):
        o_ref[:, sl] = o[h * tq:(h + 1) * tq, :]


def _attention(q, k, v, tok0, batch, t):
    tq = ATT_TQ
    nq = t // tq
    row0, seq0 = tok0 // tq, tok0 // t
    gcols = Q_PER_KV * HEAD_DIM
    vmem = 4 * t * HEAD_DIM * 2 + 3 * tq * t * 4 + (8 << 20)
    q_spec = pl.BlockSpec((tq, gcols), lambda b, g, i: (row0 + b * nq + i, g))
    kv_spec = pl.BlockSpec((t, HEAD_DIM), lambda b, g, i: (seq0 + b, g))
    return pl.pallas_call(
        _attn_kernel,
        grid=(batch, N_KV_HEADS, nq),
        in_specs=[q_spec, kv_spec, kv_spec],
        out_specs=q_spec,
        out_shape=jax.ShapeDtypeStruct(q.shape, BF16),
        input_output_aliases={0: 0},
        compiler_params=_params(("parallel", "parallel", "arbitrary"), vmem),
        name="attention",
    )(q, k, v)


def _hgrn_constants():
    c = CHUNK
    t = np.arange(c)[:, None]
    u = np.arange(c)[None, :]
    sums = [(u <= t), (u > t)]
    masks = []
    for lvl in range(N_LEVELS):
        half = (c // 2) >> lvl
        bound = (t // (2 * half)) * (2 * half) + half - 1
        late = t > bound
        sums.append(np.where(late, (u > bound) & (u <= t), (u > t) & (u <= bound)))
        same = (t // (2 * half)) == (u // (2 * half))
        masks.append(same & late & (u <= bound))
    masks.append(t == u)
    sums = np.stack(sums).astype(np.float32)
    masks = np.stack(masks).astype(np.float32)
    both_s = np.stack([sums, sums[:, ::-1, ::-1]]).reshape(2, -1, c)
    both_s = np.concatenate([both_s, both_s], axis=-1)
    both_m = np.stack([masks, masks[:, ::-1, ::-1]]).reshape(2, -1, c)
    return jnp.asarray(both_s, BF16), jnp.asarray(both_m, F32)


def _hgrn_kernel(q_ref, f_ref, v_ref, lb_ref, sums_ref, masks_ref, o_ref, st_ref, *, nchunk, nblk,
                 seq_first_blk, seq_last_blk):
    d = pl.program_id(0)
    i = pl.program_id(1)
    blk = i + d * (nblk - 1 - 2 * i)
    fresh = functools.reduce(
        jnp.logical_or,
        [jnp.logical_or(jnp.logical_and(d == 0, blk == f), jnp.logical_and(d == 1, blk == l))
         for f, l in zip(seq_first_blk, seq_last_blk)])

    @pl.when(fresh)
    def _():
        st_ref[...] = jnp.zeros_like(st_ref)

    lb = lb_ref[0]
    sums = sums_ref[0]
    c = CHUNK

    def chunk_body(j, carry):
        jj = j + d * (nchunk - 1 - 2 * j)
        rows = pl.ds(pl.multiple_of(jj * c, c), c)
        f = lb + (1.0 - lb) * jax.nn.sigmoid(f_ref[rows, :])
        k = 1.0 - f
        lf = jnp.log(f)
        qpre = q_ref[rows, :]
        q = qpre * jax.nn.sigmoid(qpre)
        vb = v_ref[rows, :].astype(BF16)
        lf_hi = lf.astype(BF16)
        lf_lo = (lf - lf_hi.astype(F32)).astype(BF16)
        e = jnp.exp(jnp.dot(sums, jnp.concatenate([lf_hi, lf_lo], axis=0),
                            preferred_element_type=F32))
        g = jnp.exp(jnp.sum(lf, axis=0, keepdims=True))
        qt = (q * e[0:c]).astype(BF16)
        kd = (k * e[c:2 * c]).astype(BF16)
        ql = [q.astype(BF16)] + [(q * e[(2 + lvl) * c:(3 + lvl) * c]).astype(BF16) for lvl in range(N_LEVELS)]
        kl = [k.astype(BF16)] + [(k * e[(2 + lvl) * c:(3 + lvl) * c]).astype(BF16) for lvl in range(N_LEVELS)]
        mk = [masks_ref[0, N_LEVELS * c:(N_LEVELS + 1) * c, :]] + [
            masks_ref[0, lvl * c:(lvl + 1) * c, :] for lvl in range(N_LEVELS)]
        heads = [slice(h * HG_HEAD_DIM, (h + 1) * HG_HEAD_DIM) for h in range(HG_HEADS)]
        scores = []
        for sl in heads:
            a = mk[0] * _dot_nt(ql[0][:, sl], kl[0][:, sl])
            for lvl in range(1, N_LEVELS + 1):
                a = a + mk[lvl] * _dot_nt(ql[lvl][:, sl], kl[lvl][:, sl])
            scores.append(a.astype(BF16))
        for h, sl in enumerate(heads):
            st = st_ref[h]
            o = _dot_nt(qt[:, sl], st.astype(BF16)) + jnp.dot(scores[h], vb[:, sl], preferred_element_type=F32)
            ut = lax.dot_general(vb[:, sl], kd[:, sl], (((0,), (0,)), ((), ())), preferred_element_type=F32)
            st_ref[h] = st * g[:, sl] + ut
            o_ref[0, rows, sl] = o
        return carry

    lax.fori_loop(0, nchunk, chunk_body, 0, unroll=HG_UNROLL)


def _hgrn(proj, lb, sums, masks, groups):
    n = proj.shape[0]
    tb = HG_TB
    nblk = n // tb
    first_blk, last_blk, tok = [], [], 0
    for n_tok, t in groups:
        for s in range(n_tok // t):
            first_blk.append((tok + s * t) // tb)
            last_blk.append((tok + (s + 1) * t) // tb - 1)
        tok += n_tok

    def rows(d, i):
        return i + d * (nblk - 1 - 2 * i)

    in_blk = lambda col: pl.BlockSpec((tb, HG_WIDTH), lambda d, i: (rows(d, i), col(d)))
    return pl.pallas_call(
        functools.partial(_hgrn_kernel, nchunk=tb // CHUNK, nblk=nblk,
                          seq_first_blk=tuple(first_blk), seq_last_blk=tuple(last_blk)),
        grid=(2, nblk),
        in_specs=[in_blk(lambda d: 0), in_blk(lambda d: 1 + d), in_blk(lambda d: 3),
                  pl.BlockSpec((1, 1, HG_WIDTH), lambda d, i: (d, 0, 0)),
                  pl.BlockSpec((1,) + sums.shape[1:], lambda d, i: (d, 0, 0)),
                  pl.BlockSpec((1,) + masks.shape[1:], lambda d, i: (d, 0, 0))],
        out_specs=pl.BlockSpec((1, tb, HG_WIDTH), lambda d, i: (d, rows(d, i), 0)),
        out_shape=jax.ShapeDtypeStruct((2, n, HG_WIDTH), F32),
        scratch_shapes=[pltpu.VMEM((HG_HEADS, HG_HEAD_DIM, HG_HEAD_DIM), F32)],
        compiler_params=_params(("parallel", "arbitrary"), 40 << 20),
        name="hgrn2",
    )(proj, proj, proj, lb.reshape(2, 1, HG_WIDTH), sums, masks)


def _out_router_kernel(xa_ref, xb_ref, of_ref, ob_ref, g_ref, att_ref, hw_ref, wo_ref, fw_ref, rw_ref, rb_ref,
                       h_ref, xn_ref, eidx_ref, gate_ref, cnt_ref, cnt_acc, *, n_first):
    @pl.when(pl.program_id(0) == 0)
    def _():
        cnt_acc[...] = jnp.zeros_like(cnt_acc)

    o = of_ref[0] + ob_ref[0]
    parts = []
    for h in range(HG_HEADS):
        sl = slice(h * HG_HEAD_DIM, (h + 1) * HG_HEAD_DIM)
        oh = o[:, sl]
        gh = g_ref[:, sl]
        parts.append(((oh * _rms_scale(oh) * hw_ref[...]) * (gh * jax.nn.sigmoid(gh))).astype(BF16))
    oa = jnp.concatenate(parts, axis=1)
    h1 = (jnp.where(pl.program_id(0) < n_first, xa_ref[...], xb_ref[...])
          + jnp.dot(oa, wo_ref[0:HG_WIDTH, :], preferred_element_type=F32)
          + jnp.dot(att_ref[...], wo_ref[HG_WIDTH:, :], preferred_element_type=F32))
    h_ref[...] = h1
    xn = h1 * _rms_scale(h1) * fw_ref[...]
    for cb in range(xn_ref.shape[1]):
        xn_ref[:, cb, :] = xn[:, cb * LANES:(cb + 1) * LANES]

    xn_hi = xn.astype(BF16)
    xn_lo = (xn - xn_hi.astype(F32)).astype(BF16)
    logits = (jnp.dot(xn_hi, rw_ref[0], preferred_element_type=F32)
              + (jnp.dot(xn_lo, rw_ref[0], preferred_element_type=F32)
                 + jnp.dot(xn_hi, rw_ref[1], preferred_element_type=F32))) + rb_ref[...]
    tm = logits.shape[0]
    lane = lax.broadcasted_iota(jnp.int32, (tm, LANES), 1)
    work = logits
    vals, idxs = [], []
    for _ in range(TOP_K):
        m = jnp.max(work, axis=-1, keepdims=True)
        idx = jnp.min(jnp.where(work == m, lane, LANES), axis=-1, keepdims=True)
        vals.append(m)
        idxs.append(idx)
        work = jnp.where(lane == idx, -jnp.inf, work)
    exps = [jnp.exp(v - vals[0]) for v in vals]
    denom = exps[0] + exps[1] + exps[2] + exps[3]

    onehot = jnp.zeros((tm, LANES), F32)
    for idx in idxs:
        onehot = onehot + (lane == idx).astype(F32)
    eidx = jnp.zeros((tm, LANES), jnp.int32)
    gate = jnp.zeros((tm, LANES), F32)
    for kk in range(TOP_K):
        eidx = jnp.where(lane == kk, idxs[kk], eidx)
        gate = jnp.where(lane == kk, exps[kk] / denom, gate)
    eidx_ref[...] = eidx
    gate_ref[...] = gate
    cnt_acc[...] = cnt_acc[...] + jnp.sum(onehot, axis=0, keepdims=True)
    cnt_ref[...] = cnt_acc[...]


def _out_router(xa, xb, o_dirs, proj, att, hgrn_norm_w, w_out_bf16, ffn_norm_w, router_w, router_b):
    d = xa.shape[1]
    n = xa.shape[0] + xb.shape[0]
    tm = OUT_TM
    rw = jnp.zeros((d, LANES), F32).at[:, :N_EXPERTS].set(router_w.astype(F32))
    rw_hi = rw.astype(BF16)
    rw = jnp.stack([rw_hi, (rw - rw_hi.astype(F32)).astype(BF16)])
    rb = jnp.full((1, LANES), -1e30, F32).at[0, :N_EXPERTS].set(router_b.astype(F32))
    g_blk = 4
    row = lambda i: (i, 0)
    const = lambda i: (0, 0)
    vmem = 2 * d * d * 2 + 12 * tm * d * 4 + (12 << 20)
    return pl.pallas_call(
        functools.partial(_out_router_kernel, n_first=xa.shape[0] // tm),
        grid=(n // tm,),
        in_specs=_pair_specs((tm, d), xa.shape[0] // tm) + [
                  pl.BlockSpec((1, tm, HG_WIDTH), lambda i: (0, i, 0)),
                  pl.BlockSpec((1, tm, HG_WIDTH), lambda i: (1, i, 0)),
                  pl.BlockSpec((tm, HG_WIDTH), lambda i: (i, g_blk)),
                  pl.BlockSpec((tm, Q_COLS), row),
                  pl.BlockSpec((1, HG_HEAD_DIM), const),
                  pl.BlockSpec((d, d), const),
                  pl.BlockSpec((1, d), const),
                  pl.BlockSpec((2, d, LANES), lambda i: (0, 0, 0)),
                  pl.BlockSpec((1, LANES), const)],
        out_specs=[pl.BlockSpec((tm, d), row), pl.BlockSpec((tm, d // LANES, LANES), lambda i: (i, 0, 0)),
                   pl.BlockSpec((tm, LANES), row), pl.BlockSpec((tm, LANES), row),
                   pl.BlockSpec((1, LANES), const)],
        out_shape=[jax.ShapeDtypeStruct((n, d), F32), jax.ShapeDtypeStruct((n, d // LANES, LANES), F32),
                   jax.ShapeDtypeStruct((n, LANES), jnp.int32),
                   jax.ShapeDtypeStruct((n, LANES), F32), jax.ShapeDtypeStruct((1, LANES), F32)],
        scratch_shapes=[pltpu.VMEM((1, LANES), F32)],
        compiler_params=_params(("arbitrary",), vmem),
        name="out_router",
    )(xa, xb, o_dirs, o_dirs, proj, att, hgrn_norm_w.reshape(1, HG_HEAD_DIM), w_out_bf16,
      ffn_norm_w.reshape(1, d), rw, rb)


ROW_PITCH = 20
GATHER_UNROLL = 8
ROW_COPY_PRIORITY = 1


def _start_row_gather(idx_ref, n_rows, src_hbm, buf, sem):
    slabs = src_hbm.shape[1]

    def issue(g, carry):
        base = pl.multiple_of(g * GATHER_UNROLL, GATHER_UNROLL)
        for u in range(GATHER_UNROLL):
            r = base + u
            pltpu.make_async_copy(src_hbm.at[idx_ref[0, 0, r]], buf.at[pl.ds(r * ROW_PITCH, slabs), :],
                                  sem).start(priority=ROW_COPY_PRIORITY)
        return carry
    lax.fori_loop(0, n_rows // GATHER_UNROLL, issue, 0)


def _wait_row_gather(n_rows, slabs, buf, sem):
    done = buf.at[pl.ds(0, n_rows * slabs), :]
    pltpu.make_async_copy(done, done, sem).wait()


def _gathered_cols(buf, row0, n_rows, cb):
    return buf[pl.ds(row0 * ROW_PITCH + cb, n_rows, stride=ROW_PITCH), :]


def _start_row_scatter(idx_ref, src, dst_hbm, sem):
    def issue(g, carry):
        base = pl.multiple_of(g * GATHER_UNROLL, GATHER_UNROLL)
        for u in range(GATHER_UNROLL):
            r = base + u
            pltpu.make_async_copy(src.at[pl.ds(r, 1), :], dst_hbm.at[pl.ds(idx_ref[0, 0, r], 1), :],
                                  sem).start(priority=ROW_COPY_PRIORITY)
        return carry
    lax.fori_loop(0, src.shape[0] // GATHER_UNROLL, issue, 0)


def _wait_row_scatter(src, sem):
    pltpu.make_async_copy(src, src, sem).wait()


def _moe_kernel(vb_ref, ve_ref, vlo_ref, vhi_ref, nvis_ref,
                tok_ref, tok_next_ref, orow_ref, xn_hbm, wg_ref, wl_ref, bg_ref, bl_ref, wd_ref, bd_ref,
                y_hbm, raw, xb, acc, sem_in, sem_out, *, nblk):
    w = pl.program_id(0)
    j = pl.program_id(1)
    nf = pl.num_programs(1)
    r = xb.shape[0]
    slabs = xn_hbm.shape[1]
    valid = w < nvis_ref[0]
    blk, lo, hi = vb_ref[w], vlo_ref[w], vhi_ref[w]
    slot = lax.rem(blk, 2)
    acc_blk = acc.at[slot]

    @pl.when(jnp.logical_and(valid, jnp.logical_and(lo == 0, j == 0)))
    def _():
        @pl.when(w == 0)
        def _():
            _start_row_gather(tok_ref, r, xn_hbm, raw, sem_in)
        _wait_row_gather(r, slabs, raw, sem_in)
        for cb in range(slabs):
            xb[:, cb * LANES:(cb + 1) * LANES] = _gathered_cols(raw, 0, r, cb).astype(BF16)

        @pl.when(blk + 1 < nblk)
        def _():
            _start_row_gather(tok_next_ref, r, xn_hbm, raw, sem_in)

        @pl.when(blk >= 2)
        def _():
            _wait_row_scatter(acc_blk, sem_out.at[slot])
        acc_blk[...] = jnp.zeros(acc_blk.shape, F32)

    @pl.when(valid)
    def _():
        sub_lo = lax.shift_right_logical(lo, MOE_SUB.bit_length() - 1)
        sub_hi = lax.shift_right_logical(hi + (MOE_SUB - 1), MOE_SUB.bit_length() - 1)
        n_sub = r // MOE_SUB
        for s0 in range(n_sub):
            for s1 in range(s0 + 1, n_sub + 1):
                rows = slice(s0 * MOE_SUB, s1 * MOE_SUB)

                @pl.when(jnp.logical_and(sub_lo == s0, sub_hi == s1))
                def _(rows=rows):
                    m = rows.stop - rows.start
                    xs = xb[rows, :]
                    gl = jnp.dot(xs, wg_ref[0].astype(BF16), preferred_element_type=F32) + bg_ref[0]
                    lin = jnp.dot(xs, wl_ref[0].astype(BF16), preferred_element_type=F32) + bl_ref[0]
                    gl = jnp.minimum(gl, SWIGLU_LIMIT)
                    lin = jnp.clip(lin, -SWIGLU_LIMIT, SWIGLU_LIMIT)
                    act = (lin + 1.0) * (gl * jax.nn.sigmoid(SWIGLU_ALPHA * gl))
                    rid = rows.start + lax.broadcasted_iota(jnp.int32, (m, 1), 0)
                    mine = jnp.logical_and(rid >= lo, rid < hi).astype(F32)
                    acc_blk[rows, :] += jnp.dot((act * mine).astype(BF16), wd_ref[0].astype(BF16),
                                                preferred_element_type=F32)

                    @pl.when(j == 0)
                    def _():
                        acc_blk[rows, :] += mine * bd_ref[0]

        @pl.when(jnp.logical_and(hi == r, j == nf - 1))
        def _():
            _start_row_scatter(orow_ref, acc_blk, y_hbm, sem_out.at[slot])

    @pl.when(jnp.logical_and(w == pl.num_programs(0) - 1, j == nf - 1))
    def _():
        for s in range(min(2, nblk)):
            _wait_row_scatter(acc.at[s], sem_out.at[s])


def _moe_experts(vis_blk, vis_exp, vis_lo, vis_hi, n_vis, slot_tok, out_row, xn3,
                 w_gate_up, b_gate_up, w_down, b_down):
    r, tf = MOE_R, MOE_TF
    n, slabs, _ = xn3.shape
    d = slabs * LANES
    nblk = slot_tok.shape[0]
    nf = D_FF // tf

    def tile(w, j, nv):
        w_eff = jnp.minimum(w, nv[0] - 1)
        j_eff = jnp.where(w < nv[0], j, nf - 1)
        return jnp.where(w_eff % 2 == 0, j_eff, nf - 1 - j_eff)

    idx_spec = lambda f: pl.BlockSpec((1, 1, r), lambda w, j, vb, ve, lo, hi, nv: (f(vb[w]), 0, 0),
                                      memory_space=pltpu.SMEM)
    vmem = (r * ROW_PITCH * LANES * 4 + r * d * 2 + 2 * r * d * 4 + 2 * 3 * d * tf * 4 + 3 * d * tf * 2
            + 3 * MOE_SUB * d * 4 + (6 << 20))
    return pl.pallas_call(
        functools.partial(_moe_kernel, nblk=nblk),
        grid_spec=pltpu.PrefetchScalarGridSpec(
            num_scalar_prefetch=5,
            grid=(vis_blk.shape[0], nf),
            in_specs=[idx_spec(lambda b: b), idx_spec(lambda b: jnp.minimum(b + 1, nblk - 1)), idx_spec(lambda b: b),
                      pl.BlockSpec(memory_space=pl.ANY),
                      pl.BlockSpec((1, d, tf), lambda w, j, vb, ve, lo, hi, nv: (ve[w], 0, tile(w, j, nv))),
                      pl.BlockSpec((1, d, tf), lambda w, j, vb, ve, lo, hi, nv: (ve[w], 0, nf + tile(w, j, nv))),
                      pl.BlockSpec((1, 1, tf), lambda w, j, vb, ve, lo, hi, nv: (ve[w], 0, tile(w, j, nv))),
                      pl.BlockSpec((1, 1, tf), lambda w, j, vb, ve, lo, hi, nv: (ve[w], 0, nf + tile(w, j, nv))),
                      pl.BlockSpec((1, tf, d), lambda w, j, vb, ve, lo, hi, nv: (ve[w], tile(w, j, nv), 0)),
                      pl.BlockSpec((1, 1, d), lambda w, j, vb, ve, lo, hi, nv: (ve[w], 0, 0))],
            out_specs=pl.BlockSpec(memory_space=pl.ANY),
            scratch_shapes=[pltpu.VMEM((r * ROW_PITCH, LANES), F32),
                            pltpu.VMEM((r, d), BF16),
                            pltpu.VMEM((2, r, d), F32),
                            pltpu.SemaphoreType.DMA, pltpu.SemaphoreType.DMA((2,))]),
        out_shape=jax.ShapeDtypeStruct((TOP_K * n, d), F32),
        compiler_params=_params(("arbitrary", "arbitrary"), vmem),
        name="moe_experts",
    )(vis_blk, vis_exp, vis_lo, vis_hi, n_vis, slot_tok, slot_tok, out_row, xn3,
      w_gate_up, w_gate_up, b_gate_up.reshape(N_EXPERTS, 1, 2 * D_FF),
      b_gate_up.reshape(N_EXPERTS, 1, 2 * D_FF), w_down, b_down.reshape(N_EXPERTS, 1, d))


def _final_kernel(h_ref, gate_ref, y0_ref, y1_ref, y2_ref, y3_ref, pa_ref, pb_ref, pp_ref, pnw_ref, pg_ref,
                  oa_ref, ob_ref, *, n_first):
    first = pl.program_id(0) < n_first
    h = h_ref[...]
    for kk, y_ref in enumerate((y0_ref, y1_ref, y2_ref, y3_ref)):
        h = h + gate_ref[:, kk:kk + 1] * y_ref[...]
    gate = jax.nn.sigmoid(jnp.dot(h.astype(BF16), pg_ref[...], preferred_element_type=F32))
    p = jnp.where(first, pa_ref[...], pb_ref[...])
    pe = jnp.dot(p.astype(BF16), pp_ref[...], preferred_element_type=F32)
    pe = pe * _rms_scale(pe) * pnw_ref[...]
    out = h + gate * pe

    @pl.when(first)
    def _():
        oa_ref[...] = out

    @pl.when(jnp.logical_not(first))
    def _():
        ob_ref[...] = out


def _final(y4, h1, gates, pa, pb, ple_proj_bf16, ple_norm_w, ple_gate_bf16):
    n, d = h1.shape
    tm = FIN_TM
    nt = n // tm
    n_first = pa.shape[0] // tm
    row = lambda i: (i, 0)
    const = lambda i: (0, 0)
    vmem = 2 * d * d * 2 + 20 * tm * d * 4 + (8 << 20)
    return pl.pallas_call(
        functools.partial(_final_kernel, n_first=n_first),
        grid=(nt,),
        in_specs=[pl.BlockSpec((tm, d), row),
                  pl.BlockSpec((tm, LANES), row)]
        + [pl.BlockSpec((tm, d), lambda i, kk=kk: (kk * nt + i, 0)) for kk in range(TOP_K)]
        + _pair_specs((tm, PLE_DIM), n_first)
        + [pl.BlockSpec((PLE_DIM, d), const),
           pl.BlockSpec((1, d), const),
           pl.BlockSpec((d, d), const)],
        out_specs=_pair_specs((tm, d), n_first),
        out_shape=[jax.ShapeDtypeStruct((pa.shape[0], d), F32), jax.ShapeDtypeStruct((pb.shape[0], d), F32)],
        compiler_params=_params(("arbitrary",), vmem),
        name="combine_final",
    )(h1, gates, y4, y4, y4, y4, pa, pb, ple_proj_bf16, ple_norm_w.reshape(1, d), ple_gate_bf16)


def _routing(eidx, counts, n_tokens):
    r = MOE_R
    nblk = (n_tokens * TOP_K) // r
    nvis_max = nblk + N_EXPERTS - 1
    order = jnp.argsort(eidx[:, :TOP_K].reshape(-1), stable=True).astype(jnp.int32)
    slot_tok = order // TOP_K
    out_row = (order % TOP_K) * n_tokens + slot_tok
    cnt = counts[0, :N_EXPERTS].astype(jnp.int32)
    ends = jnp.cumsum(cnt)
    starts = ends - cnt
    first_blk = starts // r
    n_blk = jnp.where(cnt > 0, (ends - 1) // r - first_blk + 1, 0)
    vis_end = jnp.cumsum(n_blk)
    n_vis = vis_end[-1]
    w = jnp.arange(nvis_max, dtype=jnp.int32)
    valid = w < n_vis
    e = jnp.minimum(jnp.sum((vis_end[None, :] <= w[:, None]).astype(jnp.int32), axis=1), N_EXPERTS - 1)
    e = jnp.where(valid, e, jnp.max(jnp.where(valid, e, 0)))
    blk = jnp.where(valid, first_blk[e] + w - (vis_end[e] - n_blk[e]), nblk - 1)
    lo = jnp.where(valid, jnp.clip(starts[e] - blk * r, 0, r), 0)
    hi = jnp.where(valid, jnp.clip(ends[e] - blk * r, 0, r), 0)
    i32 = lambda a: a.astype(jnp.int32)
    return (i32(blk), i32(e), i32(lo), i32(hi), i32(n_vis).reshape(1),
            slot_tok.reshape(nblk, 1, r), out_row.reshape(nblk, 1, r))


def _layer(xa, xb, pa, pb, groups, lb, mix_norm_w, w_in, hgrn_norm_w, q_norm_w, k_norm_w, w_out, ffn_norm_w,
           router_w, router_b, w_gate_up, b_gate_up, w_down, b_down, ple_proj, ple_norm_w, ple_gate):
    n = xa.shape[0] + xb.shape[0]
    proj = _in_proj(xa, xb, mix_norm_w, w_in.astype(BF16))
    att, k, v = _qk_rope(proj, q_norm_w, k_norm_w, groups)
    sums, masks = _hgrn_constants()
    o_dirs = _hgrn(proj, lb, sums, masks, groups)
    tok0 = 0
    for n_tok, t in groups:
        att = _attention(att, k, v, tok0, n_tok // t, t)
        tok0 += n_tok
    h1, xn3, eidx, gates, counts = _out_router(
        xa, xb, o_dirs, proj, att, hgrn_norm_w, w_out.astype(BF16), ffn_norm_w, router_w, router_b)
    y4 = _moe_experts(*_routing(eidx, counts, n), xn3, w_gate_up, b_gate_up, w_down, b_down)
    return _final(y4, h1, gates, pa, pb, ple_proj.astype(BF16), ple_norm_w, ple_gate.astype(BF16))


def kernel(x_prompt, x_sample, p_prompt, p_sample, mix_norm_w, w_in, hgrn_lb, hgrn_norm_w, q_norm_w, k_norm_w, w_out, ffn_norm_w, router_w, router_b, w_gate_up, b_gate_up, w_down, b_down, ple_proj, ple_norm_w, ple_gate):
    depth = w_in.shape[0]
    groups = ((x_prompt.shape[0] * x_prompt.shape[1], x_prompt.shape[1]),
              (x_sample.shape[0] * x_sample.shape[1], x_sample.shape[1]))
    d = x_prompt.shape[-1]
    ha, hb = x_prompt.reshape(-1, d), x_sample.reshape(-1, d)
    lb_all = jnp.cumsum(jax.nn.softmax(hgrn_lb.astype(F32), axis=0), axis=0)
    for i in range(depth):
        ha, hb = _layer(ha, hb, p_prompt[i].reshape(-1, PLE_DIM), p_sample[i].reshape(-1, PLE_DIM), groups,
                        lb_all[i], mix_norm_w[i], w_in[i], hgrn_norm_w[i], q_norm_w[i], k_norm_w[i],
                        w_out[i], ffn_norm_w[i], router_w[i], router_b[i], w_gate_up[i], b_gate_up[i],
                        w_down[i], b_down[i], ple_proj[i], ple_norm_w[i], ple_gate[i])
    return ha.reshape(x_prompt.shape), hb.reshape(x_sample.shape)
```

```python
import functools

import numpy as np
import jax
import jax.numpy as jnp
from jax import lax
from jax.experimental import pallas as pl
from jax.experimental.pallas import tpu as pltpu

F32 = jnp.float32
BF16 = jnp.bfloat16

D_MODEL = 2048
GRID_W = 64
HG_WIDTH = 1024
HG_HEAD_DIM = 128
HG_HEADS = HG_WIDTH // HG_HEAD_DIM
CHUNK = 64
HEAD_DIM = 128
N_Q_HEADS = 8
N_KV_HEADS = 2
Q_PER_KV = N_Q_HEADS // N_KV_HEADS
ROPE_THETA = 10000.0
ROPE_AXIS_PAIRS = HEAD_DIM // 4
N_EXPERTS = 32
TOP_K = 4
D_FF = D_MODEL
SWIGLU_LIMIT = 7.0
SWIGLU_ALPHA = 1.702
PLE_DIM = 256
NORM_EPS = 1e-6
Q_COLS = N_Q_HEADS * HEAD_DIM
KV_COLS = N_KV_HEADS * HEAD_DIM
IN_COLS = 5 * HG_WIDTH + Q_COLS + 2 * KV_COLS

LANES = 128
V7X_VMEM_BUDGET_BYTES = 56 * 1024 * 1024

IN_TM, IN_TN = 1024, 512
QK_TM = 1024
ATT_TQ = 512
ATT_TK = 1024
HG_TB = 512
HG_UNROLL = 4
OUT_TM = 256
MOE_R = 512
MOE_SUB = 128
MOE_TF = 512
FIN_TM = 256
ROW_CHUNK = 128
N_LEVELS = 6


def _params(sem, vmem_bytes):
    return pltpu.CompilerParams(dimension_semantics=sem,
                                vmem_limit_bytes=min(int(vmem_bytes), V7X_VMEM_BUDGET_BYTES))


def _rms_scale(x):
    return lax.rsqrt(jnp.mean(x * x, axis=-1, keepdims=True) + NORM_EPS)


def _dot_nt(a, b):
    return lax.dot_general(a, b, (((1,), (1,)), ((), ())), preferred_element_type=F32)


def _pair_specs(block, n_first):
    zeros = (0,) * (len(block) - 1)
    return [pl.BlockSpec(block, lambda i, *_: (jnp.minimum(i, n_first - 1),) + zeros),
            pl.BlockSpec(block, lambda i, *_: (jnp.maximum(i - n_first, 0),) + zeros)]


def _in_proj_kernel(xa_ref, xb_ref, nw_ref, w_ref, o_ref, xn_ref, *, n_first):
    @pl.when(pl.program_id(1) == 0)
    def _():
        first = pl.program_id(0) < n_first

        def body(c, carry):
            rows = pl.ds(pl.multiple_of(c * ROW_CHUNK, ROW_CHUNK), ROW_CHUNK)
            x = jnp.where(first, xa_ref[rows, :], xb_ref[rows, :])
            xn_ref[rows, :] = (x * _rms_scale(x) * nw_ref[...]).astype(BF16)
            return carry
        lax.fori_loop(0, xa_ref.shape[0] // ROW_CHUNK, body, 0)

    o_ref[...] = jnp.dot(xn_ref[...], w_ref[...], preferred_element_type=F32)


def _in_proj(xa, xb, norm_w, w_bf16):
    d = xa.shape[1]
    n = xa.shape[0] + xb.shape[0]
    cols = w_bf16.shape[1]
    tm, tn = IN_TM, IN_TN
    vmem = 4 * tm * d * 4 + tm * d * 2 + 2 * d * tn * 2 + 2 * tm * tn * 4 + (8 << 20)
    return pl.pallas_call(
        functools.partial(_in_proj_kernel, n_first=xa.shape[0] // tm),
        grid=(n // tm, cols // tn),
        in_specs=_pair_specs((tm, d), xa.shape[0] // tm) + [
            pl.BlockSpec((1, d), lambda i, j: (0, 0)),
            pl.BlockSpec((d, tn), lambda i, j: (0, j))],
        out_specs=pl.BlockSpec((tm, tn), lambda i, j: (i, j)),
        out_shape=jax.ShapeDtypeStruct((n, cols), F32),
        scratch_shapes=[pltpu.VMEM((tm, d), BF16)],
        compiler_params=_params(("parallel", "arbitrary"), vmem),
        name="in_proj",
    )(xa, xb, norm_w.reshape(1, d), w_bf16)


def _rope_tables(t_max):
    rows = t_max // GRID_W
    row = jnp.repeat(jnp.arange(rows, dtype=F32), GRID_W)
    col = jnp.tile(jnp.arange(GRID_W, dtype=F32), rows)
    inv_freq = ROPE_THETA ** (-jnp.arange(ROPE_AXIS_PAIRS, dtype=F32) / ROPE_AXIS_PAIRS)
    ang = jnp.concatenate([row[:, None] * inv_freq, col[:, None] * inv_freq], axis=-1)
    cos = jnp.repeat(jnp.cos(ang), 2, axis=-1)
    sin = jnp.repeat(jnp.sin(ang), 2, axis=-1)
    even = (jnp.arange(HEAD_DIM) % 2 == 0)[None, :]
    s_next = jnp.where(even, -sin, 0.0)
    s_prev = jnp.where(even, 0.0, sin)
    return cos, s_next, s_prev


def _qk_rope_kernel(q_ref, k_ref, v_ref, c_ref, sn_ref, sp_ref, qw_ref, kw_ref,
                    qo_ref, ko_ref, vo_ref):
    c, sn, sp = c_ref[...], sn_ref[...], sp_ref[...]

    def norm_rope(x, w):
        y = x * _rms_scale(x) * w
        return y * c + pltpu.roll(y, HEAD_DIM - 1, 1) * sn + pltpu.roll(y, 1, 1) * sp

    scale = HEAD_DIM ** -0.5
    for h in range(N_Q_HEADS):
        sl = slice(h * HEAD_DIM, (h + 1) * HEAD_DIM)
        qo_ref[:, sl] = (norm_rope(q_ref[:, sl], qw_ref[...]) * scale).astype(BF16)
    for h in range(N_KV_HEADS):
        sl = slice(h * HEAD_DIM, (h + 1) * HEAD_DIM)
        ko_ref[:, sl] = norm_rope(k_ref[:, sl], kw_ref[...]).astype(BF16)
    vo_ref[...] = v_ref[...].astype(BF16)


def _qk_rope(proj, q_norm_w, k_norm_w, groups):
    n = proj.shape[0]
    tm = QK_TM
    t_max = max(t for _, t in groups)
    cos, s_next, s_prev = _rope_tables(t_max)

    def pos_block(i):
        blk = jnp.int32(0)
        start = 0
        for n_tok, t in groups:
            first = start // tm
            blk = jnp.where(i >= first, (i - first) % (t // tm), blk)
            start += n_tok
        return blk

    tab_spec = pl.BlockSpec((tm, HEAD_DIM), lambda i: (pos_block(i), 0))
    q_blk, k_blk, v_blk = (5 * HG_WIDTH) // Q_COLS, (5 * HG_WIDTH + Q_COLS) // KV_COLS, \
        (5 * HG_WIDTH + Q_COLS + KV_COLS) // KV_COLS
    return pl.pallas_call(
        _qk_rope_kernel,
        grid=(n // tm,),
        in_specs=[pl.BlockSpec((tm, Q_COLS), lambda i: (i, q_blk)),
                  pl.BlockSpec((tm, KV_COLS), lambda i: (i, k_blk)),
                  pl.BlockSpec((tm, KV_COLS), lambda i: (i, v_blk)),
                  tab_spec, tab_spec, tab_spec,
                  pl.BlockSpec((1, HEAD_DIM), lambda i: (0, 0)),
                  pl.BlockSpec((1, HEAD_DIM), lambda i: (0, 0))],
        out_specs=[pl.BlockSpec((tm, Q_COLS), lambda i: (i, 0)),
                   pl.BlockSpec((tm, KV_COLS), lambda i: (i, 0)),
                   pl.BlockSpec((tm, KV_COLS), lambda i: (i, 0))],
        out_shape=[jax.ShapeDtypeStruct((n, Q_COLS), BF16),
                   jax.ShapeDtypeStruct((n, KV_COLS), BF16),
                   jax.ShapeDtypeStruct((n, KV_COLS), BF16)],
        compiler_params=_params(("parallel",), 32 << 20),
        name="qk_rope",
    )(proj, proj, proj, cos, s_next, s_prev,
      q_norm_w.reshape(1, HEAD_DIM), k_norm_w.reshape(1, HEAD_DIM))


def _attn_kernel(q_ref, k_ref, v_ref, o_ref):
    n_kv = k_ref.shape[0] // ATT_TK
    items = [(h, c) for h in range(Q_PER_KV) for c in range(n_kv)]

    def scores(item):
        h, c = item
        return _dot_nt(q_ref[:, h * HEAD_DIM:(h + 1) * HEAD_DIM], k_ref[c * ATT_TK:(c + 1) * ATT_TK, :])

    s = scores(items[0])
    m = l = acc = None
    for i, (h, c) in enumerate(items):
        s_next = scores(items[i + 1]) if i + 1 < len(items) else None
        vc = v_ref[c * ATT_TK:(c + 1) * ATT_TK, :]
        mc = jnp.max(s, axis=-1, keepdims=True)
        if c == 0:
            m = mc
            p = jnp.exp(s - m)
            l = jnp.sum(p, axis=-1, keepdims=True)
            acc = jnp.dot(p.astype(BF16), vc, preferred_element_type=F32)
        else:
            m_new = jnp.maximum(m, mc)
            alpha = jnp.exp(m - m_new)
            p = jnp.exp(s - m_new)
            l = alpha * l + jnp.sum(p, axis=-1, keepdims=True)
            acc = alpha * acc + jnp.dot(p.astype(BF16), vc, preferred_element_type=F32)
            m = m_new
        if c == n_kv - 1:
            o_ref[:, h * HEAD_DIM:(h + 1) * HEAD_DIM] = (acc / l).astype(BF16)
        s = s_next


def _attention(q, k, v, tok0, batch, t):
    tq = ATT_TQ
    nq = t // tq
    row0, seq0 = tok0 // tq, tok0 // t
    gcols = Q_PER_KV * HEAD_DIM
    vmem = 4 * t * HEAD_DIM * 2 + 3 * tq * t * 4 + (8 << 20)
    q_spec = pl.BlockSpec((tq, gcols), lambda b, g, i: (row0 + b * nq + i, g))
    kv_spec = pl.BlockSpec((t, HEAD_DIM), lambda b, g, i: (seq0 + b, g))
    return pl.pallas_call(
        _attn_kernel,
        grid=(batch, N_KV_HEADS, nq),
        in_specs=[q_spec, kv_spec, kv_spec],
        out_specs=q_spec,
        out_shape=jax.ShapeDtypeStruct(q.shape, BF16),
        input_output_aliases={0: 0},
        compiler_params=_params(("parallel", "parallel", "arbitrary"), vmem),
        name="attention",
    )(q, k, v)


def _hgrn_constants():
    c = CHUNK
    t = np.arange(c)[:, None]
    u = np.arange(c)[None, :]
    sums = [(u <= t), (u > t)]
    masks = []
    for lvl in range(N_LEVELS):
        half = (c // 2) >> lvl
        bound = (t // (2 * half)) * (2 * half) + half - 1
        late = t > bound
        sums.append(np.where(late, (u > bound) & (u <= t), (u > t) & (u <= bound)))
        same = (t // (2 * half)) == (u // (2 * half))
        masks.append(same & late & (u <= bound))
    masks.append(t == u)
    sums = np.stack(sums).astype(np.float32)
    masks = np.stack(masks).astype(np.float32)
    both_s = np.stack([sums, sums[:, ::-1, ::-1]]).reshape(2, -1, c)
    both_s = np.concatenate([both_s, both_s], axis=-1)
    both_m = np.stack([masks, masks[:, ::-1, ::-1]]).reshape(2, -1, c)
    return jnp.asarray(both_s, BF16), jnp.asarray(both_m, F32)


def _hgrn_kernel(q_ref, f_ref, v_ref, lb_ref, sums_ref, masks_ref, o_ref, st_ref, *, nchunk, nblk,
                 seq_first_blk, seq_last_blk):
    d = pl.program_id(0)
    i = pl.program_id(1)
    blk = i + d * (nblk - 1 - 2 * i)
    fresh = functools.reduce(
        jnp.logical_or,
        [jnp.logical_or(jnp.logical_and(d == 0, blk == f), jnp.logical_and(d == 1, blk == l))
         for f, l in zip(seq_first_blk, seq_last_blk)])

    @pl.when(fresh)
    def _():
        st_ref[...] = jnp.zeros_like(st_ref)

    lb = lb_ref[0]
    sums = sums_ref[0]
    c = CHUNK

    def chunk_body(j, carry):
        jj = j + d * (nchunk - 1 - 2 * j)
        rows = pl.ds(pl.multiple_of(jj * c, c), c)
        f = lb + (1.0 - lb) * jax.nn.sigmoid(f_ref[rows, :])
        k = 1.0 - f
        lf = jnp.log(f)
        qpre = q_ref[rows, :]
        q = qpre * jax.nn.sigmoid(qpre)
        vb = v_ref[rows, :].astype(BF16)
        lf_hi = lf.astype(BF16)
        lf_lo = (lf - lf_hi.astype(F32)).astype(BF16)
        e = jnp.exp(jnp.dot(sums, jnp.concatenate([lf_hi, lf_lo], axis=0),
                            preferred_element_type=F32))
        g = jnp.exp(jnp.sum(lf, axis=0, keepdims=True))
        qt = (q * e[0:c]).astype(BF16)
        kd = (k * e[c:2 * c]).astype(BF16)
        ql = [q.astype(BF16)] + [(q * e[(2 + lvl) * c:(3 + lvl) * c]).astype(BF16) for lvl in range(N_LEVELS)]
        kl = [k.astype(BF16)] + [(k * e[(2 + lvl) * c:(3 + lvl) * c]).astype(BF16) for lvl in range(N_LEVELS)]
        mk = [masks_ref[0, N_LEVELS * c:(N_LEVELS + 1) * c, :]] + [
            masks_ref[0, lvl * c:(lvl + 1) * c, :] for lvl in range(N_LEVELS)]
        heads = [slice(h * HG_HEAD_DIM, (h + 1) * HG_HEAD_DIM) for h in range(HG_HEADS)]
        scores = []
        for sl in heads:
            a = mk[0] * _dot_nt(ql[0][:, sl], kl[0][:, sl])
            for lvl in range(1, N_LEVELS + 1):
                a = a + mk[lvl] * _dot_nt(ql[lvl][:, sl], kl[lvl][:, sl])
            scores.append(a.astype(BF16))
        for h, sl in enumerate(heads):
            st = st_ref[h]
            o = _dot_nt(qt[:, sl], st.astype(BF16)) + jnp.dot(scores[h], vb[:, sl], preferred_element_type=F32)
            ut = lax.dot_general(vb[:, sl], kd[:, sl], (((0,), (0,)), ((), ())), preferred_element_type=F32)
            st_ref[h] = st * g[:, sl] + ut
            o_ref[0, rows, sl] = o
        return carry

    lax.fori_loop(0, nchunk, chunk_body, 0, unroll=HG_UNROLL)


def _hgrn(proj, lb, sums, masks, groups):
    n = proj.shape[0]
    tb = HG_TB
    nblk = n // tb
    first_blk, last_blk, tok = [], [], 0
    for n_tok, t in groups:
        for s in range(n_tok // t):
            first_blk.append((tok + s * t) // tb)
            last_blk.append((tok + (s + 1) * t) // tb - 1)
        tok += n_tok

    def rows(d, i):
        return i + d * (nblk - 1 - 2 * i)

    in_blk = lambda col: pl.BlockSpec((tb, HG_WIDTH), lambda d, i: (rows(d, i), col(d)))
    return pl.pallas_call(
        functools.partial(_hgrn_kernel, nchunk=tb // CHUNK, nblk=nblk,
                          seq_first_blk=tuple(first_blk), seq_last_blk=tuple(last_blk)),
        grid=(2, nblk),
        in_specs=[in_blk(lambda d: 0), in_blk(lambda d: 1 + d), in_blk(lambda d: 3),
                  pl.BlockSpec((1, 1, HG_WIDTH), lambda d, i: (d, 0, 0)),
                  pl.BlockSpec((1,) + sums.shape[1:], lambda d, i: (d, 0, 0)),
                  pl.BlockSpec((1,) + masks.shape[1:], lambda d, i: (d, 0, 0))],
        out_specs=pl.BlockSpec((1, tb, HG_WIDTH), lambda d, i: (d, rows(d, i), 0)),
        out_shape=jax.ShapeDtypeStruct((2, n, HG_WIDTH), F32),
        scratch_shapes=[pltpu.VMEM((HG_HEADS, HG_HEAD_DIM, HG_HEAD_DIM), F32)],
        compiler_params=_params(("parallel", "arbitrary"), 40 << 20),
        name="hgrn2",
    )(proj, proj, proj, lb.reshape(2, 1, HG_WIDTH), sums, masks)


def _out_router_kernel(xa_ref, xb_ref, of_ref, ob_ref, g_ref, att_ref, hw_ref, wo_ref, fw_ref, rw_ref, rb_ref,
                       h_ref, xn_ref, eidx_ref, gate_ref, cnt_ref, cnt_acc, *, n_first):
    @pl.when(pl.program_id(0) == 0)
    def _():
        cnt_acc[...] = jnp.zeros_like(cnt_acc)

    o = of_ref[0] + ob_ref[0]
    parts = []
    for h in range(HG_HEADS):
        sl = slice(h * HG_HEAD_DIM, (h + 1) * HG_HEAD_DIM)
        oh = o[:, sl]
        gh = g_ref[:, sl]
        parts.append(((oh * _rms_scale(oh) * hw_ref[...]) * (gh * jax.nn.sigmoid(gh))).astype(BF16))
    oa = jnp.concatenate(parts, axis=1)
    h1 = (jnp.where(pl.program_id(0) < n_first, xa_ref[...], xb_ref[...])
          + jnp.dot(oa, wo_ref[0:HG_WIDTH, :], preferred_element_type=F32)
          + jnp.dot(att_ref[...], wo_ref[HG_WIDTH:, :], preferred_element_type=F32))
    h_ref[...] = h1
    xn = h1 * _rms_scale(h1) * fw_ref[...]
    for cb in range(xn_ref.shape[1]):
        xn_ref[:, cb, :] = xn[:, cb * LANES:(cb + 1) * LANES]

    xn_hi = xn.astype(BF16)
    xn_lo = (xn - xn_hi.astype(F32)).astype(BF16)
    logits = (jnp.dot(xn_hi, rw_ref[0], preferred_element_type=F32)
              + (jnp.dot(xn_lo, rw_ref[0], preferred_element_type=F32)
                 + jnp.dot(xn_hi, rw_ref[1], preferred_element_type=F32))) + rb_ref[...]
    tm = logits.shape[0]
    lane = lax.broadcasted_iota(jnp.int32, (tm, LANES), 1)
    work = logits
    vals, idxs = [], []
    for _ in range(TOP_K):
        m = jnp.max(work, axis=-1, keepdims=True)
        idx = jnp.min(jnp.where(work == m, lane, LANES), axis=-1, keepdims=True)
        vals.append(m)
        idxs.append(idx)
        work = jnp.where(lane == idx, -jnp.inf, work)
    exps = [jnp.exp(v - vals[0]) for v in vals]
    denom = exps[0] + exps[1] + exps[2] + exps[3]

    onehot = jnp.zeros((tm, LANES), F32)
    for idx in idxs:
        onehot = onehot + (lane == idx).astype(F32)
    eidx = jnp.zeros((tm, LANES), jnp.int32)
    gate = jnp.zeros((tm, LANES), F32)
    for kk in range(TOP_K):
        eidx = jnp.where(lane == kk, idxs[kk], eidx)
        gate = jnp.where(lane == kk, exps[kk] / denom, gate)
    eidx_ref[...] = eidx
    gate_ref[...] = gate
    cnt_acc[...] = cnt_acc[...] + jnp.sum(onehot, axis=0, keepdims=True)
    cnt_ref[...] = cnt_acc[...]


def _out_router(xa, xb, o_dirs, proj, att, hgrn_norm_w, w_out_bf16, ffn_norm_w, router_w, router_b):
    d = xa.shape[1]
    n = xa.shape[0] + xb.shape[0]
    tm = OUT_TM
    rw = jnp.zeros((d, LANES), F32).at[:, :N_EXPERTS].set(router_w.astype(F32))
    rw_hi = rw.astype(BF16)
    rw = jnp.stack([rw_hi, (rw - rw_hi.astype(F32)).astype(BF16)])
    rb = jnp.full((1, LANES), -1e30, F32).at[0, :N_EXPERTS].set(router_b.astype(F32))
    g_blk = 4
    row = lambda i: (i, 0)
    const = lambda i: (0, 0)
    vmem = 2 * d * d * 2 + 12 * tm * d * 4 + (12 << 20)
    return pl.pallas_call(
        functools.partial(_out_router_kernel, n_first=xa.shape[0] // tm),
        grid=(n // tm,),
        in_specs=_pair_specs((tm, d), xa.shape[0] // tm) + [
                  pl.BlockSpec((1, tm, HG_WIDTH), lambda i: (0, i, 0)),
                  pl.BlockSpec((1, tm, HG_WIDTH), lambda i: (1, i, 0)),
                  pl.BlockSpec((tm, HG_WIDTH), lambda i: (i, g_blk)),
                  pl.BlockSpec((tm, Q_COLS), row),
                  pl.BlockSpec((1, HG_HEAD_DIM), const),
                  pl.BlockSpec((d, d), const),
                  pl.BlockSpec((1, d), const),
                  pl.BlockSpec((2, d, LANES), lambda i: (0, 0, 0)),
                  pl.BlockSpec((1, LANES), const)],
        out_specs=[pl.BlockSpec((tm, d), row), pl.BlockSpec((tm, d // LANES, LANES), lambda i: (i, 0, 0)),
                   pl.BlockSpec((tm, LANES), row), pl.BlockSpec((tm, LANES), row),
                   pl.BlockSpec((1, LANES), const)],
        out_shape=[jax.ShapeDtypeStruct((n, d), F32), jax.ShapeDtypeStruct((n, d // LANES, LANES), F32),
                   jax.ShapeDtypeStruct((n, LANES), jnp.int32),
                   jax.ShapeDtypeStruct((n, LANES), F32), jax.ShapeDtypeStruct((1, LANES), F32)],
        scratch_shapes=[pltpu.VMEM((1, LANES), F32)],
        compiler_params=_params(("arbitrary",), vmem),
        name="out_router",
    )(xa, xb, o_dirs, o_dirs, proj, att, hgrn_norm_w.reshape(1, HG_HEAD_DIM), w_out_bf16,
      ffn_norm_w.reshape(1, d), rw, rb)


ROW_PITCH = 20
GATHER_UNROLL = 8
ROW_COPY_PRIORITY = 1


def _start_row_gather(idx_ref, n_rows, src_hbm, buf, sem):
    slabs = src_hbm.shape[1]

    def issue(g, carry):
        base = pl.multiple_of(g * GATHER_UNROLL, GATHER_UNROLL)
        for u in range(GATHER_UNROLL):
            r = base + u
            pltpu.make_async_copy(src_hbm.at[idx_ref[0, 0, r]], buf.at[pl.ds(r * ROW_PITCH, slabs), :],
                                  sem).start(priority=ROW_COPY_PRIORITY)
        return carry
    lax.fori_loop(0, n_rows // GATHER_UNROLL, issue, 0)


def _wait_row_gather(n_rows, slabs, buf, sem):
    done = buf.at[pl.ds(0, n_rows * slabs), :]
    pltpu.make_async_copy(done, done, sem).wait()


def _gathered_cols(buf, row0, n_rows, cb):
    return buf[pl.ds(row0 * ROW_PITCH + cb, n_rows, stride=ROW_PITCH), :]


def _start_row_scatter(idx_ref, src, dst_hbm, sem):
    def issue(g, carry):
        base = pl.multiple_of(g * GATHER_UNROLL, GATHER_UNROLL)
        for u in range(GATHER_UNROLL):
            r = base + u
            pltpu.make_async_copy(src.at[pl.ds(r, 1), :], dst_hbm.at[pl.ds(idx_ref[0, 0, r], 1), :],
                                  sem).start(priority=ROW_COPY_PRIORITY)
        return carry
    lax.fori_loop(0, src.shape[0] // GATHER_UNROLL, issue, 0)


def _wait_row_scatter(src, sem):
    pltpu.make_async_copy(src, src, sem).wait()


def _moe_kernel(vb_ref, ve_ref, vlo_ref, vhi_ref, nvis_ref,
                tok_ref, tok_next_ref, orow_ref, xn_hbm, wg_ref, wl_ref, bg_ref, bl_ref, wd_ref, bd_ref,
                y_hbm, raw, xb, acc, sem_in, sem_out, *, nblk):
    w = pl.program_id(0)
    j = pl.program_id(1)
    nf = pl.num_programs(1)
    r = xb.shape[0]
    slabs = xn_hbm.shape[1]
    valid = w < nvis_ref[0]
    blk, lo, hi = vb_ref[w], vlo_ref[w], vhi_ref[w]
    slot = lax.rem(blk, 2)
    acc_blk = acc.at[slot]

    @pl.when(jnp.logical_and(valid, jnp.logical_and(lo == 0, j == 0)))
    def _():
        @pl.when(w == 0)
        def _():
            _start_row_gather(tok_ref, r, xn_hbm, raw, sem_in)
        _wait_row_gather(r, slabs, raw, sem_in)
        for cb in range(slabs):
            xb[:, cb * LANES:(cb + 1) * LANES] = _gathered_cols(raw, 0, r, cb).astype(BF16)

        @pl.when(blk + 1 < nblk)
        def _():
            _start_row_gather(tok_next_ref, r, xn_hbm, raw, sem_in)

        @pl.when(blk >= 2)
        def _():
            _wait_row_scatter(acc_blk, sem_out.at[slot])
        acc_blk[...] = jnp.zeros(acc_blk.shape, F32)

    @pl.when(valid)
    def _():
        sub_lo = lax.shift_right_logical(lo, MOE_SUB.bit_length() - 1)
        sub_hi = lax.shift_right_logical(hi + (MOE_SUB - 1), MOE_SUB.bit_length() - 1)
        n_sub = r // MOE_SUB
        for s0 in range(n_sub):
            for s1 in range(s0 + 1, n_sub + 1):
                rows = slice(s0 * MOE_SUB, s1 * MOE_SUB)

                @pl.when(jnp.logical_and(sub_lo == s0, sub_hi == s1))
                def _(rows=rows):
                    m = rows.stop - rows.start
                    xs = xb[rows, :]
                    gl = jnp.dot(xs, wg_ref[0].astype(BF16), preferred_element_type=F32) + bg_ref[0]
                    lin = jnp.dot(xs, wl_ref[0].astype(BF16), preferred_element_type=F32) + bl_ref[0]
                    gl = jnp.minimum(gl, SWIGLU_LIMIT)
                    lin = jnp.clip(lin, -SWIGLU_LIMIT, SWIGLU_LIMIT)
                    act = (lin + 1.0) * (gl * jax.nn.sigmoid(SWIGLU_ALPHA * gl))
                    rid = rows.start + lax.broadcasted_iota(jnp.int32, (m, 1), 0)
                    mine = jnp.logical_and(rid >= lo, rid < hi).astype(F32)
                    acc_blk[rows, :] += jnp.dot((act * mine).astype(BF16), wd_ref[0].astype(BF16),
                                                preferred_element_type=F32)

                    @pl.when(j == 0)
                    def _():
                        acc_blk[rows, :] += mine * bd_ref[0]

        @pl.when(jnp.logical_and(hi == r, j == nf - 1))
        def _():
            _start_row_scatter(orow_ref, acc_blk, y_hbm, sem_out.at[slot])

    @pl.when(jnp.logical_and(w == pl.num_programs(0) - 1, j == nf - 1))
    def _():
        for s in range(min(2, nblk)):
            _wait_row_scatter(acc.at[s], sem_out.at[s])


def _moe_experts(vis_blk, vis_exp, vis_lo, vis_hi, n_vis, slot_tok, out_row, xn3,
                 w_gate_up, b_gate_up, w_down, b_down):
    r, tf = MOE_R, MOE_TF
    n, slabs, _ = xn3.shape
    d = slabs * LANES
    nblk = slot_tok.shape[0]
    nf = D_FF // tf

    def tile(w, j, nv):
        w_eff = jnp.minimum(w, nv[0] - 1)
        j_eff = jnp.where(w < nv[0], j, nf - 1)
        return jnp.where(w_eff % 2 == 0, j_eff, nf - 1 - j_eff)

    idx_spec = lambda f: pl.BlockSpec((1, 1, r), lambda w, j, vb, ve, lo, hi, nv: (f(vb[w]), 0, 0),
                                      memory_space=pltpu.SMEM)
    vmem = (r * ROW_PITCH * LANES * 4 + r * d * 2 + 2 * r * d * 4 + 2 * 3 * d * tf * 4 + 3 * d * tf * 2
            + 3 * MOE_SUB * d * 4 + (6 << 20))
    return pl.pallas_call(
        functools.partial(_moe_kernel, nblk=nblk),
        grid_spec=pltpu.PrefetchScalarGridSpec(
            num_scalar_prefetch=5,
            grid=(vis_blk.shape[0], nf),
            in_specs=[idx_spec(lambda b: b), idx_spec(lambda b: jnp.minimum(b + 1, nblk - 1)), idx_spec(lambda b: b),
                      pl.BlockSpec(memory_space=pl.ANY),
                      pl.BlockSpec((1, d, tf), lambda w, j, vb, ve, lo, hi, nv: (ve[w], 0, tile(w, j, nv))),
                      pl.BlockSpec((1, d, tf), lambda w, j, vb, ve, lo, hi, nv: (ve[w], 0, nf + tile(w, j, nv))),
                      pl.BlockSpec((1, 1, tf), lambda w, j, vb, ve, lo, hi, nv: (ve[w], 0, tile(w, j, nv))),
                      pl.BlockSpec((1, 1, tf), lambda w, j, vb, ve, lo, hi, nv: (ve[w], 0, nf + tile(w, j, nv))),
                      pl.BlockSpec((1, tf, d), lambda w, j, vb, ve, lo, hi, nv: (ve[w], tile(w, j, nv), 0)),
                      pl.BlockSpec((1, 1, d), lambda w, j, vb, ve, lo, hi, nv: (ve[w], 0, 0))],
            out_specs=pl.BlockSpec(memory_space=pl.ANY),
            scratch_shapes=[pltpu.VMEM((r * ROW_PITCH, LANES), F32),
                            pltpu.VMEM((r, d), BF16),
                            pltpu.VMEM((2, r, d), F32),
                            pltpu.SemaphoreType.DMA, pltpu.SemaphoreType.DMA((2,))]),
        out_shape=jax.ShapeDtypeStruct((TOP_K * n, d), F32),
        compiler_params=_params(("arbitrary", "arbitrary"), vmem),
        name="moe_experts",
    )(vis_blk, vis_exp, vis_lo, vis_hi, n_vis, slot_tok, slot_tok, out_row, xn3,
      w_gate_up, w_gate_up, b_gate_up.reshape(N_EXPERTS, 1, 2 * D_FF),
      b_gate_up.reshape(N_EXPERTS, 1, 2 * D_FF), w_down, b_down.reshape(N_EXPERTS, 1, d))


def _final_kernel(h_ref, gate_ref, y0_ref, y1_ref, y2_ref, y3_ref, pa_ref, pb_ref, pp_ref, pnw_ref, pg_ref,
                  oa_ref, ob_ref, *, n_first):
    first = pl.program_id(0) < n_first
    h = h_ref[...]
    for kk, y_ref in enumerate((y0_ref, y1_ref, y2_ref, y3_ref)):
        h = h + gate_ref[:, kk:kk + 1] * y_ref[...]
    gate = jax.nn.sigmoid(jnp.dot(h.astype(BF16), pg_ref[...], preferred_element_type=F32))
    p = jnp.where(first, pa_ref[...], pb_ref[...])
    pe = jnp.dot(p.astype(BF16), pp_ref[...], preferred_element_type=F32)
    pe = pe * _rms_scale(pe) * pnw_ref[...]
    out = h + gate * pe

    @pl.when(first)
    def _():
        oa_ref[...] = out

    @pl.when(jnp.logical_not(first))
    def _():
        ob_ref[...] = out


def _final(y4, h1, gates, pa, pb, ple_proj_bf16, ple_norm_w, ple_gate_bf16):
    n, d = h1.shape
    tm = FIN_TM
    nt = n // tm
    n_first = pa.shape[0] // tm
    row = lambda i: (i, 0)
    const = lambda i: (0, 0)
    vmem = 2 * d * d * 2 + 20 * tm * d * 4 + (8 << 20)
    return pl.pallas_call(
        functools.partial(_final_kernel, n_first=n_first),
        grid=(nt,),
        in_specs=[pl.BlockSpec((tm, d), row),
                  pl.BlockSpec((tm, LANES), row)]
        + [pl.BlockSpec((tm, d), lambda i, kk=kk: (kk * nt + i, 0)) for kk in range(TOP_K)]
        + _pair_specs((tm, PLE_DIM), n_first)
        + [pl.BlockSpec((PLE_DIM, d), const),
           pl.BlockSpec((1, d), const),
           pl.BlockSpec((d, d), const)],
        out_specs=_pair_specs((tm, d), n_first),
        out_shape=[jax.ShapeDtypeStruct((pa.shape[0], d), F32), jax.ShapeDtypeStruct((pb.shape[0], d), F32)],
        compiler_params=_params(("arbitrary",), vmem),
        name="combine_final",
    )(h1, gates, y4, y4, y4, y4, pa, pb, ple_proj_bf16, ple_norm_w.reshape(1, d), ple_gate_bf16)


def _routing(eidx, counts, n_tokens):
    r = MOE_R
    nblk = (n_tokens * TOP_K) // r
    nvis_max = nblk + N_EXPERTS - 1
    order = jnp.argsort(eidx[:, :TOP_K].reshape(-1), stable=True).astype(jnp.int32)
    slot_tok = order // TOP_K
    out_row = (order % TOP_K) * n_tokens + slot_tok
    cnt = counts[0, :N_EXPERTS].astype(jnp.int32)
    ends = jnp.cumsum(cnt)
    starts = ends - cnt
    first_blk = starts // r
    n_blk = jnp.where(cnt > 0, (ends - 1) // r - first_blk + 1, 0)
    vis_end = jnp.cumsum(n_blk)
    n_vis = vis_end[-1]
    w = jnp.arange(nvis_max, dtype=jnp.int32)
    valid = w < n_vis
    e = jnp.minimum(jnp.sum((vis_end[None, :] <= w[:, None]).astype(jnp.int32), axis=1), N_EXPERTS - 1)
    e = jnp.where(valid, e, jnp.max(jnp.where(valid, e, 0)))
    blk = jnp.where(valid, first_blk[e] + w - (vis_end[e] - n_blk[e]), nblk - 1)
    lo = jnp.where(valid, jnp.clip(starts[e] - blk * r, 0, r), 0)
    hi = jnp.where(valid, jnp.clip(ends[e] - blk * r, 0, r), 0)
    i32 = lambda a: a.astype(jnp.int32)
    return (i32(blk), i32(e), i32(lo), i32(hi), i32(n_vis).reshape(1),
            slot_tok.reshape(nblk, 1, r), out_row.reshape(nblk, 1, r))


def _layer(xa, xb, pa, pb, groups, lb, mix_norm_w, w_in, hgrn_norm_w, q_norm_w, k_norm_w, w_out, ffn_norm_w,
           router_w, router_b, w_gate_up, b_gate_up, w_down, b_down, ple_proj, ple_norm_w, ple_gate):
    n = xa.shape[0] + xb.shape[0]
    proj = _in_proj(xa, xb, mix_norm_w, w_in.astype(BF16))
    att, k, v = _qk_rope(proj, q_norm_w, k_norm_w, groups)
    sums, masks = _hgrn_constants()
    o_dirs = _hgrn(proj, lb, sums, masks, groups)
    tok0 = 0
    for n_tok, t in groups:
        att = _attention(att, k, v, tok0, n_tok // t, t)
        tok0 += n_tok
    h1, xn3, eidx, gates, counts = _out_router(
        xa, xb, o_dirs, proj, att, hgrn_norm_w, w_out.astype(BF16), ffn_norm_w, router_w, router_b)
    y4 = _moe_experts(*_routing(eidx, counts, n), xn3, w_gate_up, b_gate_up, w_down, b_down)
    return _final(y4, h1, gates, pa, pb, ple_proj.astype(BF16), ple_norm_w, ple_gate.astype(BF16))


def kernel(x_prompt, x_sample, p_prompt, p_sample, mix_norm_w, w_in, hgrn_lb, hgrn_norm_w, q_norm_w, k_norm_w, w_out, ffn_norm_w, router_w, router_b, w_gate_up, b_gate_up, w_down, b_down, ple_proj, ple_norm_w, ple_gate):
    depth = w_in.shape[0]
    groups = ((x_prompt.shape[0] * x_prompt.shape[1], x_prompt.shape[1]),
              (x_sample.shape[0] * x_sample.shape[1], x_sample.shape[1]))
    d = x_prompt.shape[-1]
    ha, hb = x_prompt.reshape(-1, d), x_sample.reshape(-1, d)
    lb_all = jnp.cumsum(jax.nn.softmax(hgrn_lb.astype(F32), axis=0), axis=0)
    for i in range(depth):
        ha, hb = _layer(ha, hb, p_prompt[i].reshape(-1, PLE_DIM), p_sample[i].reshape(-1, PLE_DIM), groups,
                        lb_all[i], mix_norm_w[i], w_in[i], hgrn_norm_w[i], q_norm_w[i], k_norm_w[i],
                        w_out[i], ffn_norm_w[i], router_w[i], router_b[i], w_gate_up[i], b_gate_up[i],
                        w_down[i], b_down[i], ple_proj[i], ple_norm_w[i], ple_gate[i])
    return ha.reshape(x_prompt.shape), hb.reshape(x_sample.shape)
```

```python
import functools

import numpy as np
import jax
import jax.numpy as jnp
from jax import lax
from jax.experimental import pallas as pl
from jax.experimental.pallas import tpu as pltpu

F32 = jnp.float32
BF16 = jnp.bfloat16

D_MODEL = 2048
GRID_W = 64
HG_WIDTH = 1024
HG_HEAD_DIM = 128
HG_HEADS = HG_WIDTH // HG_HEAD_DIM
CHUNK = 64
HEAD_DIM = 128
N_Q_HEADS = 8
N_KV_HEADS = 2
Q_PER_KV = N_Q_HEADS // N_KV_HEADS
ROPE_THETA = 10000.0
ROPE_AXIS_PAIRS = HEAD_DIM // 4
N_EXPERTS = 32
TOP_K = 4
D_FF = D_MODEL
SWIGLU_LIMIT = 7.0
SWIGLU_ALPHA = 1.702
PLE_DIM = 256
NORM_EPS = 1e-6
Q_COLS = N_Q_HEADS * HEAD_DIM
KV_COLS = N_KV_HEADS * HEAD_DIM
IN_COLS = 5 * HG_WIDTH + Q_COLS + 2 * KV_COLS

LANES = 128
V7X_VMEM_BUDGET_BYTES = 56 * 1024 * 1024

IN_TM, IN_TN = 1024, 512
QK_TM = 1024
ATT_TQ = 1024
ATT_TK = 1024
HG_TB = 512
HG_UNROLL = 4
OUT_TM = 256
MOE_R = 512
MOE_SUB = 128
MOE_TF = 512
FIN_TM = 256
ROW_CHUNK = 128
N_LEVELS = 6


def _params(sem, vmem_bytes):
    return pltpu.CompilerParams(dimension_semantics=sem,
                                vmem_limit_bytes=min(int(vmem_bytes), V7X_VMEM_BUDGET_BYTES))


def _rms_scale(x):
    return lax.rsqrt(jnp.mean(x * x, axis=-1, keepdims=True) + NORM_EPS)


def _dot_nt(a, b):
    return lax.dot_general(a, b, (((1,), (1,)), ((), ())), preferred_element_type=F32)


def _pair_specs(block, n_first):
    zeros = (0,) * (len(block) - 1)
    return [pl.BlockSpec(block, lambda i, *_: (jnp.minimum(i, n_first - 1),) + zeros),
            pl.BlockSpec(block, lambda i, *_: (jnp.maximum(i - n_first, 0),) + zeros)]


def _in_proj_kernel(xa_ref, xb_ref, nw_ref, w_ref, o_ref, xn_ref, *, n_first):
    @pl.when(pl.program_id(1) == 0)
    def _():
        first = pl.program_id(0) < n_first

        def body(c, carry):
            rows = pl.ds(pl.multiple_of(c * ROW_CHUNK, ROW_CHUNK), ROW_CHUNK)
            x = jnp.where(first, xa_ref[rows, :], xb_ref[rows, :])
            xn_ref[rows, :] = (x * _rms_scale(x) * nw_ref[...]).astype(BF16)
            return carry
        lax.fori_loop(0, xa_ref.shape[0] // ROW_CHUNK, body, 0)

    o_ref[...] = jnp.dot(xn_ref[...], w_ref[...], preferred_element_type=F32)


def _in_proj(xa, xb, norm_w, w_bf16):
    d = xa.shape[1]
    n = xa.shape[0] + xb.shape[0]
    cols = w_bf16.shape[1]
    tm, tn = IN_TM, IN_TN
    vmem = 4 * tm * d * 4 + tm * d * 2 + 2 * d * tn * 2 + 2 * tm * tn * 4 + (8 << 20)
    return pl.pallas_call(
        functools.partial(_in_proj_kernel, n_first=xa.shape[0] // tm),
        grid=(n // tm, cols // tn),
        in_specs=_pair_specs((tm, d), xa.shape[0] // tm) + [
            pl.BlockSpec((1, d), lambda i, j: (0, 0)),
            pl.BlockSpec((d, tn), lambda i, j: (0, j))],
        out_specs=pl.BlockSpec((tm, tn), lambda i, j: (i, j)),
        out_shape=jax.ShapeDtypeStruct((n, cols), F32),
        scratch_shapes=[pltpu.VMEM((tm, d), BF16)],
        compiler_params=_params(("parallel", "arbitrary"), vmem),
        name="in_proj",
    )(xa, xb, norm_w.reshape(1, d), w_bf16)


def _rope_tables(t_max):
    rows = t_max // GRID_W
    row = jnp.repeat(jnp.arange(rows, dtype=F32), GRID_W)
    col = jnp.tile(jnp.arange(GRID_W, dtype=F32), rows)
    inv_freq = ROPE_THETA ** (-jnp.arange(ROPE_AXIS_PAIRS, dtype=F32) / ROPE_AXIS_PAIRS)
    ang = jnp.concatenate([row[:, None] * inv_freq, col[:, None] * inv_freq], axis=-1)
    cos = jnp.repeat(jnp.cos(ang), 2, axis=-1)
    sin = jnp.repeat(jnp.sin(ang), 2, axis=-1)
    even = (jnp.arange(HEAD_DIM) % 2 == 0)[None, :]
    s_next = jnp.where(even, -sin, 0.0)
    s_prev = jnp.where(even, 0.0, sin)
    return cos, s_next, s_prev


def _qk_rope_kernel(q_ref, k_ref, v_ref, c_ref, sn_ref, sp_ref, qw_ref, kw_ref,
                    qo_ref, ko_ref, vo_ref):
    c, sn, sp = c_ref[...], sn_ref[...], sp_ref[...]

    def norm_rope(x, w):
        y = x * _rms_scale(x) * w
        return y * c + pltpu.roll(y, HEAD_DIM - 1, 1) * sn + pltpu.roll(y, 1, 1) * sp

    scale = HEAD_DIM ** -0.5
    for h in range(N_Q_HEADS):
        sl = slice(h * HEAD_DIM, (h + 1) * HEAD_DIM)
        qo_ref[:, sl] = (norm_rope(q_ref[:, sl], qw_ref[...]) * scale).astype(BF16)
    for h in range(N_KV_HEADS):
        sl = slice(h * HEAD_DIM, (h + 1) * HEAD_DIM)
        ko_ref[:, sl] = norm_rope(k_ref[:, sl], kw_ref[...]).astype(BF16)
    vo_ref[...] = v_ref[...].astype(BF16)


def _qk_rope(proj, q_norm_w, k_norm_w, groups):
    n = proj.shape[0]
    tm = QK_TM
    t_max = max(t for _, t in groups)
    cos, s_next, s_prev = _rope_tables(t_max)

    def pos_block(i):
        blk = jnp.int32(0)
        start = 0
        for n_tok, t in groups:
            first = start // tm
            blk = jnp.where(i >= first, (i - first) % (t // tm), blk)
            start += n_tok
        return blk

    tab_spec = pl.BlockSpec((tm, HEAD_DIM), lambda i: (pos_block(i), 0))
    q_blk, k_blk, v_blk = (5 * HG_WIDTH) // Q_COLS, (5 * HG_WIDTH + Q_COLS) // KV_COLS, \
        (5 * HG_WIDTH + Q_COLS + KV_COLS) // KV_COLS
    return pl.pallas_call(
        _qk_rope_kernel,
        grid=(n // tm,),
        in_specs=[pl.BlockSpec((tm, Q_COLS), lambda i: (i, q_blk)),
                  pl.BlockSpec((tm, KV_COLS), lambda i: (i, k_blk)),
                  pl.BlockSpec((tm, KV_COLS), lambda i: (i, v_blk)),
                  tab_spec, tab_spec, tab_spec,
                  pl.BlockSpec((1, HEAD_DIM), lambda i: (0, 0)),
                  pl.BlockSpec((1, HEAD_DIM), lambda i: (0, 0))],
        out_specs=[pl.BlockSpec((tm, Q_COLS), lambda i: (i, 0)),
                   pl.BlockSpec((tm, KV_COLS), lambda i: (i, 0)),
                   pl.BlockSpec((tm, KV_COLS), lambda i: (i, 0))],
        out_shape=[jax.ShapeDtypeStruct((n, Q_COLS), BF16),
                   jax.ShapeDtypeStruct((n, KV_COLS), BF16),
                   jax.ShapeDtypeStruct((n, KV_COLS), BF16)],
        compiler_params=_params(("parallel",), 32 << 20),
        name="qk_rope",
    )(proj, proj, proj, cos, s_next, s_prev,
      q_norm_w.reshape(1, HEAD_DIM), k_norm_w.reshape(1, HEAD_DIM))


def _attn_kernel(q_ref, k_ref, v_ref, o_ref):
    n_kv = k_ref.shape[0] // ATT_TK
    items = [(h, c) for h in range(Q_PER_KV) for c in range(n_kv)]

    def scores(item):
        h, c = item
        return _dot_nt(q_ref[:, h * HEAD_DIM:(h + 1) * HEAD_DIM], k_ref[c * ATT_TK:(c + 1) * ATT_TK, :])

    s = scores(items[0])
    m = l = acc = None
    for i, (h, c) in enumerate(items):
        s_next = scores(items[i + 1]) if i + 1 < len(items) else None
        vc = v_ref[c * ATT_TK:(c + 1) * ATT_TK, :]
        mc = jnp.max(s, axis=-1, keepdims=True)
        if c == 0:
            m = mc
            p = jnp.exp(s - m)
            l = jnp.sum(p, axis=-1, keepdims=True)
            acc = jnp.dot(p.astype(BF16), vc, preferred_element_type=F32)
        else:
            m_new = jnp.maximum(m, mc)
            alpha = jnp.exp(m - m_new)
            p = jnp.exp(s - m_new)
            l = alpha * l + jnp.sum(p, axis=-1, keepdims=True)
            acc = alpha * acc + jnp.dot(p.astype(BF16), vc, preferred_element_type=F32)
            m = m_new
        if c == n_kv - 1:
            o_ref[:, h * HEAD_DIM:(h + 1) * HEAD_DIM] = (acc / l).astype(BF16)
        s = s_next


def _attention(q, k, v, tok0, batch, t):
    tq = ATT_TQ
    nq = t // tq
    row0, seq0 = tok0 // tq, tok0 // t
    gcols = Q_PER_KV * HEAD_DIM
    vmem = 4 * t * HEAD_DIM * 2 + 3 * tq * t * 4 + (8 << 20)
    q_spec = pl.BlockSpec((tq, gcols), lambda b, g, i: (row0 + b * nq + i, g))
    kv_spec = pl.BlockSpec((t, HEAD_DIM), lambda b, g, i: (seq0 + b, g))
    return pl.pallas_call(
        _attn_kernel,
        grid=(batch, N_KV_HEADS, nq),
        in_specs=[q_spec, kv_spec, kv_spec],
        out_specs=q_spec,
        out_shape=jax.ShapeDtypeStruct(q.shape, BF16),
        input_output_aliases={0: 0},
        compiler_params=_params(("parallel", "parallel", "arbitrary"), vmem),
        name="attention",
    )(q, k, v)


def _hgrn_constants():
    c = CHUNK
    t = np.arange(c)[:, None]
    u = np.arange(c)[None, :]
    sums = [(u <= t), (u > t)]
    masks = []
    for lvl in range(N_LEVELS):
        half = (c // 2) >> lvl
        bound = (t // (2 * half)) * (2 * half) + half - 1
        late = t > bound
        sums.append(np.where(late, (u > bound) & (u <= t), (u > t) & (u <= bound)))
        same = (t // (2 * half)) == (u // (2 * half))
        masks.append(same & late & (u <= bound))
    masks.append(t == u)
    sums = np.stack(sums).astype(np.float32)
    masks = np.stack(masks).astype(np.float32)
    both_s = np.stack([sums, sums[:, ::-1, ::-1]]).reshape(2, -1, c)
    both_s = np.concatenate([both_s, both_s], axis=-1)
    both_m = np.stack([masks, masks[:, ::-1, ::-1]]).reshape(2, -1, c)
    return jnp.asarray(both_s, BF16), jnp.asarray(both_m, F32)


def _hgrn_kernel(q_ref, f_ref, v_ref, lb_ref, sums_ref, masks_ref, o_ref, st_ref, *, nchunk, nblk,
                 seq_first_blk, seq_last_blk):
    d = pl.program_id(0)
    i = pl.program_id(1)
    blk = i + d * (nblk - 1 - 2 * i)
    fresh = functools.reduce(
        jnp.logical_or,
        [jnp.logical_or(jnp.logical_and(d == 0, blk == f), jnp.logical_and(d == 1, blk == l))
         for f, l in zip(seq_first_blk, seq_last_blk)])

    @pl.when(fresh)
    def _():
        st_ref[...] = jnp.zeros_like(st_ref)

    lb = lb_ref[0]
    sums = sums_ref[0]
    c = CHUNK

    def chunk_body(j, carry):
        jj = j + d * (nchunk - 1 - 2 * j)
        rows = pl.ds(pl.multiple_of(jj * c, c), c)
        f = lb + (1.0 - lb) * jax.nn.sigmoid(f_ref[rows, :])
        k = 1.0 - f
        lf = jnp.log(f)
        qpre = q_ref[rows, :]
        q = qpre * jax.nn.sigmoid(qpre)
        vb = v_ref[rows, :].astype(BF16)
        lf_hi = lf.astype(BF16)
        lf_lo = (lf - lf_hi.astype(F32)).astype(BF16)
        e = jnp.exp(jnp.dot(sums, jnp.concatenate([lf_hi, lf_lo], axis=0),
                            preferred_element_type=F32))
        g = jnp.exp(jnp.sum(lf, axis=0, keepdims=True))
        qt = (q * e[0:c]).astype(BF16)
        kd = (k * e[c:2 * c]).astype(BF16)
        ql = [q.astype(BF16)] + [(q * e[(2 + lvl) * c:(3 + lvl) * c]).astype(BF16) for lvl in range(N_LEVELS)]
        kl = [k.astype(BF16)] + [(k * e[(2 + lvl) * c:(3 + lvl) * c]).astype(BF16) for lvl in range(N_LEVELS)]
        mk = [masks_ref[0, N_LEVELS * c:(N_LEVELS + 1) * c, :]] + [
            masks_ref[0, lvl * c:(lvl + 1) * c, :] for lvl in range(N_LEVELS)]
        heads = [slice(h * HG_HEAD_DIM, (h + 1) * HG_HEAD_DIM) for h in range(HG_HEADS)]
        scores = []
        for sl in heads:
            a = mk[0] * _dot_nt(ql[0][:, sl], kl[0][:, sl])
            for lvl in range(1, N_LEVELS + 1):
                a = a + mk[lvl] * _dot_nt(ql[lvl][:, sl], kl[lvl][:, sl])
            scores.append(a.astype(BF16))
        for h, sl in enumerate(heads):
            st = st_ref[h]
            o = _dot_nt(qt[:, sl], st.astype(BF16)) + jnp.dot(scores[h], vb[:, sl], preferred_element_type=F32)
            ut = lax.dot_general(vb[:, sl], kd[:, sl], (((0,), (0,)), ((), ())), preferred_element_type=F32)
            st_ref[h] = st * g[:, sl] + ut
            o_ref[0, rows, sl] = o
        return carry

    lax.fori_loop(0, nchunk, chunk_body, 0, unroll=HG_UNROLL)


def _hgrn(proj, lb, sums, masks, groups):
    n = proj.shape[0]
    tb = HG_TB
    nblk = n // tb
    first_blk, last_blk, tok = [], [], 0
    for n_tok, t in groups:
        for s in range(n_tok // t):
            first_blk.append((tok + s * t) // tb)
            last_blk.append((tok + (s + 1) * t) // tb - 1)
        tok += n_tok

    def rows(d, i):
        return i + d * (nblk - 1 - 2 * i)

    in_blk = lambda col: pl.BlockSpec((tb, HG_WIDTH), lambda d, i: (rows(d, i), col(d)))
    return pl.pallas_call(
        functools.partial(_hgrn_kernel, nchunk=tb // CHUNK, nblk=nblk,
                          seq_first_blk=tuple(first_blk), seq_last_blk=tuple(last_blk)),
        grid=(2, nblk),
        in_specs=[in_blk(lambda d: 0), in_blk(lambda d: 1 + d), in_blk(lambda d: 3),
                  pl.BlockSpec((1, 1, HG_WIDTH), lambda d, i: (d, 0, 0)),
                  pl.BlockSpec((1,) + sums.shape[1:], lambda d, i: (d, 0, 0)),
                  pl.BlockSpec((1,) + masks.shape[1:], lambda d, i: (d, 0, 0))],
        out_specs=pl.BlockSpec((1, tb, HG_WIDTH), lambda d, i: (d, rows(d, i), 0)),
        out_shape=jax.ShapeDtypeStruct((2, n, HG_WIDTH), F32),
        scratch_shapes=[pltpu.VMEM((HG_HEADS, HG_HEAD_DIM, HG_HEAD_DIM), F32)],
        compiler_params=_params(("parallel", "arbitrary"), 40 << 20),
        name="hgrn2",
    )(proj, proj, proj, lb.reshape(2, 1, HG_WIDTH), sums, masks)


def _out_router_kernel(xa_ref, xb_ref, of_ref, ob_ref, g_ref, att_ref, hw_ref, wo_ref, fw_ref, rw_ref, rb_ref,
                       h_ref, xn_ref, eidx_ref, gate_ref, cnt_ref, cnt_acc, *, n_first):
    @pl.when(pl.program_id(0) == 0)
    def _():
        cnt_acc[...] = jnp.zeros_like(cnt_acc)

    o = of_ref[0] + ob_ref[0]
    parts = []
    for h in range(HG_HEADS):
        sl = slice(h * HG_HEAD_DIM, (h + 1) * HG_HEAD_DIM)
        oh = o[:, sl]
        gh = g_ref[:, sl]
        parts.append(((oh * _rms_scale(oh) * hw_ref[...]) * (gh * jax.nn.sigmoid(gh))).astype(BF16))
    oa = jnp.concatenate(parts, axis=1)
    h1 = (jnp.where(pl.program_id(0) < n_first, xa_ref[...], xb_ref[...])
          + jnp.dot(oa, wo_ref[0:HG_WIDTH, :], preferred_element_type=F32)
          + jnp.dot(att_ref[...], wo_ref[HG_WIDTH:, :], preferred_element_type=F32))
    h_ref[...] = h1
    xn = h1 * _rms_scale(h1) * fw_ref[...]
    for cb in range(xn_ref.shape[1]):
        xn_ref[:, cb, :] = xn[:, cb * LANES:(cb + 1) * LANES]

    xn_hi = xn.astype(BF16)
    xn_lo = (xn - xn_hi.astype(F32)).astype(BF16)
    logits = (jnp.dot(xn_hi, rw_ref[0], preferred_element_type=F32)
              + (jnp.dot(xn_lo, rw_ref[0], preferred_element_type=F32)
                 + jnp.dot(xn_hi, rw_ref[1], preferred_element_type=F32))) + rb_ref[...]
    tm = logits.shape[0]
    lane = lax.broadcasted_iota(jnp.int32, (tm, LANES), 1)
    work = logits
    vals, idxs = [], []
    for _ in range(TOP_K):
        m = jnp.max(work, axis=-1, keepdims=True)
        idx = jnp.min(jnp.where(work == m, lane, LANES), axis=-1, keepdims=True)
        vals.append(m)
        idxs.append(idx)
        work = jnp.where(lane == idx, -jnp.inf, work)
    exps = [jnp.exp(v - vals[0]) for v in vals]
    denom = exps[0] + exps[1] + exps[2] + exps[3]

    onehot = jnp.zeros((tm, LANES), F32)
    for idx in idxs:
        onehot = onehot + (lane == idx).astype(F32)
    eidx = jnp.zeros((tm, LANES), jnp.int32)
    gate = jnp.zeros((tm, LANES), F32)
    for kk in range(TOP_K):
        eidx = jnp.where(lane == kk, idxs[kk], eidx)
        gate = jnp.where(lane == kk, exps[kk] / denom, gate)
    eidx_ref[...] = eidx
    gate_ref[...] = gate
    cnt_acc[...] = cnt_acc[...] + jnp.sum(onehot, axis=0, keepdims=True)
    cnt_ref[...] = cnt_acc[...]


def _out_router(xa, xb, o_dirs, proj, att, hgrn_norm_w, w_out_bf16, ffn_norm_w, router_w, router_b):
    d = xa.shape[1]
    n = xa.shape[0] + xb.shape[0]
    tm = OUT_TM
    rw = jnp.zeros((d, LANES), F32).at[:, :N_EXPERTS].set(router_w.astype(F32))
    rw_hi = rw.astype(BF16)
    rw = jnp.stack([rw_hi, (rw - rw_hi.astype(F32)).astype(BF16)])
    rb = jnp.full((1, LANES), -1e30, F32).at[0, :N_EXPERTS].set(router_b.astype(F32))
    g_blk = 4
    row = lambda i: (i, 0)
    const = lambda i: (0, 0)
    vmem = 2 * d * d * 2 + 12 * tm * d * 4 + (12 << 20)
    return pl.pallas_call(
        functools.partial(_out_router_kernel, n_first=xa.shape[0] // tm),
        grid=(n // tm,),
        in_specs=_pair_specs((tm, d), xa.shape[0] // tm) + [
                  pl.BlockSpec((1, tm, HG_WIDTH), lambda i: (0, i, 0)),
                  pl.BlockSpec((1, tm, HG_WIDTH), lambda i: (1, i, 0)),
                  pl.BlockSpec((tm, HG_WIDTH), lambda i: (i, g_blk)),
                  pl.BlockSpec((tm, Q_COLS), row),
                  pl.BlockSpec((1, HG_HEAD_DIM), const),
                  pl.BlockSpec((d, d), const),
                  pl.BlockSpec((1, d), const),
                  pl.BlockSpec((2, d, LANES), lambda i: (0, 0, 0)),
                  pl.BlockSpec((1, LANES), const)],
        out_specs=[pl.BlockSpec((tm, d), row), pl.BlockSpec((tm, d // LANES, LANES), lambda i: (i, 0, 0)),
                   pl.BlockSpec((tm, LANES), row), pl.BlockSpec((tm, LANES), row),
                   pl.BlockSpec((1, LANES), const)],
        out_shape=[jax.ShapeDtypeStruct((n, d), F32), jax.ShapeDtypeStruct((n, d // LANES, LANES), F32),
                   jax.ShapeDtypeStruct((n, LANES), jnp.int32),
                   jax.ShapeDtypeStruct((n, LANES), F32), jax.ShapeDtypeStruct((1, LANES), F32)],
        scratch_shapes=[pltpu.VMEM((1, LANES), F32)],
        compiler_params=_params(("arbitrary",), vmem),
        name="out_router",
    )(xa, xb, o_dirs, o_dirs, proj, att, hgrn_norm_w.reshape(1, HG_HEAD_DIM), w_out_bf16,
      ffn_norm_w.reshape(1, d), rw, rb)


ROW_PITCH = 20
GATHER_UNROLL = 8
ROW_COPY_PRIORITY = 1


def _start_row_gather(idx_ref, n_rows, src_hbm, buf, sem):
    slabs = src_hbm.shape[1]

    def issue(g, carry):
        base = pl.multiple_of(g * GATHER_UNROLL, GATHER_UNROLL)
        for u in range(GATHER_UNROLL):
            r = base + u
            pltpu.make_async_copy(src_hbm.at[idx_ref[0, 0, r]], buf.at[pl.ds(r * ROW_PITCH, slabs), :],
                                  sem).start(priority=ROW_COPY_PRIORITY)
        return carry
    lax.fori_loop(0, n_rows // GATHER_UNROLL, issue, 0)


def _wait_row_gather(n_rows, slabs, buf, sem):
    done = buf.at[pl.ds(0, n_rows * slabs), :]
    pltpu.make_async_copy(done, done, sem).wait()


def _gathered_cols(buf, row0, n_rows, cb):
    return buf[pl.ds(row0 * ROW_PITCH + cb, n_rows, stride=ROW_PITCH), :]


def _start_row_scatter(idx_ref, src, dst_hbm, sem):
    def issue(g, carry):
        base = pl.multiple_of(g * GATHER_UNROLL, GATHER_UNROLL)
        for u in range(GATHER_UNROLL):
            r = base + u
            pltpu.make_async_copy(src.at[pl.ds(r, 1), :], dst_hbm.at[pl.ds(idx_ref[0, 0, r], 1), :],
                                  sem).start(priority=ROW_COPY_PRIORITY)
        return carry
    lax.fori_loop(0, src.shape[0] // GATHER_UNROLL, issue, 0)


def _wait_row_scatter(src, sem):
    pltpu.make_async_copy(src, src, sem).wait()


def _moe_kernel(vb_ref, ve_ref, vlo_ref, vhi_ref, nvis_ref,
                tok_ref, tok_next_ref, orow_ref, xn_hbm, wg_ref, wl_ref, bg_ref, bl_ref, wd_ref, bd_ref,
                y_hbm, raw, xb, acc, sem_in, sem_out, *, nblk):
    w = pl.program_id(0)
    j = pl.program_id(1)
    nf = pl.num_programs(1)
    r = xb.shape[0]
    slabs = xn_hbm.shape[1]
    valid = w < nvis_ref[0]
    blk, lo, hi = vb_ref[w], vlo_ref[w], vhi_ref[w]
    slot = lax.rem(blk, 2)
    acc_blk = acc.at[slot]

    @pl.when(jnp.logical_and(valid, jnp.logical_and(lo == 0, j == 0)))
    def _():
        @pl.when(w == 0)
        def _():
            _start_row_gather(tok_ref, r, xn_hbm, raw, sem_in)
        _wait_row_gather(r, slabs, raw, sem_in)
        for cb in range(slabs):
            xb[:, cb * LANES:(cb + 1) * LANES] = _gathered_cols(raw, 0, r, cb).astype(BF16)

        @pl.when(blk + 1 < nblk)
        def _():
            _start_row_gather(tok_next_ref, r, xn_hbm, raw, sem_in)

        @pl.when(blk >= 2)
        def _():
            _wait_row_scatter(acc_blk, sem_out.at[slot])
        acc_blk[...] = jnp.zeros(acc_blk.shape, F32)

    @pl.when(valid)
    def _():
        sub_lo = lax.shift_right_logical(lo, MOE_SUB.bit_length() - 1)
        sub_hi = lax.shift_right_logical(hi + (MOE_SUB - 1), MOE_SUB.bit_length() - 1)
        n_sub = r // MOE_SUB
        for s0 in range(n_sub):
            for s1 in range(s0 + 1, n_sub + 1):
                rows = slice(s0 * MOE_SUB, s1 * MOE_SUB)

                @pl.when(jnp.logical_and(sub_lo == s0, sub_hi == s1))
                def _(rows=rows):
                    m = rows.stop - rows.start
                    xs = xb[rows, :]
                    gl = jnp.dot(xs, wg_ref[0].astype(BF16), preferred_element_type=F32) + bg_ref[0]
                    lin = jnp.dot(xs, wl_ref[0].astype(BF16), preferred_element_type=F32) + bl_ref[0]
                    gl = jnp.minimum(gl, SWIGLU_LIMIT)
                    lin = jnp.clip(lin, -SWIGLU_LIMIT, SWIGLU_LIMIT)
                    act = (lin + 1.0) * (gl * jax.nn.sigmoid(SWIGLU_ALPHA * gl))
                    rid = rows.start + lax.broadcasted_iota(jnp.int32, (m, 1), 0)
                    mine = jnp.logical_and(rid >= lo, rid < hi).astype(F32)
                    acc_blk[rows, :] += jnp.dot((act * mine).astype(BF16), wd_ref[0].astype(BF16),
                                                preferred_element_type=F32)

                    @pl.when(j == 0)
                    def _():
                        acc_blk[rows, :] += mine * bd_ref[0]

        @pl.when(jnp.logical_and(hi == r, j == nf - 1))
        def _():
            _start_row_scatter(orow_ref, acc_blk, y_hbm, sem_out.at[slot])

    @pl.when(jnp.logical_and(w == pl.num_programs(0) - 1, j == nf - 1))
    def _():
        for s in range(min(2, nblk)):
            _wait_row_scatter(acc.at[s], sem_out.at[s])


def _moe_experts(vis_blk, vis_exp, vis_lo, vis_hi, n_vis, slot_tok, out_row, xn3,
                 w_gate_up, b_gate_up, w_down, b_down):
    r, tf = MOE_R, MOE_TF
    n, slabs, _ = xn3.shape
    d = slabs * LANES
    nblk = slot_tok.shape[0]
    nf = D_FF // tf

    def tile(w, j, nv):
        w_eff = jnp.minimum(w, nv[0] - 1)
        j_eff = jnp.where(w < nv[0], j, nf - 1)
        return jnp.where(w_eff % 2 == 0, j_eff, nf - 1 - j_eff)

    idx_spec = lambda f: pl.BlockSpec((1, 1, r), lambda w, j, vb, ve, lo, hi, nv: (f(vb[w]), 0, 0),
                                      memory_space=pltpu.SMEM)
    vmem = (r * ROW_PITCH * LANES * 4 + r * d * 2 + 2 * r * d * 4 + 2 * 3 * d * tf * 4 + 3 * d * tf * 2
            + 3 * MOE_SUB * d * 4 + (6 << 20))
    return pl.pallas_call(
        functools.partial(_moe_kernel, nblk=nblk),
        grid_spec=pltpu.PrefetchScalarGridSpec(
            num_scalar_prefetch=5,
            grid=(vis_blk.shape[0], nf),
            in_specs=[idx_spec(lambda b: b), idx_spec(lambda b: jnp.minimum(b + 1, nblk - 1)), idx_spec(lambda b: b),
                      pl.BlockSpec(memory_space=pl.ANY),
                      pl.BlockSpec((1, d, tf), lambda w, j, vb, ve, lo, hi, nv: (ve[w], 0, tile(w, j, nv))),
                      pl.BlockSpec((1, d, tf), lambda w, j, vb, ve, lo, hi, nv: (ve[w], 0, nf + tile(w, j, nv))),
                      pl.BlockSpec((1, 1, tf), lambda w, j, vb, ve, lo, hi, nv: (ve[w], 0, tile(w, j, nv))),
                      pl.BlockSpec((1, 1, tf), lambda w, j, vb, ve, lo, hi, nv: (ve[w], 0, nf + tile(w, j, nv))),
                      pl.BlockSpec((1, tf, d), lambda w, j, vb, ve, lo, hi, nv: (ve[w], tile(w, j, nv), 0)),
                      pl.BlockSpec((1, 1, d), lambda w, j, vb, ve, lo, hi, nv: (ve[w], 0, 0))],
            out_specs=pl.BlockSpec(memory_space=pl.ANY),
            scratch_shapes=[pltpu.VMEM((r * ROW_PITCH, LANES), F32),
                            pltpu.VMEM((r, d), BF16),
                            pltpu.VMEM((2, r, d), F32),
                            pltpu.SemaphoreType.DMA, pltpu.SemaphoreType.DMA((2,))]),
        out_shape=jax.ShapeDtypeStruct((TOP_K * n, d), F32),
        compiler_params=_params(("arbitrary", "arbitrary"), vmem),
        name="moe_experts",
    )(vis_blk, vis_exp, vis_lo, vis_hi, n_vis, slot_tok, slot_tok, out_row, xn3,
      w_gate_up, w_gate_up, b_gate_up.reshape(N_EXPERTS, 1, 2 * D_FF),
      b_gate_up.reshape(N_EXPERTS, 1, 2 * D_FF), w_down, b_down.reshape(N_EXPERTS, 1, d))


def _final_kernel(h_ref, gate_ref, y0_ref, y1_ref, y2_ref, y3_ref, pa_ref, pb_ref, pp_ref, pnw_ref, pg_ref,
                  oa_ref, ob_ref, *, n_first):
    first = pl.program_id(0) < n_first
    h = h_ref[...]
    for kk, y_ref in enumerate((y0_ref, y1_ref, y2_ref, y3_ref)):
        h = h + gate_ref[:, kk:kk + 1] * y_ref[...]
    gate = jax.nn.sigmoid(jnp.dot(h.astype(BF16), pg_ref[...], preferred_element_type=F32))
    p = jnp.where(first, pa_ref[...], pb_ref[...])
    pe = jnp.dot(p.astype(BF16), pp_ref[...], preferred_element_type=F32)
    pe = pe * _rms_scale(pe) * pnw_ref[...]
    out = h + gate * pe

    @pl.when(first)
    def _():
        oa_ref[...] = out

    @pl.when(jnp.logical_not(first))
    def _():
        ob_ref[...] = out


def _final(y4, h1, gates, pa, pb, ple_proj_bf16, ple_norm_w, ple_gate_bf16):
    n, d = h1.shape
    tm = FIN_TM
    nt = n // tm
    n_first = pa.shape[0] // tm
    row = lambda i: (i, 0)
    const = lambda i: (0, 0)
    vmem = 2 * d * d * 2 + 20 * tm * d * 4 + (8 << 20)
    return pl.pallas_call(
        functools.partial(_final_kernel, n_first=n_first),
        grid=(nt,),
        in_specs=[pl.BlockSpec((tm, d), row),
                  pl.BlockSpec((tm, LANES), row)]
        + [pl.BlockSpec((tm, d), lambda i, kk=kk: (kk * nt + i, 0)) for kk in range(TOP_K)]
        + _pair_specs((tm, PLE_DIM), n_first)
        + [pl.BlockSpec((PLE_DIM, d), const),
           pl.BlockSpec((1, d), const),
           pl.BlockSpec((d, d), const)],
        out_specs=_pair_specs((tm, d), n_first),
        out_shape=[jax.ShapeDtypeStruct((pa.shape[0], d), F32), jax.ShapeDtypeStruct((pb.shape[0], d), F32)],
        compiler_params=_params(("arbitrary",), vmem),
        name="combine_final",
    )(h1, gates, y4, y4, y4, y4, pa, pb, ple_proj_bf16, ple_norm_w.reshape(1, d), ple_gate_bf16)


def _routing(eidx, counts, n_tokens):
    r = MOE_R
    nblk = (n_tokens * TOP_K) // r
    nvis_max = nblk + N_EXPERTS - 1
    order = jnp.argsort(eidx[:, :TOP_K].reshape(-1), stable=True).astype(jnp.int32)
    slot_tok = order // TOP_K
    out_row = (order % TOP_K) * n_tokens + slot_tok
    cnt = counts[0, :N_EXPERTS].astype(jnp.int32)
    ends = jnp.cumsum(cnt)
    starts = ends - cnt
    first_blk = starts // r
    n_blk = jnp.where(cnt > 0, (ends - 1) // r - first_blk + 1, 0)
    vis_end = jnp.cumsum(n_blk)
    n_vis = vis_end[-1]
    w = jnp.arange(nvis_max, dtype=jnp.int32)
    valid = w < n_vis
    e = jnp.minimum(jnp.sum((vis_end[None, :] <= w[:, None]).astype(jnp.int32), axis=1), N_EXPERTS - 1)
    e = jnp.where(valid, e, jnp.max(jnp.where(valid, e, 0)))
    blk = jnp.where(valid, first_blk[e] + w - (vis_end[e] - n_blk[e]), nblk - 1)
    lo = jnp.where(valid, jnp.clip(starts[e] - blk * r, 0, r), 0)
    hi = jnp.where(valid, jnp.clip(ends[e] - blk * r, 0, r), 0)
    i32 = lambda a: a.astype(jnp.int32)
    return (i32(blk), i32(e), i32(lo), i32(hi), i32(n_vis).reshape(1),
            slot_tok.reshape(nblk, 1, r), out_row.reshape(nblk, 1, r))


def _layer(xa, xb, pa, pb, groups, lb, mix_norm_w, w_in, hgrn_norm_w, q_norm_w, k_norm_w, w_out, ffn_norm_w,
           router_w, router_b, w_gate_up, b_gate_up, w_down, b_down, ple_proj, ple_norm_w, ple_gate):
    n = xa.shape[0] + xb.shape[0]
    proj = _in_proj(xa, xb, mix_norm_w, w_in.astype(BF16))
    att, k, v = _qk_rope(proj, q_norm_w, k_norm_w, groups)
    sums, masks = _hgrn_constants()
    o_dirs = _hgrn(proj, lb, sums, masks, groups)
    tok0 = 0
    for n_tok, t in groups:
        att = _attention(att, k, v, tok0, n_tok // t, t)
        tok0 += n_tok
    h1, xn3, eidx, gates, counts = _out_router(
        xa, xb, o_dirs, proj, att, hgrn_norm_w, w_out.astype(BF16), ffn_norm_w, router_w, router_b)
    y4 = _moe_experts(*_routing(eidx, counts, n), xn3, w_gate_up, b_gate_up, w_down, b_down)
    return _final(y4, h1, gates, pa, pb, ple_proj.astype(BF16), ple_norm_w, ple_gate.astype(BF16))


def kernel(x_prompt, x_sample, p_prompt, p_sample, mix_norm_w, w_in, hgrn_lb, hgrn_norm_w, q_norm_w, k_norm_w, w_out, ffn_norm_w, router_w, router_b, w_gate_up, b_gate_up, w_down, b_down, ple_proj, ple_norm_w, ple_gate):
    depth = w_in.shape[0]
    groups = ((x_prompt.shape[0] * x_prompt.shape[1], x_prompt.shape[1]),
              (x_sample.shape[0] * x_sample.shape[1], x_sample.shape[1]))
    d = x_prompt.shape[-1]
    ha, hb = x_prompt.reshape(-1, d), x_sample.reshape(-1, d)
    lb_all = jnp.cumsum(jax.nn.softmax(hgrn_lb.astype(F32), axis=0), axis=0)
    for i in range(depth):
        ha, hb = _layer(ha, hb, p_prompt[i].reshape(-1, PLE_DIM), p_sample[i].reshape(-1, PLE_DIM), groups,
                        lb_all[i], mix_norm_w[i], w_in[i], hgrn_norm_w[i], q_norm_w[i], k_norm_w[i],
                        w_out[i], ffn_norm_w[i], router_w[i], router_b[i], w_gate_up[i], b_gate_up[i],
                        w_down[i], b_down[i], ple_proj[i], ple_norm_w[i], ple_gate[i])
    return ha.reshape(x_prompt.shape), hb.reshape(x_sample.shape)
```

```python
import functools

import numpy as np
import jax
import jax.numpy as jnp
from jax import lax
from jax.experimental import pallas as pl
from jax.experimental.pallas import tpu as pltpu

F32 = jnp.float32
BF16 = jnp.bfloat16

D_MODEL = 2048
GRID_W = 64
HG_WIDTH = 1024
HG_HEAD_DIM = 128
HG_HEADS = HG_WIDTH // HG_HEAD_DIM
CHUNK = 64
HEAD_DIM = 128
N_Q_HEADS = 8
N_KV_HEADS = 2
Q_PER_KV = N_Q_HEADS // N_KV_HEADS
ROPE_THETA = 10000.0
ROPE_AXIS_PAIRS = HEAD_DIM // 4
N_EXPERTS = 32
TOP_K = 4
D_FF = D_MODEL
SWIGLU_LIMIT = 7.0
SWIGLU_ALPHA = 1.702
PLE_DIM = 256
NORM_EPS = 1e-6
Q_COLS = N_Q_HEADS * HEAD_DIM
KV_COLS = N_KV_HEADS * HEAD_DIM
IN_COLS = 5 * HG_WIDTH + Q_COLS + 2 * KV_COLS

LANES = 128
V7X_VMEM_BUDGET_BYTES = 56 * 1024 * 1024

IN_TM, IN_TN = 1024, 512
QK_TM = 1024
ATT_TQ = 512
ATT_TK = 1024
HG_TB = 512
HG_UNROLL = 4
OUT_TM = 256
MOE_R = 512
MOE_SUB = 128
MOE_TF = 512
FIN_TM = 256
ROW_CHUNK = 128
N_LEVELS = 6


def _params(sem, vmem_bytes):
    return pltpu.CompilerParams(dimension_semantics=sem,
                                vmem_limit_bytes=min(int(vmem_bytes), V7X_VMEM_BUDGET_BYTES))


def _rms_scale(x):
    return lax.rsqrt(jnp.mean(x * x, axis=-1, keepdims=True) + NORM_EPS)


def _dot_nt(a, b):
    return lax.dot_general(a, b, (((1,), (1,)), ((), ())), preferred_element_type=F32)


def _pair_specs(block, n_first):
    zeros = (0,) * (len(block) - 1)
    return [pl.BlockSpec(block, lambda i, *_: (jnp.minimum(i, n_first - 1),) + zeros),
            pl.BlockSpec(block, lambda i, *_: (jnp.maximum(i - n_first, 0),) + zeros)]


def _in_proj_kernel(xa_ref, xb_ref, nw_ref, w_ref, o_ref, xn_ref, *, n_first):
    @pl.when(pl.program_id(1) == 0)
    def _():
        first = pl.program_id(0) < n_first

        def body(c, carry):
            rows = pl.ds(pl.multiple_of(c * ROW_CHUNK, ROW_CHUNK), ROW_CHUNK)
            x = jnp.where(first, xa_ref[rows, :], xb_ref[rows, :])
            xn_ref[rows, :] = (x * _rms_scale(x) * nw_ref[...]).astype(BF16)
            return carry
        lax.fori_loop(0, xa_ref.shape[0] // ROW_CHUNK, body, 0)

    o_ref[...] = jnp.dot(xn_ref[...], w_ref[...], preferred_element_type=F32)


def _in_proj(xa, xb, norm_w, w_bf16):
    d = xa.shape[1]
    n = xa.shape[0] + xb.shape[0]
    cols = w_bf16.shape[1]
    tm, tn = IN_TM, IN_TN
    vmem = 4 * tm * d * 4 + tm * d * 2 + 2 * d * tn * 2 + 2 * tm * tn * 4 + (8 << 20)
    return pl.pallas_call(
        functools.partial(_in_proj_kernel, n_first=xa.shape[0] // tm),
        grid=(n // tm, cols // tn),
        in_specs=_pair_specs((tm, d), xa.shape[0] // tm) + [
            pl.BlockSpec((1, d), lambda i, j: (0, 0)),
            pl.BlockSpec((d, tn), lambda i, j: (0, j))],
        out_specs=pl.BlockSpec((tm, tn), lambda i, j: (i, j)),
        out_shape=jax.ShapeDtypeStruct((n, cols), F32),
        scratch_shapes=[pltpu.VMEM((tm, d), BF16)],
        compiler_params=_params(("parallel", "arbitrary"), vmem),
        name="in_proj",
    )(xa, xb, norm_w.reshape(1, d), w_bf16)


def _rope_tables(t_max):
    rows = t_max // GRID_W
    row = jnp.repeat(jnp.arange(rows, dtype=F32), GRID_W)
    col = jnp.tile(jnp.arange(GRID_W, dtype=F32), rows)
    inv_freq = ROPE_THETA ** (-jnp.arange(ROPE_AXIS_PAIRS, dtype=F32) / ROPE_AXIS_PAIRS)
    ang = jnp.concatenate([row[:, None] * inv_freq, col[:, None] * inv_freq], axis=-1)
    cos = jnp.repeat(jnp.cos(ang), 2, axis=-1)
    sin = jnp.repeat(jnp.sin(ang), 2, axis=-1)
    even = (jnp.arange(HEAD_DIM) % 2 == 0)[None, :]
    s_next = jnp.where(even, -sin, 0.0)
    s_prev = jnp.where(even, 0.0, sin)
    return cos, s_next, s_prev


def _qk_rope_kernel(q_ref, k_ref, v_ref, c_ref, sn_ref, sp_ref, qw_ref, kw_ref,
                    qo_ref, ko_ref, vo_ref):
    c, sn, sp = c_ref[...], sn_ref[...], sp_ref[...]

    def norm_rope(x, w):
        y = x * _rms_scale(x) * w
        return y * c + pltpu.roll(y, HEAD_DIM - 1, 1) * sn + pltpu.roll(y, 1, 1) * sp

    scale = HEAD_DIM ** -0.5
    for h in range(N_Q_HEADS):
        sl = slice(h * HEAD_DIM, (h + 1) * HEAD_DIM)
        qo_ref[:, sl] = (norm_rope(q_ref[:, sl], qw_ref[...]) * scale).astype(BF16)
    for h in range(N_KV_HEADS):
        sl = slice(h * HEAD_DIM, (h + 1) * HEAD_DIM)
        ko_ref[:, sl] = norm_rope(k_ref[:, sl], kw_ref[...]).astype(BF16)
    vo_ref[...] = v_ref[...].astype(BF16)


def _qk_rope(proj, q_norm_w, k_norm_w, groups):
    n = proj.shape[0]
    tm = QK_TM
    t_max = max(t for _, t in groups)
    cos, s_next, s_prev = _rope_tables(t_max)

    def pos_block(i):
        blk = jnp.int32(0)
        start = 0
        for n_tok, t in groups:
            first = start // tm
            blk = jnp.where(i >= first, (i - first) % (t // tm), blk)
            start += n_tok
        return blk

    tab_spec = pl.BlockSpec((tm, HEAD_DIM), lambda i: (pos_block(i), 0))
    q_blk, k_blk, v_blk = (5 * HG_WIDTH) // Q_COLS, (5 * HG_WIDTH + Q_COLS) // KV_COLS, \
        (5 * HG_WIDTH + Q_COLS + KV_COLS) // KV_COLS
    return pl.pallas_call(
        _qk_rope_kernel,
        grid=(n // tm,),
        in_specs=[pl.BlockSpec((tm, Q_COLS), lambda i: (i, q_blk)),
                  pl.BlockSpec((tm, KV_COLS), lambda i: (i, k_blk)),
                  pl.BlockSpec((tm, KV_COLS), lambda i: (i, v_blk)),
                  tab_spec, tab_spec, tab_spec,
                  pl.BlockSpec((1, HEAD_DIM), lambda i: (0, 0)),
                  pl.BlockSpec((1, HEAD_DIM), lambda i: (0, 0))],
        out_specs=[pl.BlockSpec((tm, Q_COLS), lambda i: (i, 0)),
                   pl.BlockSpec((tm, KV_COLS), lambda i: (i, 0)),
                   pl.BlockSpec((tm, KV_COLS), lambda i: (i, 0))],
        out_shape=[jax.ShapeDtypeStruct((n, Q_COLS), BF16),
                   jax.ShapeDtypeStruct((n, KV_COLS), BF16),
                   jax.ShapeDtypeStruct((n, KV_COLS), BF16)],
        compiler_params=_params(("parallel",), 32 << 20),
        name="qk_rope",
    )(proj, proj, proj, cos, s_next, s_prev,
      q_norm_w.reshape(1, HEAD_DIM), k_norm_w.reshape(1, HEAD_DIM))


def _attn_kernel(q_ref, k_ref, v_ref, o_ref):
    n_kv = k_ref.shape[0] // ATT_TK
    items = [(h, c) for h in range(Q_PER_KV) for c in range(n_kv)]

    def scores(item):
        h, c = item
        return _dot_nt(q_ref[:, h * HEAD_DIM:(h + 1) * HEAD_DIM], k_ref[c * ATT_TK:(c + 1) * ATT_TK, :])

    s = scores(items[0])
    m = l = acc = None
    for i, (h, c) in enumerate(items):
        s_next = scores(items[i + 1]) if i + 1 < len(items) else None
        vc = v_ref[c * ATT_TK:(c + 1) * ATT_TK, :]
        mc = jnp.max(s, axis=-1, keepdims=True)
        if c == 0:
            m = mc
            p = jnp.exp(s - m)
            l = jnp.sum(p, axis=-1, keepdims=True)
            acc = jnp.dot(p.astype(BF16), vc, preferred_element_type=F32)
        else:
            m_new = jnp.maximum(m, mc)
            alpha = jnp.exp(m - m_new)
            p = jnp.exp(s - m_new)
            l = alpha * l + jnp.sum(p, axis=-1, keepdims=True)
            acc = alpha * acc + jnp.dot(p.astype(BF16), vc, preferred_element_type=F32)
            m = m_new
        if c == n_kv - 1:
            o_ref[:, h * HEAD_DIM:(h + 1) * HEAD_DIM] = (acc / l).astype(BF16)
        s = s_next


def _attention(q, k, v, tok0, batch, t):
    tq = ATT_TQ
    nq = t // tq
    row0, seq0 = tok0 // tq, tok0 // t
    gcols = Q_PER_KV * HEAD_DIM
    vmem = 4 * t * HEAD_DIM * 2 + 3 * tq * t * 4 + (8 << 20)
    q_spec = pl.BlockSpec((tq, gcols), lambda b, g, i: (row0 + b * nq + i, g))
    kv_spec = pl.BlockSpec((t, HEAD_DIM), lambda b, g, i: (seq0 + b, g))
    return pl.pallas_call(
        _attn_kernel,
        grid=(batch, N_KV_HEADS, nq),
        in_specs=[q_spec, kv_spec, kv_spec],
        out_specs=q_spec,
        out_shape=jax.ShapeDtypeStruct(q.shape, BF16),
        input_output_aliases={0: 0},
        compiler_params=_params(("parallel", "parallel", "arbitrary"), vmem),
        name="attention",
    )(q, k, v)


def _hgrn_constants():
    c = CHUNK
    t = np.arange(c)[:, None]
    u = np.arange(c)[None, :]
    sums = [(u <= t), (u > t)]
    masks = []
    for lvl in range(N_LEVELS):
        half = (c // 2) >> lvl
        bound = (t // (2 * half)) * (2 * half) + half - 1
        late = t > bound
        sums.append(np.where(late, (u > bound) & (u <= t), (u > t) & (u <= bound)))
        same = (t // (2 * half)) == (u // (2 * half))
        masks.append(same & late & (u <= bound))
    masks.append(t == u)
    sums = np.stack(sums).astype(np.float32)
    masks = np.stack(masks).astype(np.float32)
    both_s = np.stack([sums, sums[:, ::-1, ::-1]]).reshape(2, -1, c)
    both_s = np.concatenate([both_s, both_s], axis=-1)
    both_m = np.stack([masks, masks[:, ::-1, ::-1]]).reshape(2, -1, c)
    return jnp.asarray(both_s, BF16), jnp.asarray(both_m, F32)


def _hgrn_kernel(q_ref, f_ref, v_ref, lb_ref, sums_ref, masks_ref, o_ref, st_ref, *, nchunk, nblk,
                 seq_first_blk, seq_last_blk):
    d = pl.program_id(0)
    i = pl.program_id(1)
    blk = i + d * (nblk - 1 - 2 * i)
    fresh = functools.reduce(
        jnp.logical_or,
        [jnp.logical_or(jnp.logical_and(d == 0, blk == f), jnp.logical_and(d == 1, blk == l))
         for f, l in zip(seq_first_blk, seq_last_blk)])

    @pl.when(fresh)
    def _():
        st_ref[...] = jnp.zeros_like(st_ref)

    lb = lb_ref[0]
    sums = sums_ref[0]
    c = CHUNK

    def chunk_body(j, carry):
        jj = j + d * (nchunk - 1 - 2 * j)
        rows = pl.ds(pl.multiple_of(jj * c, c), c)
        f = lb + (1.0 - lb) * jax.nn.sigmoid(f_ref[rows, :])
        k = 1.0 - f
        lf = jnp.log(f)
        qpre = q_ref[rows, :]
        q = qpre * jax.nn.sigmoid(qpre)
        vb = v_ref[rows, :].astype(BF16)
        lf_hi = lf.astype(BF16)
        lf_lo = (lf - lf_hi.astype(F32)).astype(BF16)
        e = jnp.exp(jnp.dot(sums, jnp.concatenate([lf_hi, lf_lo], axis=0),
                            preferred_element_type=F32))
        g = jnp.exp(jnp.sum(lf, axis=0, keepdims=True))
        qt = (q * e[0:c]).astype(BF16)
        kd = (k * e[c:2 * c]).astype(BF16)
        ql = [q.astype(BF16)] + [(q * e[(2 + lvl) * c:(3 + lvl) * c]).astype(BF16) for lvl in range(N_LEVELS)]
        kl = [k.astype(BF16)] + [(k * e[(2 + lvl) * c:(3 + lvl) * c]).astype(BF16) for lvl in range(N_LEVELS)]
        mk = [masks_ref[0, N_LEVELS * c:(N_LEVELS + 1) * c, :]] + [
            masks_ref[0, lvl * c:(lvl + 1) * c, :] for lvl in range(N_LEVELS)]
        heads = [slice(h * HG_HEAD_DIM, (h + 1) * HG_HEAD_DIM) for h in range(HG_HEADS)]
        scores = []
        for sl in heads:
            a = mk[0] * _dot_nt(ql[0][:, sl], kl[0][:, sl])
            for lvl in range(1, N_LEVELS + 1):
                a = a + mk[lvl] * _dot_nt(ql[lvl][:, sl], kl[lvl][:, sl])
            scores.append(a.astype(BF16))
        for h, sl in enumerate(heads):
            st = st_ref[h]
            o = _dot_nt(qt[:, sl], st.astype(BF16)) + jnp.dot(scores[h], vb[:, sl], preferred_element_type=F32)
            ut = lax.dot_general(vb[:, sl], kd[:, sl], (((0,), (0,)), ((), ())), preferred_element_type=F32)
            st_ref[h] = st * g[:, sl] + ut
            o_ref[0, rows, sl] = o
        return carry

    lax.fori_loop(0, nchunk, chunk_body, 0, unroll=HG_UNROLL)


def _hgrn(proj, lb, sums, masks, groups):
    n = proj.shape[0]
    tb = HG_TB
    nblk = n // tb
    first_blk, last_blk, tok = [], [], 0
    for n_tok, t in groups:
        for s in range(n_tok // t):
            first_blk.append((tok + s * t) // tb)
            last_blk.append((tok + (s + 1) * t) // tb - 1)
        tok += n_tok

    def rows(d, i):
        return i + d * (nblk - 1 - 2 * i)

    in_blk = lambda col: pl.BlockSpec((tb, HG_WIDTH), lambda d, i: (rows(d, i), col(d)))
    return pl.pallas_call(
        functools.partial(_hgrn_kernel, nchunk=tb // CHUNK, nblk=nblk,
                          seq_first_blk=tuple(first_blk), seq_last_blk=tuple(last_blk)),
        grid=(2, nblk),
        in_specs=[in_blk(lambda d: 0), in_blk(lambda d: 1 + d), in_blk(lambda d: 3),
                  pl.BlockSpec((1, 1, HG_WIDTH), lambda d, i: (d, 0, 0)),
                  pl.BlockSpec((1,) + sums.shape[1:], lambda d, i: (d, 0, 0)),
                  pl.BlockSpec((1,) + masks.shape[1:], lambda d, i: (d, 0, 0))],
        out_specs=pl.BlockSpec((1, tb, HG_WIDTH), lambda d, i: (d, rows(d, i), 0)),
        out_shape=jax.ShapeDtypeStruct((2, n, HG_WIDTH), F32),
        scratch_shapes=[pltpu.VMEM((HG_HEADS, HG_HEAD_DIM, HG_HEAD_DIM), F32)],
        compiler_params=_params(("parallel", "arbitrary"), 40 << 20),
        name="hgrn2",
    )(proj, proj, proj, lb.reshape(2, 1, HG_WIDTH), sums, masks)


def _out_router_kernel(xa_ref, xb_ref, of_ref, ob_ref, g_ref, att_ref, hw_ref, wo_ref, fw_ref, rw_ref, rb_ref,
                       h_ref, xn_ref, eidx_ref, gate_ref, cnt_ref, cnt_acc, *, n_first):
    @pl.when(pl.program_id(0) == 0)
    def _():
        cnt_acc[...] = jnp.zeros_like(cnt_acc)

    o = of_ref[0] + ob_ref[0]
    parts = []
    for h in range(HG_HEADS):
        sl = slice(h * HG_HEAD_DIM, (h + 1) * HG_HEAD_DIM)
        oh = o[:, sl]
        gh = g_ref[:, sl]
        parts.append(((oh * _rms_scale(oh) * hw_ref[...]) * (gh * jax.nn.sigmoid(gh))).astype(BF16))
    oa = jnp.concatenate(parts, axis=1)
    h1 = (jnp.where(pl.program_id(0) < n_first, xa_ref[...], xb_ref[...])
          + jnp.dot(oa, wo_ref[0:HG_WIDTH, :], preferred_element_type=F32)
          + jnp.dot(att_ref[...], wo_ref[HG_WIDTH:, :], preferred_element_type=F32))
    h_ref[...] = h1
    xn = h1 * _rms_scale(h1) * fw_ref[...]
    for cb in range(xn_ref.shape[1]):
        xn_ref[:, cb, :] = xn[:, cb * LANES:(cb + 1) * LANES]

    xn_hi = xn.astype(BF16)
    xn_lo = (xn - xn_hi.astype(F32)).astype(BF16)
    logits = (jnp.dot(xn_hi, rw_ref[0], preferred_element_type=F32)
              + (jnp.dot(xn_lo, rw_ref[0], preferred_element_type=F32)
                 + jnp.dot(xn_hi, rw_ref[1], preferred_element_type=F32))) + rb_ref[...]
    tm = logits.shape[0]
    lane = lax.broadcasted_iota(jnp.int32, (tm, LANES), 1)
    work = logits
    vals, idxs = [], []
    for _ in range(TOP_K):
        m = jnp.max(work, axis=-1, keepdims=True)
        idx = jnp.min(jnp.where(work == m, lane, LANES), axis=-1, keepdims=True)
        vals.append(m)
        idxs.append(idx)
        work = jnp.where(lane == idx, -jnp.inf, work)
    exps = [jnp.exp(v - vals[0]) for v in vals]
    denom = exps[0] + exps[1] + exps[2] + exps[3]

    onehot = jnp.zeros((tm, LANES), F32)
    for idx in idxs:
        onehot = onehot + (lane == idx).astype(F32)
    eidx = jnp.zeros((tm, LANES), jnp.int32)
    gate = jnp.zeros((tm, LANES), F32)
    for kk in range(TOP_K):
        eidx = jnp.where(lane == kk, idxs[kk], eidx)
        gate = jnp.where(lane == kk, exps[kk] / denom, gate)
    eidx_ref[...] = eidx
    gate_ref[...] = gate
    cnt_acc[...] = cnt_acc[...] + jnp.sum(onehot, axis=0, keepdims=True)
    cnt_ref[...] = cnt_acc[...]


def _out_router(xa, xb, o_dirs, proj, att, hgrn_norm_w, w_out_bf16, ffn_norm_w, router_w, router_b):
    d = xa.shape[1]
    n = xa.shape[0] + xb.shape[0]
    tm = OUT_TM
    rw = jnp.zeros((d, LANES), F32).at[:, :N_EXPERTS].set(router_w.astype(F32))
    rw_hi = rw.astype(BF16)
    rw = jnp.stack([rw_hi, (rw - rw_hi.astype(F32)).astype(BF16)])
    rb = jnp.full((1, LANES), -1e30, F32).at[0, :N_EXPERTS].set(router_b.astype(F32))
    g_blk = 4
    row = lambda i: (i, 0)
    const = lambda i: (0, 0)
    vmem = 2 * d * d * 2 + 12 * tm * d * 4 + (12 << 20)
    return pl.pallas_call(
        functools.partial(_out_router_kernel, n_first=xa.shape[0] // tm),
        grid=(n // tm,),
        in_specs=_pair_specs((tm, d), xa.shape[0] // tm) + [
                  pl.BlockSpec((1, tm, HG_WIDTH), lambda i: (0, i, 0)),
                  pl.BlockSpec((1, tm, HG_WIDTH), lambda i: (1, i, 0)),
                  pl.BlockSpec((tm, HG_WIDTH), lambda i: (i, g_blk)),
                  pl.BlockSpec((tm, Q_COLS), row),
                  pl.BlockSpec((1, HG_HEAD_DIM), const),
                  pl.BlockSpec((d, d), const),
                  pl.BlockSpec((1, d), const),
                  pl.BlockSpec((2, d, LANES), lambda i: (0, 0, 0)),
                  pl.BlockSpec((1, LANES), const)],
        out_specs=[pl.BlockSpec((tm, d), row), pl.BlockSpec((tm, d // LANES, LANES), lambda i: (i, 0, 0)),
                   pl.BlockSpec((tm, LANES), row), pl.BlockSpec((tm, LANES), row),
                   pl.BlockSpec((1, LANES), const)],
        out_shape=[jax.ShapeDtypeStruct((n, d), F32), jax.ShapeDtypeStruct((n, d // LANES, LANES), F32),
                   jax.ShapeDtypeStruct((n, LANES), jnp.int32),
                   jax.ShapeDtypeStruct((n, LANES), F32), jax.ShapeDtypeStruct((1, LANES), F32)],
        scratch_shapes=[pltpu.VMEM((1, LANES), F32)],
        compiler_params=_params(("arbitrary",), vmem),
        name="out_router",
    )(xa, xb, o_dirs, o_dirs, proj, att, hgrn_norm_w.reshape(1, HG_HEAD_DIM), w_out_bf16,
      ffn_norm_w.reshape(1, d), rw, rb)


ROW_PITCH = 20
GATHER_UNROLL = 8
ROW_COPY_PRIORITY = 1


def _start_row_gather(idx_ref, n_rows, src_hbm, buf, sem):
    slabs = src_hbm.shape[1]

    def issue(g, carry):
        base = pl.multiple_of(g * GATHER_UNROLL, GATHER_UNROLL)
        for u in range(GATHER_UNROLL):
            r = base + u
            pltpu.make_async_copy(src_hbm.at[idx_ref[0, 0, r]], buf.at[pl.ds(r * ROW_PITCH, slabs), :],
                                  sem).start(priority=ROW_COPY_PRIORITY)
        return carry
    lax.fori_loop(0, n_rows // GATHER_UNROLL, issue, 0)


def _wait_row_gather(n_rows, slabs, buf, sem):
    done = buf.at[pl.ds(0, n_rows * slabs), :]
    pltpu.make_async_copy(done, done, sem).wait()


def _gathered_cols(buf, row0, n_rows, cb):
    return buf[pl.ds(row0 * ROW_PITCH + cb, n_rows, stride=ROW_PITCH), :]


def _start_row_scatter(idx_ref, src, dst_hbm, sem):
    def issue(g, carry):
        base = pl.multiple_of(g * GATHER_UNROLL, GATHER_UNROLL)
        for u in range(GATHER_UNROLL):
            r = base + u
            pltpu.make_async_copy(src.at[pl.ds(r, 1), :], dst_hbm.at[pl.ds(idx_ref[0, 0, r], 1), :],
                                  sem).start(priority=ROW_COPY_PRIORITY)
        return carry
    lax.fori_loop(0, src.shape[0] // GATHER_UNROLL, issue, 0)


def _wait_row_scatter(src, sem):
    pltpu.make_async_copy(src, src, sem).wait()


def _moe_kernel(vb_ref, ve_ref, vlo_ref, vhi_ref, nvis_ref,
                tok_ref, tok_next_ref, orow_ref, xn_hbm, wg_ref, wl_ref, bg_ref, bl_ref, wd_ref, bd_ref,
                y_hbm, raw, xb, acc, sem_in, sem_out, *, nblk):
    w = pl.program_id(0)
    j = pl.program_id(1)
    nf = pl.num_programs(1)
    r = xb.shape[0]
    slabs = xn_hbm.shape[1]
    valid = w < nvis_ref[0]
    blk, lo, hi = vb_ref[w], vlo_ref[w], vhi_ref[w]
    slot = lax.rem(blk, 2)
    acc_blk = acc.at[slot]

    @pl.when(jnp.logical_and(valid, jnp.logical_and(lo == 0, j == 0)))
    def _():
        @pl.when(w == 0)
        def _():
            _start_row_gather(tok_ref, r, xn_hbm, raw, sem_in)
        _wait_row_gather(r, slabs, raw, sem_in)
        for cb in range(slabs):
            xb[:, cb * LANES:(cb + 1) * LANES] = _gathered_cols(raw, 0, r, cb).astype(BF16)

        @pl.when(blk + 1 < nblk)
        def _():
            _start_row_gather(tok_next_ref, r, xn_hbm, raw, sem_in)

        @pl.when(blk >= 2)
        def _():
            _wait_row_scatter(acc_blk, sem_out.at[slot])
        acc_blk[...] = jnp.zeros(acc_blk.shape, F32)

    @pl.when(valid)
    def _():
        sub_lo = lax.shift_right_logical(lo, MOE_SUB.bit_length() - 1)
        sub_hi = lax.shift_right_logical(hi + (MOE_SUB - 1), MOE_SUB.bit_length() - 1)
        n_sub = r // MOE_SUB
        for s0 in range(n_sub):
            for s1 in range(s0 + 1, n_sub + 1):
                rows = slice(s0 * MOE_SUB, s1 * MOE_SUB)

                @pl.when(jnp.logical_and(sub_lo == s0, sub_hi == s1))
                def _(rows=rows):
                    m = rows.stop - rows.start
                    xs = xb[rows, :]
                    gl = jnp.dot(xs, wg_ref[0].astype(BF16), preferred_element_type=F32) + bg_ref[0]
                    lin = jnp.dot(xs, wl_ref[0].astype(BF16), preferred_element_type=F32) + bl_ref[0]
                    gl = jnp.minimum(gl, SWIGLU_LIMIT)
                    lin = jnp.clip(lin, -SWIGLU_LIMIT, SWIGLU_LIMIT)
                    act = (lin + 1.0) * (gl * jax.nn.sigmoid(SWIGLU_ALPHA * gl))
                    rid = rows.start + lax.broadcasted_iota(jnp.int32, (m, 1), 0)
                    mine = jnp.logical_and(rid >= lo, rid < hi).astype(F32)
                    acc_blk[rows, :] += jnp.dot((act * mine).astype(BF16), wd_ref[0].astype(BF16),
                                                preferred_element_type=F32)

                    @pl.when(j == 0)
                    def _():
                        acc_blk[rows, :] += mine * bd_ref[0]

        @pl.when(jnp.logical_and(hi == r, j == nf - 1))
        def _():
            _start_row_scatter(orow_ref, acc_blk, y_hbm, sem_out.at[slot])

    @pl.when(jnp.logical_and(w == pl.num_programs(0) - 1, j == nf - 1))
    def _():
        for s in range(min(2, nblk)):
            _wait_row_scatter(acc.at[s], sem_out.at[s])


def _moe_experts(vis_blk, vis_exp, vis_lo, vis_hi, n_vis, slot_tok, out_row, xn3,
                 w_gate_up, b_gate_up, w_down, b_down):
    r, tf = MOE_R, MOE_TF
    n, slabs, _ = xn3.shape
    d = slabs * LANES
    nblk = slot_tok.shape[0]
    nf = D_FF // tf

    def tile(w, j, nv):
        w_eff = jnp.minimum(w, nv[0] - 1)
        j_eff = jnp.where(w < nv[0], j, nf - 1)
        return jnp.where(w_eff % 2 == 0, j_eff, nf - 1 - j_eff)

    idx_spec = lambda f: pl.BlockSpec((1, 1, r), lambda w, j, vb, ve, lo, hi, nv: (f(vb[w]), 0, 0),
                                      memory_space=pltpu.SMEM)
    vmem = (r * ROW_PITCH * LANES * 4 + r * d * 2 + 2 * r * d * 4 + 2 * 3 * d * tf * 4 + 3 * d * tf * 2
            + 3 * MOE_SUB * d * 4 + (6 << 20))
    return pl.pallas_call(
        functools.partial(_moe_kernel, nblk=nblk),
        grid_spec=pltpu.PrefetchScalarGridSpec(
            num_scalar_prefetch=5,
            grid=(vis_blk.shape[0], nf),
            in_specs=[idx_spec(lambda b: b), idx_spec(lambda b: jnp.minimum(b + 1, nblk - 1)), idx_spec(lambda b: b),
                      pl.BlockSpec(memory_space=pl.ANY),
                      pl.BlockSpec((1, d, tf), lambda w, j, vb, ve, lo, hi, nv: (ve[w], 0, tile(w, j, nv))),
                      pl.BlockSpec((1, d, tf), lambda w, j, vb, ve, lo, hi, nv: (ve[w], 0, nf + tile(w, j, nv))),
                      pl.BlockSpec((1, 1, tf), lambda w, j, vb, ve, lo, hi, nv: (ve[w], 0, tile(w, j, nv))),
                      pl.BlockSpec((1, 1, tf), lambda w, j, vb, ve, lo, hi, nv: (ve[w], 0, nf + tile(w, j, nv))),
                      pl.BlockSpec((1, tf, d), lambda w, j, vb, ve, lo, hi, nv: (ve[w], tile(w, j, nv), 0)),
                      pl.BlockSpec((1, 1, d), lambda w, j, vb, ve, lo, hi, nv: (ve[w], 0, 0))],
            out_specs=pl.BlockSpec(memory_space=pl.ANY),
            scratch_shapes=[pltpu.VMEM((r * ROW_PITCH, LANES), F32),
                            pltpu.VMEM((r, d), BF16),
                            pltpu.VMEM((2, r, d), F32),
                            pltpu.SemaphoreType.DMA, pltpu.SemaphoreType.DMA((2,))]),
        out_shape=jax.ShapeDtypeStruct((TOP_K * n, d), F32),
        compiler_params=_params(("arbitrary", "arbitrary"), vmem),
        name="moe_experts",
    )(vis_blk, vis_exp, vis_lo, vis_hi, n_vis, slot_tok, slot_tok, out_row, xn3,
      w_gate_up, w_gate_up, b_gate_up.reshape(N_EXPERTS, 1, 2 * D_FF),
      b_gate_up.reshape(N_EXPERTS, 1, 2 * D_FF), w_down, b_down.reshape(N_EXPERTS, 1, d))


def _final_kernel(h_ref, gate_ref, y0_ref, y1_ref, y2_ref, y3_ref, pa_ref, pb_ref, pp_ref, pnw_ref, pg_ref,
                  oa_ref, ob_ref, *, n_first):
    first = pl.program_id(0) < n_first
    h = h_ref[...]
    for kk, y_ref in enumerate((y0_ref, y1_ref, y2_ref, y3_ref)):
        h = h + gate_ref[:, kk:kk + 1] * y_ref[...]
    gate = jax.nn.sigmoid(jnp.dot(h.astype(BF16), pg_ref[...], preferred_element_type=F32))
    p = jnp.where(first, pa_ref[...], pb_ref[...])
    pe = jnp.dot(p.astype(BF16), pp_ref[...], preferred_element_type=F32)
    pe = pe * _rms_scale(pe) * pnw_ref[...]
    out = h + gate * pe

    @pl.when(first)
    def _():
        oa_ref[...] = out

    @pl.when(jnp.logical_not(first))
    def _():
        ob_ref[...] = out


def _final(y4, h1, gates, pa, pb, ple_proj_bf16, ple_norm_w, ple_gate_bf16):
    n, d = h1.shape
    tm = FIN_TM
    nt = n // tm
    n_first = pa.shape[0] // tm
    row = lambda i: (i, 0)
    const = lambda i: (0, 0)
    vmem = 2 * d * d * 2 + 20 * tm * d * 4 + (8 << 20)
    return pl.pallas_call(
        functools.partial(_final_kernel, n_first=n_first),
        grid=(nt,),
        in_specs=[pl.BlockSpec((tm, d), row),
                  pl.BlockSpec((tm, LANES), row)]
        + [pl.BlockSpec((tm, d), lambda i, kk=kk: (kk * nt + i, 0)) for kk in range(TOP_K)]
        + _pair_specs((tm, PLE_DIM), n_first)
        + [pl.BlockSpec((PLE_DIM, d), const),
           pl.BlockSpec((1, d), const),
           pl.BlockSpec((d, d), const)],
        out_specs=_pair_specs((tm, d), n_first),
        out_shape=[jax.ShapeDtypeStruct((pa.shape[0], d), F32), jax.ShapeDtypeStruct((pb.shape[0], d), F32)],
        compiler_params=_params(("arbitrary",), vmem),
        name="combine_final",
    )(h1, gates, y4, y4, y4, y4, pa, pb, ple_proj_bf16, ple_norm_w.reshape(1, d), ple_gate_bf16)


def _routing(eidx, counts, n_tokens):
    r = MOE_R
    nblk = (n_tokens * TOP_K) // r
    nvis_max = nblk + N_EXPERTS - 1
    order = jnp.argsort(eidx[:, :TOP_K].reshape(-1), stable=True).astype(jnp.int32)
    slot_tok = order // TOP_K
    out_row = (order % TOP_K) * n_tokens + slot_tok
    cnt = counts[0, :N_EXPERTS].astype(jnp.int32)
    ends = jnp.cumsum(cnt)
    starts = ends - cnt
    first_blk = starts // r
    n_blk = jnp.where(cnt > 0, (ends - 1) // r - first_blk + 1, 0)
    vis_end = jnp.cumsum(n_blk)
    n_vis = vis_end[-1]
    w = jnp.arange(nvis_max, dtype=jnp.int32)
    valid = w < n_vis
    e = jnp.minimum(jnp.sum((vis_end[None, :] <= w[:, None]).astype(jnp.int32), axis=1), N_EXPERTS - 1)
    e = jnp.where(valid, e, jnp.max(jnp.where(valid, e, 0)))
    blk = jnp.where(valid, first_blk[e] + w - (vis_end[e] - n_blk[e]), nblk - 1)
    lo = jnp.where(valid, jnp.clip(starts[e] - blk * r, 0, r), 0)
    hi = jnp.where(valid, jnp.clip(ends[e] - blk * r, 0, r), 0)
    i32 = lambda a: a.astype(jnp.int32)
    return (i32(blk), i32(e), i32(lo), i32(hi), i32(n_vis).reshape(1),
            slot_tok.reshape(nblk, 1, r), out_row.reshape(nblk, 1, r))


def _layer(xa, xb, pa, pb, groups, lb, mix_norm_w, w_in, hgrn_norm_w, q_norm_w, k_norm_w, w_out, ffn_norm_w,
           router_w, router_b, w_gate_up, b_gate_up, w_down, b_down, ple_proj, ple_norm_w, ple_gate):
    n = xa.shape[0] + xb.shape[0]
    proj = _in_proj(xa, xb, mix_norm_w, w_in.astype(BF16))
    att, k, v = _qk_rope(proj, q_norm_w, k_norm_w, groups)
    sums, masks = _hgrn_constants()
    o_dirs = _hgrn(proj, lb, sums, masks, groups)
    tok0 = 0
    for n_tok, t in groups:
        att = _attention(att, k, v, tok0, n_tok // t, t)
        tok0 += n_tok
    h1, xn3, eidx, gates, counts = _out_router(
        xa, xb, o_dirs, proj, att, hgrn_norm_w, w_out.astype(BF16), ffn_norm_w, router_w, router_b)
    y4 = _moe_experts(*_routing(eidx, counts, n), xn3, w_gate_up, b_gate_up, w_down, b_down)
    return _final(y4, h1, gates, pa, pb, ple_proj.astype(BF16), ple_norm_w, ple_gate.astype(BF16))


def kernel(x_prompt, x_sample, p_prompt, p_sample, mix_norm_w, w_in, hgrn_lb, hgrn_norm_w, q_norm_w, k_norm_w, w_out, ffn_norm_w, router_w, router_b, w_gate_up, b_gate_up, w_down, b_down, ple_proj, ple_norm_w, ple_gate):
    depth = w_in.shape[0]
    groups = ((x_prompt.shape[0] * x_prompt.shape[1], x_prompt.shape[1]),
              (x_sample.shape[0] * x_sample.shape[1], x_sample.shape[1]))
    d = x_prompt.shape[-1]
    ha, hb = x_prompt.reshape(-1, d), x_sample.reshape(-1, d)
    lb_all = jnp.cumsum(jax.nn.softmax(hgrn_lb.astype(F32), axis=0), axis=0)
    for i in range(depth):
        ha, hb = _layer(ha, hb, p_prompt[i].reshape(-1, PLE_DIM), p_sample[i].reshape(-1, PLE_DIM), groups,
                        lb_all[i], mix_norm_w[i], w_in[i], hgrn_norm_w[i], q_norm_w[i], k_norm_w[i],
                        w_out[i], ffn_norm_w[i], router_w[i], router_b[i], w_gate_up[i], b_gate_up[i],
                        w_down[i], b_down[i], ple_proj[i], ple_norm_w[i], ple_gate[i])
    return ha.reshape(x_prompt.shape), hb.reshape(x_sample.shape)
```
